```python
import jax, jax.numpy as jnp
from jax import lax
import numpy as np


D_MODEL = 1024
BATCH = 1
SEQ = 16384
DEPTH = 4

EPS = 1e-6
N_MEM = 256
MLSTM_HEADS = 4
MLSTM_DH = 128
MLSTM_W = MLSTM_HEADS * MLSTM_DH
MLSTM_CHUNK = 128
CONV_W = 4
M_INIT = -1e30
DIL_PATTERNS = ((128, 1), (512, 4), (2048, 16))
DIL_HEADS_PER_GROUP = 4
DIL_DH = 64
DIL_W = 3 * DIL_HEADS_PER_GROUP * DIL_DH
DIL_OUT_W = DIL_HEADS_PER_GROUP * DIL_DH
Q_BLOCK = 128
MEM_HEADS = 4
MEM_DH = 128
MEM_W = MEM_HEADS * MEM_DH
N_BRANCH = 3
IN_SPLITS = (MLSTM_W, MLSTM_W, MLSTM_W, MLSTM_W, 2 * MLSTM_HEADS, DIL_W, DIL_W, DIL_W, MEM_W, N_BRANCH * D_MODEL)
IN_COLS = 4 * MLSTM_W + 2 * MLSTM_HEADS + 3 * DIL_W + MEM_W + N_BRANCH * D_MODEL
D_FF = 11 * D_MODEL // 4
N_EXPERTS = 8
TOP_K = 2
D_FF_EXPERT = 7 * D_MODEL // 2
MOE_TOKEN_BLOCK = 128
N_DENSE = (DEPTH + 1) // 2
N_MOE = DEPTH // 2

kernel_name = 'hybrid_mlstm_dilated_memory_moe'


def _rmsnorm(x, g):
    xf = x.astype(jnp.float32)
    y = xf * lax.rsqrt(jnp.mean(xf * xf, axis=-1, keepdims=True) + EPS)
    return (y * g.astype(jnp.float32)).astype(x.dtype)


def _headwise_layernorm(h, g):
    mu = jnp.mean(h, axis=-1, keepdims=True)
    var = jnp.mean(jnp.square(h - mu), axis=-1, keepdims=True)
    return (h - mu) * lax.rsqrt(var + EPS) * g.astype(jnp.float32).reshape(h.shape[-2], h.shape[-1])


def _causal_depthwise_conv(x, w):
    c = x.shape[-1]
    return lax.conv_general_dilated(x, w[:, None, :].astype(x.dtype), window_strides=(1,), padding=((CONV_W - 1, 0),), dimension_numbers=('NWC', 'WIO', 'NWC'), feature_group_count=c)


def _mlstm_chunkwise(q, k, v, i_pre, f_pre):
    b, s, h, dh = q.shape
    L = MLSTM_CHUNK
    nc = s // L
    k = k * (dh ** -0.5)
    log_f = jax.nn.log_sigmoid(f_pre)

    def seq_chunks(a):
        return a.reshape(b, nc, L, h, dh).transpose(1, 0, 3, 2, 4)

    def gate_chunks(a):
        return a.reshape(b, nc, L, h).transpose(1, 0, 3, 2)

    causal = jnp.tril(jnp.ones((L, L), dtype=bool))

    def step(carry, xs):
        c_state, n_state, m_state = carry
        qc, kc, vc, ic, lfc = xs
        cum = jnp.cumsum(lfc, axis=-1)
        d_mat = cum[..., :, None] - cum[..., None, :] + ic[..., None, :]
        d_mat = jnp.where(causal, d_mat, -jnp.inf)
        inter = cum + m_state[..., None]
        m_row = jnp.maximum(jnp.max(d_mat, axis=-1), inter)
        w_intra = jnp.exp(d_mat - m_row[..., None])
        w_inter = jnp.exp(inter - m_row)
        s_mat = jnp.einsum('bhld,bhjd->bhlj', qc, kc) * w_intra
        num = jnp.einsum('bhlj,bhjd->bhld', s_mat, vc) + w_inter[..., None] * jnp.einsum('bhvd,bhld->bhlv', c_state, qc)
        den = jnp.sum(s_mat, axis=-1) + w_inter * jnp.einsum('bhd,bhld->bhl', n_state, qc)
        h_out = num / jnp.maximum(jnp.abs(den), jnp.exp(-m_row))[..., None]
        total = cum[..., -1]
        g_end = total[..., None] - cum + ic
        m_new = jnp.maximum(total + m_state, jnp.max(g_end, axis=-1))
        w_end = jnp.exp(g_end - m_new[..., None])
        decay = jnp.exp(total + m_state - m_new)
        c_new = decay[..., None, None] * c_state + jnp.einsum('bhlv,bhld->bhvd', vc * w_end[..., None], kc)
        n_new = decay[..., None] * n_state + jnp.einsum('bhl,bhld->bhd', w_end, kc)
        return (c_new, n_new, m_new), h_out

    init = (jnp.zeros((b, h, dh, dh), jnp.float32), jnp.zeros((b, h, dh), jnp.float32), jnp.full((b, h), M_INIT, jnp.float32))
    _, hs = lax.scan(step, init, (seq_chunks(q), seq_chunks(k), seq_chunks(v), gate_chunks(i_pre), gate_chunks(log_f)))
    return hs.transpose(1, 0, 3, 2, 4).reshape(b, s, h, dh)


def _dilated_attention(q, k, v):
    b, s, _ = q.shape
    g_n = len(DIL_PATTERNS)
    hg, dh = DIL_HEADS_PER_GROUP, DIL_DH
    q = q.reshape(b, s, g_n, hg, dh)
    k = k.reshape(b, s, g_n, hg, dh)
    v = v.reshape(b, s, g_n, hg, dh)
    nb = s // Q_BLOCK
    q_blocks = q.reshape(b, nb, Q_BLOCK, g_n, hg, dh).transpose(1, 0, 2, 3, 4, 5)
    scale = dh ** -0.5

    def block(args):
        q_blk, start = args
        pos = start + jnp.arange(Q_BLOCK)
        outs, lses = [], []
        for g, (window, dil) in enumerate(DIL_PATTERNS):
            offs = jnp.arange(window // dil + 1) * dil
            idx = pos[:, None] - offs[None, :]
            valid = idx >= 0
            idx = jnp.maximum(idx, 0)
            kg = jnp.take(k[:, :, g], idx, axis=1)
            vg = jnp.take(v[:, :, g], idx, axis=1)
            sc = jnp.einsum('bqhd,bqjhd->bhqj', q_blk[:, :, g], kg) * scale
            sc = jnp.where(valid[None, None], sc, -jnp.inf)
            mx = jnp.max(sc, axis=-1, keepdims=True)
            p = jnp.exp(sc - mx)
            den = jnp.sum(p, axis=-1)
            o = jnp.einsum('bhqj,bqjhd->bqhd', p, vg) / den.transpose(0, 2, 1)[..., None]
            lse = (mx[..., 0] + jnp.log(den)).transpose(0, 2, 1)
            outs.append(o)
            lses.append(lse)
        o_all = jnp.stack(outs, axis=2)
        w = jax.nn.softmax(jnp.stack(lses, axis=2), axis=2)
        return jnp.einsum('bqgh,bqghd->bqhd', w, o_all)

    out = lax.map(block, (q_blocks, jnp.arange(nb) * Q_BLOCK))
    return out.transpose(1, 0, 2, 3, 4).reshape(b, s, hg * dh)


def _memory_attention(q, mem_n, w_kv):
    b, s, _ = q.shape
    kv = (mem_n @ w_kv).astype(jnp.float32)
    km, vm = jnp.split(kv, 2, axis=-1)
    km = km.reshape(b, -1, MEM_HEADS, MEM_DH)
    vm = vm.reshape(b, -1, MEM_HEADS, MEM_DH)
    qh = q.reshape(b, s, MEM_HEADS, MEM_DH)
    sc = jnp.einsum('bshd,bnhd->bhsn', qh, km) * (MEM_DH ** -0.5)
    p = jax.nn.softmax(sc, axis=-1)
    return jnp.einsum('bhsn,bnhd->bshd', p, vm).reshape(b, s, MEM_W)


def _swiglu(u, wg, wu, wd):
    return (jax.nn.silu(u @ wg) * (u @ wu)) @ wd


def _moe_swiglu(u, w_router, wg, wu, wd):
    b, s, d = u.shape
    ut = u.reshape(-1, d)
    t = ut.shape[0]
    probs = jax.nn.softmax((ut @ w_router).astype(jnp.float32), axis=-1)
    top_p, top_i = lax.top_k(probs, TOP_K)
    top_p = top_p / jnp.sum(top_p, axis=-1, keepdims=True)
    gates = jnp.sum(jax.nn.one_hot(top_i, N_EXPERTS, dtype=jnp.float32) * top_p[..., None], axis=1).astype(u.dtype)
    nblk = t // MOE_TOKEN_BLOCK

    def block(args):
        xb, gb = args
        h = jax.nn.silu(jnp.einsum('td,edf->tef', xb, wg)) * jnp.einsum('td,edf->tef', xb, wu)
        return jnp.einsum('tef,efd->td', h * gb[..., None], wd)

    y = lax.map(block, (ut.reshape(nblk, MOE_TOKEN_BLOCK, d), gates.reshape(nblk, MOE_TOKEN_BLOCK, N_EXPERTS)))
    return y.reshape(b, s, d)


def setup_inputs(seed: int = 0) -> dict:
    key = jax.random.key(seed)
    ks = jax.random.split(key, 24)

    def nrm(k, shape, scale):
        return jax.random.normal(k, shape, jnp.float32) * scale

    def gain(k, shape):
        return 1.0 + 0.02 * jax.random.normal(k, shape, jnp.float32)

    b_i = nrm(ks[4], (DEPTH, MLSTM_HEADS), 0.1)
    b_f = 3.0 + nrm(ks[5], (DEPTH, MLSTM_HEADS), 0.5)
    return {
        'x': nrm(ks[0], (BATCH, SEQ, D_MODEL), 1.0),
        'mem': nrm(ks[1], (BATCH, N_MEM, D_MODEL), 1.0),
        'norm_mix': gain(ks[2], (DEPTH, D_MODEL)),
        'w_in': nrm(ks[3], (DEPTH, D_MODEL, IN_COLS), D_MODEL ** -0.5),
        'conv_qk': nrm(ks[6], (DEPTH, CONV_W, 2 * MLSTM_W), CONV_W ** -0.5),
        'b_gate_if': jnp.concatenate([b_i, b_f], axis=-1),
        'mlstm_norm': gain(ks[7], (DEPTH, MLSTM_W)),
        'norm_mem': gain(ks[8], (DEPTH, D_MODEL)),
        'w_mem_kv': nrm(ks[9], (DEPTH, D_MODEL, 2 * MEM_W), D_MODEL ** -0.5),
        'w_br_m': nrm(ks[10], (DEPTH, MLSTM_W, D_MODEL), MLSTM_W ** -0.5),
        'w_br_d': nrm(ks[11], (DEPTH, DIL_OUT_W, D_MODEL), DIL_OUT_W ** -0.5),
        'w_br_x': nrm(ks[12], (DEPTH, MEM_W, D_MODEL), MEM_W ** -0.5),
        'w_out': nrm(ks[13], (DEPTH, D_MODEL, D_MODEL), D_MODEL ** -0.5),
        'norm_ffn': gain(ks[14], (DEPTH, D_MODEL)),
        'ffn_w_gate': nrm(ks[15], (N_DENSE, D_MODEL, D_FF), D_MODEL ** -0.5),
        'ffn_w_up': nrm(ks[16], (N_DENSE, D_MODEL, D_FF), D_MODEL ** -0.5),
        'ffn_w_down': nrm(ks[17], (N_DENSE, D_FF, D_MODEL), D_FF ** -0.5),
        'moe_router': nrm(ks[18], (N_MOE, D_MODEL, N_EXPERTS), D_MODEL ** -0.5),
        'moe_w_gate': nrm(ks[19], (N_MOE, N_EXPERTS, D_MODEL, D_FF_EXPERT), D_MODEL ** -0.5),
        'moe_w_up': nrm(ks[20], (N_MOE, N_EXPERTS, D_MODEL, D_FF_EXPERT), D_MODEL ** -0.5),
        'moe_w_down': nrm(ks[21], (N_MOE, N_EXPERTS, D_FF_EXPERT, D_MODEL), D_FF_EXPERT ** -0.5),
        'norm_final': gain(ks[22], (D_MODEL,)),
    }


def reference(x, mem, norm_mix, w_in, conv_qk, b_gate_if, mlstm_norm, norm_mem, w_mem_kv, w_br_m, w_br_d, w_br_x, w_out, norm_ffn, ffn_w_gate, ffn_w_up, ffn_w_down, moe_router, moe_w_gate, moe_w_up, moe_w_down, norm_final):
    b, s, _ = x.shape
    split_points = [int(p) for p in np.cumsum(IN_SPLITS)[:-1]]
    for layer in range(DEPTH):
        u = _rmsnorm(x, norm_mix[layer])
        proj = u @ w_in[layer]
        q_m, k_m, v_m, o_m, if_m, q_d, k_d, v_d, q_x, gate_pre = jnp.split(proj, split_points, axis=-1)
        qk = jax.nn.silu(_causal_depthwise_conv(jnp.concatenate([q_m, k_m], axis=-1), conv_qk[layer]))
        q_m, k_m = jnp.split(qk, 2, axis=-1)
        if_pre = (if_m + b_gate_if[layer]).astype(jnp.float32)
        h_m = _mlstm_chunkwise(q_m.astype(jnp.float32).reshape(b, s, MLSTM_HEADS, MLSTM_DH), k_m.astype(jnp.float32).reshape(b, s, MLSTM_HEADS, MLSTM_DH), v_m.astype(jnp.float32).reshape(b, s, MLSTM_HEADS, MLSTM_DH), if_pre[..., :MLSTM_HEADS], if_pre[..., MLSTM_HEADS:])
        h_m = (_headwise_layernorm(h_m, mlstm_norm[layer]).reshape(b, s, MLSTM_W) * jax.nn.sigmoid(o_m.astype(jnp.float32))).astype(x.dtype)
        h_d = _dilated_attention(q_d.astype(jnp.float32), k_d.astype(jnp.float32), v_d.astype(jnp.float32)).astype(x.dtype)
        mem_n = _rmsnorm(mem, norm_mem[layer])
        h_x = _memory_attention(q_x.astype(jnp.float32), mem_n, w_mem_kv[layer]).astype(x.dtype)
        g_m, g_d, g_x = jnp.split(jax.nn.sigmoid(gate_pre), N_BRANCH, axis=-1)
        merged = g_m * (h_m @ w_br_m[layer]) + g_d * (h_d @ w_br_d[layer]) + g_x * (h_x @ w_br_x[layer])
        x = x + merged @ w_out[layer]
        u = _rmsnorm(x, norm_ffn[layer])
        if layer % 2 == 0:
            x = x + _swiglu(u, ffn_w_gate[layer // 2], ffn_w_up[layer // 2], ffn_w_down[layer // 2])
        else:
            x = x + _moe_swiglu(u, moe_router[layer // 2], moe_w_gate[layer // 2], moe_w_up[layer // 2], moe_w_down[layer // 2])
    return _rmsnorm(x, norm_final)
```

```python
import functools

import jax
import jax.numpy as jnp
from jax import lax
from jax.experimental import pallas as pl
from jax.experimental.pallas import tpu as pltpu

F32 = jnp.float32
BF16 = jnp.bfloat16

EPS = 1e-6
D_MODEL = 1024
DEPTH = 4
N_MEM = 256
MLSTM_HEADS = 4
MLSTM_DH = 128
MLSTM_W = MLSTM_HEADS * MLSTM_DH
MLSTM_CHUNK = 128
CONV_W = 4
M_INIT = -1e30
DIL_PATTERNS = ((128, 1), (512, 4), (2048, 16))
DIL_HEADS = 4
DIL_DH = 64
DIL_GW = DIL_HEADS * DIL_DH
DIL_W = 3 * DIL_GW
Q_BLOCK = 128
MEM_HEADS = 4
MEM_DH = 128
MEM_W = MEM_HEADS * MEM_DH
D_FF = 2816
N_EXPERTS = 8
D_FF_EXPERT = 3584

OFF_IF = 4 * MLSTM_W
OFF_QD = OFF_IF + 2 * MLSTM_HEADS
OFF_KD = OFF_QD + DIL_W
OFF_VD = OFF_KD + DIL_W
OFF_QX = OFF_VD + DIL_W
OFF_GATE = OFF_QX + MEM_W
IN_COLS = OFF_GATE + 3 * D_MODEL

LANES = 128
NEG = -1e30
VMEM_LIMIT = 56 * 1024 * 1024

TOK_TILE = 512
FF_CHUNK = 256
MOE_TT = 256
MOE_BM = 512
MOE_TF = 1792

NT_DIMS = (((1,), (1,)), ((), ()))
TN_DIMS = (((0,), (0,)), ((), ()))


def _params(*sem):
    return pltpu.CompilerParams(dimension_semantics=sem, vmem_limit_bytes=VMEM_LIMIT)


def _dot(a, b):
    return jnp.dot(a, b, preferred_element_type=F32)


def _dot_nt(a, b):
    return lax.dot_general(a, b, NT_DIMS, preferred_element_type=F32)


def _rms(x, g):
    return x * lax.rsqrt(jnp.mean(x * x, axis=-1, keepdims=True) + EPS) * g


def _split3(x):
    hi = x.astype(BF16)
    r1 = x - hi.astype(F32)
    mid = r1.astype(BF16)
    lo = (r1 - mid.astype(F32)).astype(BF16)
    return hi, mid, lo


def _const_spec(shape):
    nd = len(shape)
    return pl.BlockSpec(shape, lambda *_: (0,) * nd, pipeline_mode=pl.Buffered(1))


def _memkv_kernel(mem_ref, g_ref, w_ref, k_ref, v_ref):
    u = _rms(mem_ref[...], g_ref[...]).astype(BF16)
    kv = _dot(u, w_ref[...])
    k_ref[...] = kv[:, :MEM_W].astype(BF16)
    v_ref[...] = kv[:, MEM_W:].astype(BF16)


def _memkv(mem, g, w_kv):
    return pl.pallas_call(
        _memkv_kernel,
        out_shape=(jax.ShapeDtypeStruct((N_MEM, MEM_W), BF16),) * 2,
        compiler_params=pltpu.CompilerParams(vmem_limit_bytes=VMEM_LIMIT),
    )(mem, g, w_kv)


def _inproj_kernel(x_ref, g_ref, wa_ref, wif_ref, wift_ref, bif_ref, bift_ref, cw_ref,
                   wd_ref, wqx_ref, wg_ref, km_ref, vm_ref,
                   q_out, k_out, v_out, o_out, if_out, ift_out, qd_out, kd_out, vd_out,
                   hx_out, gate_out, conv_buf):
    tm = x_ref.shape[0]
    u = _rms(x_ref[...], g_ref[...]).astype(BF16)

    @pl.when(pl.program_id(0) == 0)
    def _():
        conv_buf[0:8, :] = jnp.zeros((8, 2 * MLSTM_W), F32)

    conv_buf[8:tm + 8, :] = _dot(u, wa_ref[:, 0:2 * MLSTM_W])
    acc = cw_ref[0:1, :] * conv_buf[pl.ds(8 - (CONV_W - 1), tm), :]
    for j in range(1, CONV_W):
        acc = acc + cw_ref[j:j + 1, :] * conv_buf[pl.ds(8 - (CONV_W - 1) + j, tm), :]
    conv_buf[0:8, :] = conv_buf[tm:tm + 8, :]
    qk = acc * jax.nn.sigmoid(acc)
    q_out[...] = qk[:, :MLSTM_W].astype(BF16)
    k_out[...] = (qk[:, MLSTM_W:] * (MLSTM_DH ** -0.5)).astype(BF16)

    v_out[...] = _dot(u, wa_ref[:, 2 * MLSTM_W:3 * MLSTM_W]).astype(BF16)
    o_out[...] = jax.nn.sigmoid(_dot(u, wa_ref[:, 3 * MLSTM_W:4 * MLSTM_W])).astype(BF16)

    if_out[...] = _dot(u, wif_ref[...]) + bif_ref[...]
    ift_out[...] = _dot_nt(wift_ref[...], u) + bift_ref[...]

    d = _dot(u, wd_ref[...])
    qd_out[...] = d[:, 0:DIL_GW].astype(BF16)
    kd_out[...] = d[:, DIL_GW:2 * DIL_GW].astype(BF16)
    vd_out[...] = d[:, 2 * DIL_GW:].astype(BF16)

    qx = (_dot(u, wqx_ref[...]) * (MEM_DH ** -0.5)).astype(BF16)
    outs = []
    for h in range(MEM_HEADS):
        sl = slice(h * MEM_DH, (h + 1) * MEM_DH)
        s = _dot_nt(qx[:, sl], km_ref[:, sl])
        p = jnp.exp(s - jnp.max(s, axis=-1, keepdims=True))
        den = jnp.sum(p, axis=-1, keepdims=True)
        outs.append(_dot(p.astype(BF16), vm_ref[:, sl]) / den)
    hx_out[...] = jnp.concatenate(outs, axis=-1).astype(BF16)

    gate_out[...] = jax.nn.sigmoid(_dot(u, wg_ref[...])).astype(BF16)


def _inproj(x, g, wa, wif, wift, bif, bift, cw, wd0, wqx, wgate, km, vm):
    s = x.shape[0]
    tm = TOK_TILE
    row = lambda w: pl.BlockSpec((tm, w), lambda i: (i, 0))
    out_shape = (
        jax.ShapeDtypeStruct((s, MLSTM_W), BF16),
        jax.ShapeDtypeStruct((s, MLSTM_W), BF16),
        jax.ShapeDtypeStruct((s, MLSTM_W), BF16),
        jax.ShapeDtypeStruct((s, MLSTM_W), BF16),
        jax.ShapeDtypeStruct((s, LANES), F32),
        jax.ShapeDtypeStruct((8, s), F32),
        jax.ShapeDtypeStruct((s, DIL_GW), BF16),
        jax.ShapeDtypeStruct((s, DIL_GW), BF16),
        jax.ShapeDtypeStruct((s, DIL_GW), BF16),
        jax.ShapeDtypeStruct((s, MEM_W), BF16),
        jax.ShapeDtypeStruct((s, 3 * D_MODEL), BF16),
    )
    out_specs = (row(MLSTM_W), row(MLSTM_W), row(MLSTM_W), row(MLSTM_W), row(LANES),
                 pl.BlockSpec((8, tm), lambda i: (0, i)),
                 row(DIL_GW), row(DIL_GW), row(DIL_GW), row(MEM_W), row(3 * D_MODEL))
    in_specs = [row(D_MODEL)] + [_const_spec(a.shape) for a in
                                 (g, wa, wif, wift, bif, bift, cw, wd0, wqx, wgate, km, vm)]
    return pl.pallas_call(
        _inproj_kernel,
        grid=(s // tm,),
        in_specs=in_specs,
        out_specs=out_specs,
        out_shape=out_shape,
        scratch_shapes=[pltpu.VMEM((tm + 8, 2 * MLSTM_W), F32)],
        compiler_params=_params("arbitrary"),
    )(x, g, wa, wif, wift, bif, bift, cw, wd0, wqx, wgate, km, vm)


def _perm_proj_kernel(x_ref, g_ref, w_ref, q_out, k_out, v_out):
    u = _rms(x_ref[...], g_ref[...]).astype(BF16)
    d = _dot(u, w_ref[...])
    q_out[...] = d[:, 0:DIL_GW].astype(BF16)
    k_out[...] = d[:, DIL_GW:2 * DIL_GW].astype(BF16)
    v_out[...] = d[:, 2 * DIL_GW:].astype(BF16)


def _perm_proj(x, g, w, dil):
    s = x.shape[0]
    ls = s // dil
    tm = min(TOK_TILE, ls)
    nt = ls // tm
    xv = x.reshape(ls, dil * D_MODEL)
    out = pl.BlockSpec((tm, DIL_GW), lambda r, i: (r * nt + i, 0))
    return pl.pallas_call(
        _perm_proj_kernel,
        grid=(dil, nt),
        in_specs=[pl.BlockSpec((tm, D_MODEL), lambda r, i: (i, r)),
                  _const_spec(g.shape), _const_spec(w.shape)],
        out_specs=(out, out, out),
        out_shape=(jax.ShapeDtypeStruct((s, DIL_GW), BF16),) * 3,
        compiler_params=_params("arbitrary", "arbitrary"),
    )(xv, g, w)


def _log_sigmoid(x):
    return jnp.minimum(x, 0.0) - jnp.log(1.0 + jnp.exp(-jnp.abs(x)))


def _mlstm_kernel(q_ref, k_ref, v_ref, o_ref, ifc_ref, ifr_ref, g_ref, out_ref, ct_ref, m_ref):
    L = MLSTM_CHUNK
    H = MLSTM_HEADS

    @pl.when(pl.program_id(0) == 0)
    def _():
        ct_ref[...] = jnp.zeros(ct_ref.shape, F32)
        m_ref[...] = jnp.full(m_ref.shape, M_INIT, F32)

    row = lax.broadcasted_iota(jnp.int32, (L, L), 0)
    col = lax.broadcasted_iota(jnp.int32, (L, L), 1)
    causal = col <= row
    tril = jnp.where(causal, 1.0, 0.0).astype(BF16)
    triu = jnp.where(row <= col, 1.0, 0.0).astype(BF16)
    ones_col = jnp.where(col == 0, 1.0, 0.0).astype(BF16)

    ifc = ifc_ref[...]
    ifr = ifr_ref[...]
    cum_c = sum(_dot(tril, p) for p in _split3(_log_sigmoid(ifc)))
    cum_r = sum(_dot(p, triu) for p in _split3(_log_sigmoid(ifr)))

    for h in range(H):
        sl = slice(h * MLSTM_DH, (h + 1) * MLSTM_DH)
        i_c = ifc[:, h:h + 1]
        i_r = ifr[h:h + 1, :]
        cc = cum_c[:, H + h:H + h + 1]
        cr = cum_r[H + h:H + h + 1, :]
        total = cr[:, L - 1:L]
        m_prev = m_ref[h:h + 1, 0:1]

        dm = jnp.where(causal, cc - cr + i_r, -jnp.inf)
        inter = cc + m_prev
        m_row = jnp.maximum(jnp.max(dm, axis=-1, keepdims=True), inter)
        w_intra = jnp.exp(dm - m_row)
        w_inter = jnp.exp(inter - m_row)

        qh = q_ref[:, sl]
        kh = k_ref[:, sl]
        vaug = jnp.concatenate([v_ref[:, sl], ones_col], axis=-1)
        s_mat = _dot_nt(qh, kh) * w_intra
        tot = _dot(s_mat.astype(BF16), vaug) + w_inter * _dot(qh, ct_ref[h].astype(BF16))
        den = tot[:, MLSTM_DH:MLSTM_DH + 1]
        h_out = tot[:, :MLSTM_DH] / jnp.maximum(jnp.abs(den), jnp.exp(-m_row))

        g_end = total - cc + i_c
        m_new = jnp.maximum(total + m_prev, jnp.max(g_end, axis=0, keepdims=True))
        w_end = jnp.exp(g_end - m_new)
        decay = jnp.exp(total + m_prev - m_new)
        vw = (vaug.astype(F32) * w_end).astype(BF16)
        ct_ref[h] = decay * ct_ref[h] + lax.dot_general(kh, vw, TN_DIMS, preferred_element_type=F32)
        m_ref[h:h + 1, :] = jnp.broadcast_to(m_new, (1, LANES))

        mu = jnp.mean(h_out, axis=-1, keepdims=True)
        cen = h_out - mu
        var = jnp.mean(cen * cen, axis=-1, keepdims=True)
        y = cen * lax.rsqrt(var + EPS) * g_ref[:, sl] * o_ref[:, sl].astype(F32)
        out_ref[:, sl] = y.astype(BF16)


def _mlstm(q, k, v, o, ifc, ifr, g):
    s = q.shape[0]
    L = MLSTM_CHUNK
    row = pl.BlockSpec((L, MLSTM_W), lambda c: (c, 0))
    return pl.pallas_call(
        _mlstm_kernel,
        grid=(s // L,),
        in_specs=[row, row, row, row,
                  pl.BlockSpec((L, LANES), lambda c: (c, 0)),
                  pl.BlockSpec((8, L), lambda c: (0, c)),
                  _const_spec(g.shape)],
        out_specs=row,
        out_shape=jax.ShapeDtypeStruct((s, MLSTM_W), BF16),
        scratch_shapes=[pltpu.VMEM((MLSTM_HEADS, MLSTM_DH, 2 * MLSTM_DH), F32),
                        pltpu.VMEM((8, LANES), F32)],
        compiler_params=_params("arbitrary"),
    )(q, k, v, o, ifc, ifr, g)


def _band_attn_kernel(q_ref, kc_ref, kp_ref, vc_ref, vp_ref, o_ref, lse_ref):
    B = Q_BLOCK
    row = lax.broadcasted_iota(jnp.int32, (B, B), 0)
    col = lax.broadcasted_iota(jnp.int32, (B, B), 1)
    cur_ok = col <= row
    prev_ok = col >= row
    prev_bias = jnp.where(pl.program_id(1) > 0, 0.0, NEG)
    outs, lses = [], []
    for h in range(DIL_HEADS):
        sl = slice(h * DIL_DH, (h + 1) * DIL_DH)
        q = q_ref[:, sl]
        sc = jnp.where(cur_ok, _dot_nt(q, kc_ref[:, sl]), NEG)
        sp = jnp.where(prev_ok, _dot_nt(q, kp_ref[:, sl]), NEG) + prev_bias
        mx = jnp.maximum(jnp.max(sc, axis=-1, keepdims=True), jnp.max(sp, axis=-1, keepdims=True))
        pc = jnp.exp(sc - mx)
        pp = jnp.exp(sp - mx)
        den = jnp.sum(pc, axis=-1, keepdims=True) + jnp.sum(pp, axis=-1, keepdims=True)
        o = (_dot(pc.astype(BF16), vc_ref[:, sl]) + _dot(pp.astype(BF16), vp_ref[:, sl])) / den
        outs.append(o)
        lses.append(jnp.broadcast_to(mx + jnp.log(den), (B, DIL_DH)))
    o_ref[...] = jnp.concatenate(outs, axis=-1)
    lse_ref[...] = jnp.concatenate(lses, axis=-1)


def _band_attn(q, k, v, dil):
    s = q.shape[0]
    B = Q_BLOCK
    nt = s // dil // B
    cur = pl.BlockSpec((B, DIL_GW), lambda r, t: (r * nt + t, 0))
    prev = pl.BlockSpec((B, DIL_GW), lambda r, t: (r * nt + jnp.maximum(t - 1, 0), 0))
    out = pl.BlockSpec((B, DIL_GW), lambda r, t: (t, r))
    o, lse = pl.pallas_call(
        _band_attn_kernel,
        grid=(dil, nt),
        in_specs=[cur, cur, prev, cur, prev],
        out_specs=(out, out),
        out_shape=(jax.ShapeDtypeStruct((s // dil, dil * DIL_GW), F32),) * 2,
        compiler_params=_params("arbitrary", "arbitrary"),
    )(q, k, k, v, v)
    return o.reshape(s, DIL_GW), lse.reshape(s, DIL_GW)


def _merge_core(x_ref, hm_ref, hx_ref, gate_ref, od_refs, lse_refs, wm_ref, wdd_ref, wx_ref, wo_ref):
    lses = [r[...] for r in lse_refs]
    mx = jnp.maximum(jnp.maximum(lses[0], lses[1]), lses[2])
    es = [jnp.exp(l - mx) for l in lses]
    den = es[0] + es[1] + es[2]
    hd = (es[0] * od_refs[0][...] + es[1] * od_refs[1][...] + es[2] * od_refs[2][...]) / den
    d = D_MODEL
    merged = (gate_ref[:, 0:d].astype(F32) * _dot(hm_ref[...], wm_ref[...])
              + gate_ref[:, d:2 * d].astype(F32) * _dot(hd.astype(BF16), wdd_ref[...])
              + gate_ref[:, 2 * d:3 * d].astype(F32) * _dot(hx_ref[...], wx_ref[...]))
    return x_ref[...] + _dot(merged.astype(BF16), wo_ref[...])


def _merge_dense_kernel(x_ref, hm_ref, hx_ref, gate_ref, o0, o1, o2, l0, l1, l2,
                        wm_ref, wdd_ref, wx_ref, wo_ref, gf_ref, wg_ref, wu_ref, wdn_ref,
                        out_ref, acc_ref):
    x1 = _merge_core(x_ref, hm_ref, hx_ref, gate_ref, (o0, o1, o2), (l0, l1, l2),
                     wm_ref, wdd_ref, wx_ref, wo_ref)
    u = _rms(x1, gf_ref[...]).astype(BF16)
    acc_ref[...] = x1

    def body(c, carry):
        g = _dot(u, wg_ref[c])
        hcol = (g * jax.nn.sigmoid(g) * _dot(u, wu_ref[c])).astype(BF16)
        acc_ref[...] += _dot(hcol, wdn_ref[c])
        return carry

    lax.fori_loop(0, wg_ref.shape[0], body, 0)
    out_ref[...] = acc_ref[...]


def _merge_moe_kernel(x_ref, hm_ref, hx_ref, gate_ref, o0, o1, o2, l0, l1, l2,
                      wm_ref, wdd_ref, wx_ref, wo_ref, gf_ref, wr_ref,
                      x1_out, u_out, route_out, cnt_out, carry_ref):
    tm = x_ref.shape[0]

    @pl.when(pl.program_id(0) == 0)
    def _():
        carry_ref[...] = jnp.zeros(carry_ref.shape, F32)

    x1 = _merge_core(x_ref, hm_ref, hx_ref, gate_ref, (o0, o1, o2), (l0, l1, l2),
                     wm_ref, wdd_ref, wx_ref, wo_ref)
    x1_out[...] = x1
    uf = _rms(x1, gf_ref[...])
    u_out[...] = uf.astype(BF16)

    uh, um, ul = _split3(uf)
    wh, wmid, wl = _split3(wr_ref[...])
    logits = (_dot(uh, wh) + (_dot(uh, wmid) + _dot(um, wh))
              + (_dot(uh, wl) + _dot(um, wmid) + _dot(ul, wh)))
    lane = lax.broadcasted_iota(jnp.int32, (tm, LANES), 1).astype(F32)
    valid = lane < N_EXPERTS
    lg = jnp.where(valid, logits, NEG)
    ex = jnp.exp(lg - jnp.max(lg, axis=-1, keepdims=True))
    probs = jnp.where(valid, ex / jnp.sum(ex, axis=-1, keepdims=True), -1.0)
    p1 = jnp.max(probs, axis=-1, keepdims=True)
    i1 = jnp.min(jnp.where(probs == p1, lane, float(LANES)), axis=-1, keepdims=True)
    rest = jnp.where(lane == i1, -1.0, probs)
    p2 = jnp.max(rest, axis=-1, keepdims=True)
    i2 = jnp.min(jnp.where(rest == p2, lane, float(LANES)), axis=-1, keepdims=True)
    g1 = p1 / (p1 + p2)
    g2 = p2 / (p1 + p2)
    sel = jnp.where(lane == i1, 1.0, jnp.where(lane == i2, 1.0, 0.0))
    row = lax.broadcasted_iota(jnp.int32, (tm, tm), 0)
    col = lax.broadcasted_iota(jnp.int32, (tm, tm), 1)
    before = jnp.where(col < row, 1.0, 0.0).astype(BF16)
    ranks = _dot(before, sel.astype(BF16)) + carry_ref[0:1, :]
    r1 = jnp.sum(jnp.where(lane == i1, ranks, 0.0), axis=-1, keepdims=True)
    r2 = jnp.sum(jnp.where(lane == i2, ranks, 0.0), axis=-1, keepdims=True)
    carry_ref[...] = carry_ref[...] + jnp.sum(sel, axis=0, keepdims=True)
    cnt_out[...] = carry_ref[...]
    route = jnp.where(lane == 0, i1, jnp.where(lane == 1, i2, jnp.where(lane == 2, g1,
            jnp.where(lane == 3, g2, jnp.where(lane == 4, r1, jnp.where(lane == 5, r2, 0.0))))))
    route_out[...] = route


def _merge(x, hm, hx, gates, ods, lses, wm, wdd, wx, wo, gf, dense_w=None, w_router=None):
    s = x.shape[0]
    tm = TOK_TILE
    row = lambda w: pl.BlockSpec((tm, w), lambda i: (i, 0))
    acts = (x, hm, hx, gates) + tuple(ods) + tuple(lses)
    act_specs = [row(D_MODEL), row(MLSTM_W), row(MEM_W), row(3 * D_MODEL)] + [row(DIL_GW)] * 6
    if dense_w is not None:
        consts = (wm, wdd, wx, wo, gf) + tuple(dense_w)
        return pl.pallas_call(
            _merge_dense_kernel,
            grid=(s // tm,),
            in_specs=act_specs + [_const_spec(c.shape) for c in consts],
            out_specs=row(D_MODEL),
            out_shape=jax.ShapeDtypeStruct((s, D_MODEL), F32),
            scratch_shapes=[pltpu.VMEM((tm, D_MODEL), F32)],
            compiler_params=_params("arbitrary"),
        )(*acts, *consts)
    consts = (wm, wdd, wx, wo, gf, w_router)
    return pl.pallas_call(
        _merge_moe_kernel,
        grid=(s // tm,),
        in_specs=act_specs + [_const_spec(c.shape) for c in consts],
        out_specs=(row(D_MODEL), row(D_MODEL), row(LANES), pl.BlockSpec((8, LANES), lambda i: (0, 0))),
        out_shape=(jax.ShapeDtypeStruct((s, D_MODEL), F32),
                   jax.ShapeDtypeStruct((s, D_MODEL), BF16),
                   jax.ShapeDtypeStruct((s, LANES), F32),
                   jax.ShapeDtypeStruct((8, LANES), F32)),
        scratch_shapes=[pltpu.VMEM((8, LANES), F32)],
        compiler_params=_params("arbitrary"),
    )(*acts, *consts)


FLAG_VALID, FLAG_FIRST, FLAG_LAST = 1, 2, 4


def _dispatch_kernel(wb_ref, wi_ref, wf_ref, u_ref, p1_ref, p2_ref, g1_ref, g2_ref,
                     xs_ref, gs_ref, acc_ref, gacc_ref):
    w = pl.program_id(0)
    flags = wf_ref[w]
    bm, tt = xs_ref.shape[0], u_ref.shape[0]

    @pl.when((flags & FLAG_FIRST) != 0)
    def _():
        acc_ref[...] = jnp.zeros(acc_ref.shape, F32)
        gacc_ref[...] = jnp.zeros(gacc_ref.shape, F32)

    @pl.when((flags & FLAG_VALID) != 0)
    def _():
        rows = wb_ref[w] * bm + lax.broadcasted_iota(jnp.int32, (bm, tt), 0)
        e1 = rows == p1_ref[...]
        e2 = rows == p2_ref[...]
        pick = jnp.where(e1, 1.0, jnp.where(e2, 1.0, 0.0)).astype(BF16)
        acc_ref[...] += _dot(pick, u_ref[...])
        gw = jnp.where(e1, g1_ref[...], jnp.where(e2, g2_ref[...], 0.0))
        gacc_ref[...] += jnp.sum(gw, axis=-1, keepdims=True)

    @pl.when((flags & FLAG_LAST) != 0)
    def _():
        xs_ref[...] = acc_ref[...].astype(BF16)
        gs_ref[...] = jnp.broadcast_to(gacc_ref[...], gs_ref.shape)


def _dispatch(u, p1r, p2r, g1r, g2r, wb, wi, wf, n_rows):
    tt, bm = MOE_TT, MOE_BM
    tok = pl.BlockSpec((None, 1, tt), lambda w, wb, wi, wf: (wi[w], 0, 0))
    grid_spec = pltpu.PrefetchScalarGridSpec(
        num_scalar_prefetch=3,
        grid=(wb.shape[0],),
        in_specs=[pl.BlockSpec((tt, D_MODEL), lambda w, wb, wi, wf: (wi[w], 0)), tok, tok, tok, tok],
        out_specs=(pl.BlockSpec((bm, D_MODEL), lambda w, wb, wi, wf: (wb[w], 0)),
                   pl.BlockSpec((bm, LANES), lambda w, wb, wi, wf: (wb[w], 0))),
        scratch_shapes=[pltpu.VMEM((bm, D_MODEL), F32), pltpu.VMEM((bm, 1), F32)],
    )
    return pl.pallas_call(
        _dispatch_kernel,
        grid_spec=grid_spec,
        out_shape=(jax.ShapeDtypeStruct((n_rows, D_MODEL), BF16),
                   jax.ShapeDtypeStruct((n_rows, LANES), F32)),
        compiler_params=_params("arbitrary"),
    )(wb, wi, wf, u, p1r, p2r, g1r, g2r)


def _expert_kernel(be_ref, bv_ref, bx_ref, x_ref, gs_ref, wg_ref, wu_ref, wd_ref, y_ref, acc_ref):
    b = pl.program_id(0)
    f = pl.program_id(1)
    nf = pl.num_programs(1)
    valid = bv_ref[b] != 0

    @pl.when(valid)
    def _():
        x = x_ref[...]
        g = _dot(x, wg_ref[...])
        hcol = (g * jax.nn.sigmoid(g) * _dot(x, wu_ref[...])).astype(BF16)
        part = _dot(hcol, wd_ref[...])

        @pl.when(f == 0)
        def _():
            acc_ref[...] = part

        @pl.when(f != 0)
        def _():
            acc_ref[...] += part

        @pl.when(f == nf - 1)
        def _():
            y_ref[...] = (acc_ref[...] * gs_ref[:, 0:1]).astype(BF16)

    @pl.when(jnp.logical_and(jnp.logical_not(valid), f == nf - 1))
    def _():
        y_ref[...] = jnp.zeros(y_ref.shape, BF16)


def _experts(xs, gs, wg, wu, wd, be, bv, bx):
    n_rows = xs.shape[0]
    bm, tf = MOE_BM, MOE_TF
    nf = D_FF_EXPERT // tf

    def fidx(b, f, bv):
        return jnp.where(bv[b] != 0, f, nf - 1)

    grid_spec = pltpu.PrefetchScalarGridSpec(
        num_scalar_prefetch=3,
        grid=(n_rows // bm, nf),
        in_specs=[pl.BlockSpec((bm, D_MODEL), lambda b, f, be, bv, bx: (bx[b], 0)),
                  pl.BlockSpec((bm, LANES), lambda b, f, be, bv, bx: (bx[b], 0)),
                  pl.BlockSpec((None, D_MODEL, tf), lambda b, f, be, bv, bx: (be[b], 0, fidx(b, f, bv))),
                  pl.BlockSpec((None, D_MODEL, tf), lambda b, f, be, bv, bx: (be[b], 0, fidx(b, f, bv))),
                  pl.BlockSpec((None, tf, D_MODEL), lambda b, f, be, bv, bx: (be[b], fidx(b, f, bv), 0))],
        out_specs=pl.BlockSpec((bm, D_MODEL), lambda b, f, be, bv, bx: (b, 0)),
        scratch_shapes=[pltpu.VMEM((bm, D_MODEL), F32)],
    )
    return pl.pallas_call(
        _expert_kernel,
        grid_spec=grid_spec,
        out_shape=jax.ShapeDtypeStruct((n_rows, D_MODEL), BF16),
        compiler_params=_params("arbitrary", "arbitrary"),
    )(be, bv, bx, xs, gs, wg, wu, wd)


def _combine_kernel(wb_ref, wi_ref, wf_ref, x_ref, y_ref, p1_ref, p2_ref, gn_ref, out_ref, acc_ref,
                    *, final_norm):
    w = pl.program_id(0)
    flags = wf_ref[w]
    tt, bm = x_ref.shape[0], y_ref.shape[0]

    @pl.when((flags & FLAG_FIRST) != 0)
    def _():
        acc_ref[...] = x_ref[...]

    @pl.when((flags & FLAG_VALID) != 0)
    def _():
        cols = wb_ref[w] * bm + lax.broadcasted_iota(jnp.int32, (tt, bm), 1)
        pick = jnp.where(cols == p1_ref[...], 1.0, jnp.where(cols == p2_ref[...], 1.0, 0.0))
        acc_ref[...] += _dot(pick.astype(BF16), y_ref[...])

    @pl.when((flags & FLAG_LAST) != 0)
    def _():
        if final_norm:
            out_ref[...] = _rms(acc_ref[...], gn_ref[...])
        else:
            out_ref[...] = acc_ref[...]


def _combine(x1, y, p1c, p2c, gn, wb, wi, wf, final_norm):
    s = x1.shape[0]
    tt, bm = MOE_TT, MOE_BM
    tok = lambda w_: pl.BlockSpec((tt, w_), lambda w, wb, wi, wf: (wi[w], 0))
    grid_spec = pltpu.PrefetchScalarGridSpec(
        num_scalar_prefetch=3,
        grid=(wb.shape[0],),
        in_specs=[tok(D_MODEL),
                  pl.BlockSpec((bm, D_MODEL), lambda w, wb, wi, wf: (wb[w], 0)),
                  tok(1), tok(1),
                  pl.BlockSpec(gn.shape, lambda w, wb, wi, wf: (0, 0))],
        out_specs=tok(D_MODEL),
        scratch_shapes=[pltpu.VMEM((tt, D_MODEL), F32)],
    )
    return pl.pallas_call(
        functools.partial(_combine_kernel, final_norm=final_norm),
        grid_spec=grid_spec,
        out_shape=jax.ShapeDtypeStruct((s, D_MODEL), F32),
        compiler_params=_params("arbitrary"),
    )(wb, wi, wf, x1, y, p1c, p2c, gn)


def _moe_plan(route, counts, s):
    tt, bm = MOE_TT, MOE_BM
    nt = s // tt
    nb = (2 * s) // bm + N_EXPERTS
    n_work = nt * N_EXPERTS + nb
    i1 = route[:, 0].astype(jnp.int32)
    i2 = route[:, 1].astype(jnp.int32)
    g1, g2 = route[:, 2], route[:, 3]
    r1 = route[:, 4].astype(jnp.int32)
    r2 = route[:, 5].astype(jnp.int32)
    cnt = counts[0, :N_EXPERTS].astype(jnp.int32)
    padded = ((cnt + bm - 1) // bm) * bm
    ends = jnp.cumsum(padded)
    off = ends - padded
    p1 = off[i1] + r1
    p2 = off[i2] + r2
    nb_used = ends[-1] // bm

    eids = jnp.arange(N_EXPERTS, dtype=jnp.int32)
    sel = ((i1[:, None] == eids) | (i2[:, None] == eids)).astype(jnp.int32)
    c = sel.reshape(nt, tt, N_EXPERTS).sum(axis=1)
    start = off[None, :] + jnp.cumsum(c, axis=0) - c
    b0 = start // bm
    b1 = (start + c - 1) // bm
    tiles = jnp.broadcast_to(jnp.arange(nt, dtype=jnp.int32)[:, None], (nt, N_EXPERTS))
    cand_b = jnp.stack([b0, b1], axis=-1).reshape(-1)
    cand_i = jnp.stack([tiles, tiles], axis=-1).reshape(-1)
    cand_ok = jnp.stack([c > 0, (c > 0) & (b1 != b0)], axis=-1).reshape(-1)
    n_valid = jnp.sum(cand_ok.astype(jnp.int32))

    def work_list(major, minor, minor_range):
        key = jnp.where(cand_ok, major * minor_range + minor, jnp.iinfo(jnp.int32).max)
        order = jnp.argsort(key)[:n_work]
        idx = jnp.arange(n_work, dtype=jnp.int32)
        ok = idx < n_valid
        last_valid = jnp.maximum(n_valid - 1, 0)
        src = order[jnp.where(ok, idx, last_valid)]
        wb, wi = cand_b[src], cand_i[src]
        maj = major[src]
        prev_diff = jnp.concatenate([jnp.ones((1,), bool), maj[1:] != maj[:-1]])
        next_diff = jnp.concatenate([maj[1:] != maj[:-1], jnp.ones((1,), bool)])
        first = ok & prev_diff
        last = ok & (next_diff | (idx == last_valid))
        flags = (ok * FLAG_VALID + first * FLAG_FIRST + last * FLAG_LAST).astype(jnp.int32)
        return wb.astype(jnp.int32), wi.astype(jnp.int32), flags

    disp = work_list(cand_b, cand_i, nt)
    comb = work_list(cand_i, cand_b, nb)

    bidx = jnp.arange(nb, dtype=jnp.int32)
    bvalid = (bidx < nb_used).astype(jnp.int32)
    bsrc = jnp.minimum(bidx, jnp.maximum(nb_used - 1, 0))
    bexp = jnp.minimum(jnp.searchsorted(ends, bsrc * bm, side="right"), N_EXPERTS - 1).astype(jnp.int32)
    rowv = lambda a: a.reshape(nt, 1, tt)
    return dict(p1r=rowv(p1), p2r=rowv(p2), g1r=rowv(g1), g2r=rowv(g2),
                p1c=p1.reshape(s, 1), p2c=p2.reshape(s, 1),
                disp=disp, comb=comb, bexp=bexp, bvalid=bvalid, bsrc=bsrc, n_rows=nb * bm)


def _dil_weights(w_in, g, q_scale):
    sl = lambda off: w_in[:, off + g * DIL_GW: off + (g + 1) * DIL_GW]
    return jnp.concatenate([sl(OFF_QD) * q_scale, sl(OFF_KD), sl(OFF_VD)], axis=-1).astype(BF16)


def kernel(x, mem, norm_mix, w_in, conv_qk, b_gate_if, mlstm_norm, norm_mem, w_mem_kv, w_br_m, w_br_d,
           w_br_x, w_out, norm_ffn, ffn_w_gate, ffn_w_up, ffn_w_down, moe_router, moe_w_gate, moe_w_up,
           moe_w_down, norm_final):
    s = x.shape[1]
    xs = x.reshape(s, D_MODEL)
    mem2 = mem.reshape(N_MEM, D_MODEL)
    row = lambda a: a.reshape(1, -1)
    q_scale = DIL_DH ** -0.5

    for layer in range(DEPTH):
        wl = w_in[layer]
        g_mix = row(norm_mix[layer])
        km, vm = _memkv(mem2, row(norm_mem[layer]), w_mem_kv[layer].astype(BF16))

        w_if = wl[:, OFF_IF:OFF_QD]
        wif = jnp.pad(w_if, ((0, 0), (0, LANES - 2 * MLSTM_HEADS))).astype(BF16)
        wift = w_if.T.astype(BF16)
        bif = jnp.pad(b_gate_if[layer], (0, LANES - 2 * MLSTM_HEADS)).reshape(1, LANES)
        bift = b_gate_if[layer].reshape(2 * MLSTM_HEADS, 1)
        (q_m, k_m, v_m, o_m, ifc, ifr, qd0, kd0, vd0, h_x, gates) = _inproj(
            xs, g_mix, wl[:, :OFF_IF].astype(BF16), wif, wift, bif, bift, conv_qk[layer],
            _dil_weights(wl, 0, q_scale), wl[:, OFF_QX:OFF_GATE].astype(BF16),
            wl[:, OFF_GATE:].astype(BF16), km, vm)

        h_m = _mlstm(q_m, k_m, v_m, o_m, ifc, ifr, row(mlstm_norm[layer]))

        ods, lses = [], []
        for g, (_, dil) in enumerate(DIL_PATTERNS):
            if dil == 1:
                qd, kd, vd = qd0, kd0, vd0
            else:
                qd, kd, vd = _perm_proj(xs, g_mix, _dil_weights(wl, g, q_scale), dil)
            o_g, lse_g = _band_attn(qd, kd, vd, dil)
            ods.append(o_g)
            lses.append(lse_g)

        merge_w = (w_br_m[layer].astype(BF16), w_br_d[layer].astype(BF16), w_br_x[layer].astype(BF16),
                   w_out[layer].astype(BF16), row(norm_ffn[layer]))
        if layer % 2 == 0:
            li = layer // 2
            nch = D_FF // FF_CHUNK
            wg = ffn_w_gate[li].astype(BF16).reshape(D_MODEL, nch, FF_CHUNK).transpose(1, 0, 2)
            wu = ffn_w_up[li].astype(BF16).reshape(D_MODEL, nch, FF_CHUNK).transpose(1, 0, 2)
            wd = ffn_w_down[li].astype(BF16).reshape(nch, FF_CHUNK, D_MODEL)
            xs = _merge(xs, h_m, h_x, gates, ods, lses, *merge_w, dense_w=(wg, wu, wd))
        else:
            li = layer // 2
            wr = jnp.pad(moe_router[li], ((0, 0), (0, LANES - N_EXPERTS)))
            x1, u, route, counts = _merge(xs, h_m, h_x, gates, ods, lses, *merge_w, w_router=wr)
            plan = _moe_plan(route, counts, s)
            rows, gsort = _dispatch(u, plan["p1r"], plan["p2r"], plan["g1r"], plan["g2r"],
                                    *plan["disp"], plan["n_rows"])
            y = _experts(rows, gsort, moe_w_gate[li].astype(BF16), moe_w_up[li].astype(BF16),
                         moe_w_down[li].astype(BF16), plan["bexp"], plan["bvalid"], plan["bsrc"])
            final = layer == DEPTH - 1
            xs = _combine(x1, y, plan["p1c"], plan["p2c"], row(norm_final), *plan["comb"], final_norm=final)
    return xs.reshape(x.shape)
```

```python
import functools

import jax
import jax.numpy as jnp
from jax import lax
from jax.experimental import pallas as pl
from jax.experimental.pallas import tpu as pltpu

F32 = jnp.float32
BF16 = jnp.bfloat16

EPS = 1e-6
D_MODEL = 1024
DEPTH = 4
N_MEM = 256
MLSTM_HEADS = 4
MLSTM_DH = 128
MLSTM_W = MLSTM_HEADS * MLSTM_DH
MLSTM_CHUNK = 128
CONV_W = 4
M_INIT = -1e30
DIL_PATTERNS = ((128, 1), (512, 4), (2048, 16))
DIL_HEADS = 4
DIL_DH = 64
DIL_GW = DIL_HEADS * DIL_DH
DIL_W = 3 * DIL_GW
Q_BLOCK = 128
MEM_HEADS = 4
MEM_DH = 128
MEM_W = MEM_HEADS * MEM_DH
D_FF = 2816
N_EXPERTS = 8
D_FF_EXPERT = 3584

OFF_IF = 4 * MLSTM_W
OFF_QD = OFF_IF + 2 * MLSTM_HEADS
OFF_KD = OFF_QD + DIL_W
OFF_VD = OFF_KD + DIL_W
OFF_QX = OFF_VD + DIL_W
OFF_GATE = OFF_QX + MEM_W
IN_COLS = OFF_GATE + 3 * D_MODEL

LANES = 128
NEG = -1e30
VMEM_LIMIT = 56 * 1024 * 1024

DIL_SLABS = 3 * DIL_GW // LANES
ATT_SUPER = 2048
TOK_TILE = 512
FF_CHUNK = 256
MOE_TT = 256
MOE_BM = 512
MOE_TF = 1792

NT_DIMS = (((1,), (1,)), ((), ()))
TN_DIMS = (((0,), (0,)), ((), ()))


def _params(*sem):
    return pltpu.CompilerParams(dimension_semantics=sem, vmem_limit_bytes=VMEM_LIMIT)


def _dot(a, b):
    return jnp.dot(a, b, preferred_element_type=F32)


def _dot_nt(a, b):
    return lax.dot_general(a, b, NT_DIMS, preferred_element_type=F32)


def _rms(x, g):
    return x * lax.rsqrt(jnp.mean(x * x, axis=-1, keepdims=True) + EPS) * g


def _split3(x):
    hi = x.astype(BF16)
    r1 = x - hi.astype(F32)
    mid = r1.astype(BF16)
    lo = (r1 - mid.astype(F32)).astype(BF16)
    return hi, mid, lo


def _const_spec(shape):
    nd = len(shape)
    return pl.BlockSpec(shape, lambda *_: (0,) * nd, pipeline_mode=pl.Buffered(1))


def _memkv_kernel(mem_ref, g_ref, w_ref, k_ref, v_ref):
    u = _rms(mem_ref[...], g_ref[...]).astype(BF16)
    kv = _dot(u, w_ref[...])
    k_ref[...] = kv[:, :MEM_W].astype(BF16)
    v_ref[...] = kv[:, MEM_W:].astype(BF16)


def _memkv(mem, g, w_kv):
    return pl.pallas_call(
        _memkv_kernel,
        out_shape=(jax.ShapeDtypeStruct((N_MEM, MEM_W), BF16),) * 2,
        compiler_params=pltpu.CompilerParams(vmem_limit_bytes=VMEM_LIMIT),
    )(mem, g, w_kv)


def _inproj_kernel(x_ref, g_ref, wa_ref, wif_ref, wift_ref, bif_ref, bift_ref, cw_ref,
                   wd_ref, wqx_ref, wg_ref, km_ref, vm_ref,
                   q_out, k_out, v_out, o_out, if_out, ift_out, d0_out, d1_out, d2_out,
                   hx_out, gate_out, conv_buf):
    tm = x_ref.shape[0]
    u = _rms(x_ref[...], g_ref[...]).astype(BF16)

    @pl.when(pl.program_id(0) == 0)
    def _():
        conv_buf[0:8, :] = jnp.zeros((8, 2 * MLSTM_W), F32)

    conv_buf[8:tm + 8, :] = _dot(u, wa_ref[:, 0:2 * MLSTM_W])
    acc = cw_ref[0:1, :] * conv_buf[pl.ds(8 - (CONV_W - 1), tm), :]
    for j in range(1, CONV_W):
        acc = acc + cw_ref[j:j + 1, :] * conv_buf[pl.ds(8 - (CONV_W - 1) + j, tm), :]
    conv_buf[0:8, :] = conv_buf[tm:tm + 8, :]
    qk = acc * jax.nn.sigmoid(acc)
    q_out[...] = qk[:, :MLSTM_W].astype(BF16)
    k_out[...] = (qk[:, MLSTM_W:] * (MLSTM_DH ** -0.5)).astype(BF16)

    v_out[...] = _dot(u, wa_ref[:, 2 * MLSTM_W:3 * MLSTM_W]).astype(BF16)
    o_out[...] = jax.nn.sigmoid(_dot(u, wa_ref[:, 3 * MLSTM_W:4 * MLSTM_W])).astype(BF16)

    if_out[...] = _dot(u, wif_ref[...]) + bif_ref[...]
    ift_out[...] = _dot_nt(wift_ref[...], u) + bift_ref[...]

    for gi, d_out in enumerate((d0_out, d1_out, d2_out)):
        d = _dot(u, wd_ref[:, gi * 3 * DIL_GW:(gi + 1) * 3 * DIL_GW])
        for j in range(DIL_SLABS):
            d_out[j] = d[:, j * LANES:(j + 1) * LANES]

    qx = (_dot(u, wqx_ref[...]) * (MEM_DH ** -0.5)).astype(BF16)
    outs = []
    for h in range(MEM_HEADS):
        sl = slice(h * MEM_DH, (h + 1) * MEM_DH)
        s = _dot_nt(qx[:, sl], km_ref[:, sl])
        p = jnp.exp(s - jnp.max(s, axis=-1, keepdims=True))
        den = jnp.sum(p, axis=-1, keepdims=True)
        outs.append(_dot(p.astype(BF16), vm_ref[:, sl]) / den)
    hx_out[...] = jnp.concatenate(outs, axis=-1).astype(BF16)

    gate_out[...] = jax.nn.sigmoid(_dot(u, wg_ref[...])).astype(BF16)


def _inproj(x, g, wa, wif, wift, bif, bift, cw, wd0, wqx, wgate, km, vm):
    s = x.shape[0]
    tm = TOK_TILE
    row = lambda w: pl.BlockSpec((tm, w), lambda i: (i, 0))
    out_shape = (
        jax.ShapeDtypeStruct((s, MLSTM_W), BF16),
        jax.ShapeDtypeStruct((s, MLSTM_W), BF16),
        jax.ShapeDtypeStruct((s, MLSTM_W), BF16),
        jax.ShapeDtypeStruct((s, MLSTM_W), BF16),
        jax.ShapeDtypeStruct((s, LANES), F32),
        jax.ShapeDtypeStruct((8, s), F32),
        jax.ShapeDtypeStruct((DIL_SLABS, s, LANES), F32),
        jax.ShapeDtypeStruct((DIL_SLABS, s, LANES), F32),
        jax.ShapeDtypeStruct((DIL_SLABS, s, LANES), F32),
        jax.ShapeDtypeStruct((s, MEM_W), BF16),
        jax.ShapeDtypeStruct((s, 3 * D_MODEL), BF16),
    )
    slab = pl.BlockSpec((DIL_SLABS, tm, LANES), lambda i: (0, i, 0))
    out_specs = (row(MLSTM_W), row(MLSTM_W), row(MLSTM_W), row(MLSTM_W), row(LANES),
                 pl.BlockSpec((8, tm), lambda i: (0, i)),
                 slab, slab, slab, row(MEM_W), row(3 * D_MODEL))
    in_specs = [row(D_MODEL)] + [_const_spec(a.shape) for a in
                                 (g, wa, wif, wift, bif, bift, cw, wd0, wqx, wgate, km, vm)]
    return pl.pallas_call(
        _inproj_kernel,
        grid=(s // tm,),
        in_specs=in_specs,
        out_specs=out_specs,
        out_shape=out_shape,
        scratch_shapes=[pltpu.VMEM((tm + 8, 2 * MLSTM_W), F32)],
        compiler_params=_params("arbitrary"),
    )(x, g, wa, wif, wift, bif, bift, cw, wd0, wqx, wgate, km, vm)


def _log_sigmoid(x):
    return jnp.minimum(x, 0.0) - jnp.log(1.0 + jnp.exp(-jnp.abs(x)))


def _mlstm_kernel(q_ref, k_ref, v_ref, o_ref, ifc_ref, ifr_ref, g_ref, out_ref, ct_ref, m_ref):
    L = MLSTM_CHUNK
    H = MLSTM_HEADS

    @pl.when(pl.program_id(0) == 0)
    def _():
        ct_ref[...] = jnp.zeros(ct_ref.shape, F32)
        m_ref[...] = jnp.full(m_ref.shape, M_INIT, F32)

    row = lax.broadcasted_iota(jnp.int32, (L, L), 0)
    col = lax.broadcasted_iota(jnp.int32, (L, L), 1)
    causal = col <= row
    tril = jnp.where(causal, 1.0, 0.0).astype(BF16)
    triu = jnp.where(row <= col, 1.0, 0.0).astype(BF16)
    ones_col = jnp.where(col == 0, 1.0, 0.0).astype(BF16)

    ifc = ifc_ref[...]
    ifr = ifr_ref[...]
    cum_c = sum(_dot(tril, p) for p in _split3(_log_sigmoid(ifc)))
    cum_r = sum(_dot(p, triu) for p in _split3(_log_sigmoid(ifr)))

    for h in range(H):
        sl = slice(h * MLSTM_DH, (h + 1) * MLSTM_DH)
        i_c = ifc[:, h:h + 1]
        i_r = ifr[h:h + 1, :]
        cc = cum_c[:, H + h:H + h + 1]
        cr = cum_r[H + h:H + h + 1, :]
        total = cr[:, L - 1:L]
        m_prev = m_ref[h:h + 1, 0:1]

        dm = jnp.where(causal, cc - cr + i_r, -jnp.inf)
        inter = cc + m_prev
        m_row = jnp.maximum(jnp.max(dm, axis=-1, keepdims=True), inter)
        w_intra = jnp.exp(dm - m_row)
        w_inter = jnp.exp(inter - m_row)

        qh = q_ref[:, sl]
        kh = k_ref[:, sl]
        vaug = jnp.concatenate([v_ref[:, sl], ones_col], axis=-1)
        s_mat = _dot_nt(qh, kh) * w_intra
        tot = _dot(s_mat.astype(BF16), vaug) + w_inter * _dot(qh, ct_ref[h].astype(BF16))
        den = tot[:, MLSTM_DH:MLSTM_DH + 1]
        h_out = tot[:, :MLSTM_DH] / jnp.maximum(jnp.abs(den), jnp.exp(-m_row))

        g_end = total - cc + i_c
        m_new = jnp.maximum(total + m_prev, jnp.max(g_end, axis=0, keepdims=True))
        w_end = jnp.exp(g_end - m_new)
        decay = jnp.exp(total + m_prev - m_new)
        vw = (vaug.astype(F32) * w_end).astype(BF16)
        ct_ref[h] = decay * ct_ref[h] + lax.dot_general(kh, vw, TN_DIMS, preferred_element_type=F32)
        m_ref[h:h + 1, :] = jnp.broadcast_to(m_new, (1, LANES))

        mu = jnp.mean(h_out, axis=-1, keepdims=True)
        cen = h_out - mu
        var = jnp.mean(cen * cen, axis=-1, keepdims=True)
        y = cen * lax.rsqrt(var + EPS) * g_ref[:, sl] * o_ref[:, sl].astype(F32)
        out_ref[:, sl] = y.astype(BF16)


def _mlstm(q, k, v, o, ifc, ifr, g):
    s = q.shape[0]
    L = MLSTM_CHUNK
    row = pl.BlockSpec((L, MLSTM_W), lambda c: (c, 0))
    return pl.pallas_call(
        _mlstm_kernel,
        grid=(s // L,),
        in_specs=[row, row, row, row,
                  pl.BlockSpec((L, LANES), lambda c: (c, 0)),
                  pl.BlockSpec((8, L), lambda c: (0, c)),
                  _const_spec(g.shape)],
        out_specs=row,
        out_shape=jax.ShapeDtypeStruct((s, MLSTM_W), BF16),
        scratch_shapes=[pltpu.VMEM((MLSTM_HEADS, MLSTM_DH, 2 * MLSTM_DH), F32),
                        pltpu.VMEM((8, LANES), F32)],
        compiler_params=_params("arbitrary"),
    )(q, k, v, o, ifc, ifr, g)


def _dil_attn_kernel(kv_ref, q_ref, kvp_ref, o_ref, lse_ref, *, dil):
    B = Q_BLOCK
    span = B * dil
    n_sub = ATT_SUPER // span
    row = lax.broadcasted_iota(jnp.int32, (B, 2 * B), 0)
    col = lax.broadcasted_iota(jnp.int32, (B, 2 * B), 1)
    band = jnp.where(col >= row, jnp.where(col <= row + B, 0.0, NEG), NEG)
    first = jnp.where(pl.program_id(0) == 0, 1.0, 0.0)
    band_first = band + first * jnp.where(col < B, NEG, 0.0)
    lane = lax.broadcasted_iota(jnp.int32, (B, LANES), 1)
    lo = lane < DIL_DH
    hi = lane >= DIL_DH

    def rows(start):
        return pl.ds(start, B, stride=dil) if dil > 1 else pl.ds(start, B)

    def unit(cur_start, prev_ref, prev_start, bias):
        for half in range(2):
            q2 = q_ref[half, rows(cur_start), :]
            k2 = jnp.concatenate([prev_ref[half, rows(prev_start), :],
                                  kv_ref[half, rows(cur_start), :]], axis=0).astype(BF16)
            v2 = jnp.concatenate([prev_ref[2 + half, rows(prev_start), :],
                                  kv_ref[2 + half, rows(cur_start), :]], axis=0).astype(BF16)
            res = []
            for keep in (lo, hi):
                qm = jnp.where(keep, q2, 0.0).astype(BF16)
                s = _dot_nt(qm, k2) + bias
                mx = jnp.max(s, axis=-1, keepdims=True)
                p = jnp.exp(s - mx)
                den = jnp.sum(p, axis=-1, keepdims=True)
                res.append((_dot(p.astype(BF16), v2) / den, mx + jnp.log(den)))
            o_ref[half, rows(cur_start), :] = jnp.where(lo, res[0][0], res[1][0])
            lse_ref[half, rows(cur_start), :] = jnp.where(lo, res[0][1], res[1][1])

    def per_residue(r, carry):
        unit(r, kvp_ref, r, band_first)

        def per_sub(j, c):
            unit(j * span + r, kv_ref, (j - 1) * span + r, band)
            return c

        if n_sub > 1:
            lax.fori_loop(1, n_sub, per_sub, 0)
        return carry

    if dil > 1:
        lax.fori_loop(0, dil, per_residue, 0)
    else:
        per_residue(0, 0)


def _dil_attn(qkv, dil):
    s = qkv.shape[1]
    span = Q_BLOCK * dil
    n_prev = ATT_SUPER // span
    blk = lambda n: pl.BlockSpec((n, ATT_SUPER, LANES), lambda i: (0, i, 0))
    return pl.pallas_call(
        functools.partial(_dil_attn_kernel, dil=dil),
        grid=(s // ATT_SUPER,),
        in_specs=[blk(4),
                  pl.BlockSpec((2, ATT_SUPER, LANES), lambda i: (2, i, 0)),
                  pl.BlockSpec((4, span, LANES), lambda i: (0, jnp.maximum(i * n_prev - 1, 0), 0))],
        out_specs=(blk(2), blk(2)),
        out_shape=(jax.ShapeDtypeStruct((2, s, LANES), F32),) * 2,
        compiler_params=_params("arbitrary"),
    )(qkv, qkv, qkv)


def _merge_core(x_ref, hm_ref, hx_ref, gate_ref, od_refs, lse_refs, wm_ref, wdd_ref, wx_ref, wo_ref):
    wide = lambda r: jnp.concatenate([r[0], r[1]], axis=-1)
    lses = [wide(r) for r in lse_refs]
    mx = jnp.maximum(jnp.maximum(lses[0], lses[1]), lses[2])
    es = [jnp.exp(l - mx) for l in lses]
    den = es[0] + es[1] + es[2]
    hd = (es[0] * wide(od_refs[0]) + es[1] * wide(od_refs[1]) + es[2] * wide(od_refs[2])) / den
    d = D_MODEL
    merged = (gate_ref[:, 0:d].astype(F32) * _dot(hm_ref[...], wm_ref[...])
              + gate_ref[:, d:2 * d].astype(F32) * _dot(hd.astype(BF16), wdd_ref[...])
              + gate_ref[:, 2 * d:3 * d].astype(F32) * _dot(hx_ref[...], wx_ref[...]))
    return x_ref[...] + _dot(merged.astype(BF16), wo_ref[...])


def _merge_dense_kernel(x_ref, hm_ref, hx_ref, gate_ref, o0, o1, o2, l0, l1, l2,
                        wm_ref, wdd_ref, wx_ref, wo_ref, gf_ref, wg_ref, wu_ref, wdn_ref,
                        out_ref, acc_ref):
    x1 = _merge_core(x_ref, hm_ref, hx_ref, gate_ref, (o0, o1, o2), (l0, l1, l2),
                     wm_ref, wdd_ref, wx_ref, wo_ref)
    u = _rms(x1, gf_ref[...]).astype(BF16)
    acc_ref[...] = x1

    def body(c, carry):
        cols = pl.ds(pl.multiple_of(c * FF_CHUNK, FF_CHUNK), FF_CHUNK)
        g = _dot(u, wg_ref[:, cols])
        hcol = (g * jax.nn.sigmoid(g) * _dot(u, wu_ref[:, cols])).astype(BF16)
        acc_ref[...] += _dot(hcol, wdn_ref[cols, :])
        return carry

    lax.fori_loop(0, D_FF // FF_CHUNK, body, 0)
    out_ref[...] = acc_ref[...]


def _merge_moe_kernel(x_ref, hm_ref, hx_ref, gate_ref, o0, o1, o2, l0, l1, l2,
                      wm_ref, wdd_ref, wx_ref, wo_ref, gf_ref, wr_ref,
                      x1_out, u_out, route_out, cnt_out, carry_ref):
    tm = x_ref.shape[0]

    @pl.when(pl.program_id(0) == 0)
    def _():
        carry_ref[...] = jnp.zeros(carry_ref.shape, F32)

    x1 = _merge_core(x_ref, hm_ref, hx_ref, gate_ref, (o0, o1, o2), (l0, l1, l2),
                     wm_ref, wdd_ref, wx_ref, wo_ref)
    x1_out[...] = x1
    uf = _rms(x1, gf_ref[...])
    u_out[...] = uf.astype(BF16)

    uh, um, ul = _split3(uf)
    wh, wmid, wl = _split3(wr_ref[...])
    logits = (_dot(uh, wh) + (_dot(uh, wmid) + _dot(um, wh))
              + (_dot(uh, wl) + _dot(um, wmid) + _dot(ul, wh)))
    lane = lax.broadcasted_iota(jnp.int32, (tm, LANES), 1).astype(F32)
    valid = lane < N_EXPERTS
    lg = jnp.where(valid, logits, NEG)
    ex = jnp.exp(lg - jnp.max(lg, axis=-1, keepdims=True))
    probs = jnp.where(valid, ex / jnp.sum(ex, axis=-1, keepdims=True), -1.0)
    p1 = jnp.max(probs, axis=-1, keepdims=True)
    i1 = jnp.min(jnp.where(probs == p1, lane, float(LANES)), axis=-1, keepdims=True)
    rest = jnp.where(lane == i1, -1.0, probs)
    p2 = jnp.max(rest, axis=-1, keepdims=True)
    i2 = jnp.min(jnp.where(rest == p2, lane, float(LANES)), axis=-1, keepdims=True)
    g1 = p1 / (p1 + p2)
    g2 = p2 / (p1 + p2)
    sel = jnp.where(lane == i1, 1.0, jnp.where(lane == i2, 1.0, 0.0))
    row = lax.broadcasted_iota(jnp.int32, (tm, tm), 0)
    col = lax.broadcasted_iota(jnp.int32, (tm, tm), 1)
    before = jnp.where(col < row, 1.0, 0.0).astype(BF16)
    ranks = _dot(before, sel.astype(BF16)) + carry_ref[0:1, :]
    r1 = jnp.sum(jnp.where(lane == i1, ranks, 0.0), axis=-1, keepdims=True)
    r2 = jnp.sum(jnp.where(lane == i2, ranks, 0.0), axis=-1, keepdims=True)
    carry_ref[...] = carry_ref[...] + jnp.sum(sel, axis=0, keepdims=True)
    cnt_out[...] = carry_ref[...]
    route = jnp.where(lane == 0, i1, jnp.where(lane == 1, i2, jnp.where(lane == 2, g1,
            jnp.where(lane == 3, g2, jnp.where(lane == 4, r1, jnp.where(lane == 5, r2, 0.0))))))
    route_out[...] = route


def _merge(x, hm, hx, gates, ods, lses, wm, wdd, wx, wo, gf, dense_w=None, w_router=None):
    s = x.shape[0]
    tm = TOK_TILE
    row = lambda w: pl.BlockSpec((tm, w), lambda i: (i, 0))
    acts = (x, hm, hx, gates) + tuple(ods) + tuple(lses)
    slab = pl.BlockSpec((2, tm, LANES), lambda i: (0, i, 0))
    act_specs = [row(D_MODEL), row(MLSTM_W), row(MEM_W), row(3 * D_MODEL)] + [slab] * 6
    if dense_w is not None:
        consts = (wm, wdd, wx, wo, gf) + tuple(dense_w)
        return pl.pallas_call(
            _merge_dense_kernel,
            grid=(s // tm,),
            in_specs=act_specs + [_const_spec(c.shape) for c in consts],
            out_specs=row(D_MODEL),
            out_shape=jax.ShapeDtypeStruct((s, D_MODEL), F32),
            scratch_shapes=[pltpu.VMEM((tm, D_MODEL), F32)],
            compiler_params=_params("arbitrary"),
        )(*acts, *consts)
    consts = (wm, wdd, wx, wo, gf, w_router)
    return pl.pallas_call(
        _merge_moe_kernel,
        grid=(s // tm,),
        in_specs=act_specs + [_const_spec(c.shape) for c in consts],
        out_specs=(row(D_MODEL), row(D_MODEL), row(LANES), pl.BlockSpec((8, LANES), lambda i: (0, 0))),
        out_shape=(jax.ShapeDtypeStruct((s, D_MODEL), F32),
                   jax.ShapeDtypeStruct((s, D_MODEL), BF16),
                   jax.ShapeDtypeStruct((s, LANES), F32),
                   jax.ShapeDtypeStruct((8, LANES), F32)),
        scratch_shapes=[pltpu.VMEM((8, LANES), F32)],
        compiler_params=_params("arbitrary"),
    )(*acts, *consts)


FLAG_VALID, FLAG_FIRST, FLAG_LAST = 1, 2, 4


def _dispatch_kernel(wb_ref, wi_ref, wf_ref, u_ref, p1_ref, p2_ref, g1_ref, g2_ref,
                     xs_ref, gs_ref, acc_ref, gacc_ref):
    w = pl.program_id(0)
    flags = wf_ref[w]
    bm, tt = xs_ref.shape[0], u_ref.shape[0]

    @pl.when((flags & FLAG_FIRST) != 0)
    def _():
        acc_ref[...] = jnp.zeros(acc_ref.shape, F32)
        gacc_ref[...] = jnp.zeros(gacc_ref.shape, F32)

    @pl.when((flags & FLAG_VALID) != 0)
    def _():
        rows = wb_ref[w] * bm + lax.broadcasted_iota(jnp.int32, (bm, tt), 0)
        e1 = rows == p1_ref[...]
        e2 = rows == p2_ref[...]
        pick = jnp.where(e1, 1.0, jnp.where(e2, 1.0, 0.0)).astype(BF16)
        acc_ref[...] += _dot(pick, u_ref[...])
        gw = jnp.where(e1, g1_ref[...], jnp.where(e2, g2_ref[...], 0.0))
        gacc_ref[...] += jnp.sum(gw, axis=-1, keepdims=True)

    @pl.when((flags & FLAG_LAST) != 0)
    def _():
        xs_ref[...] = acc_ref[...].astype(BF16)
        gs_ref[...] = jnp.broadcast_to(gacc_ref[...], gs_ref.shape)


def _dispatch(u, p1r, p2r, g1r, g2r, wb, wi, wf, n_rows):
    tt, bm = MOE_TT, MOE_BM
    tok = pl.BlockSpec((None, 1, tt), lambda w, wb, wi, wf: (wi[w], 0, 0))
    grid_spec = pltpu.PrefetchScalarGridSpec(
        num_scalar_prefetch=3,
        grid=(wb.shape[0],),
        in_specs=[pl.BlockSpec((tt, D_MODEL), lambda w, wb, wi, wf: (wi[w], 0)), tok, tok, tok, tok],
        out_specs=(pl.BlockSpec((bm, D_MODEL), lambda w, wb, wi, wf: (wb[w], 0)),
                   pl.BlockSpec((bm, LANES), lambda w, wb, wi, wf: (wb[w], 0))),
        scratch_shapes=[pltpu.VMEM((bm, D_MODEL), F32), pltpu.VMEM((bm, 1), F32)],
    )
    return pl.pallas_call(
        _dispatch_kernel,
        grid_spec=grid_spec,
        out_shape=(jax.ShapeDtypeStruct((n_rows, D_MODEL), BF16),
                   jax.ShapeDtypeStruct((n_rows, LANES), F32)),
        compiler_params=_params("arbitrary"),
    )(wb, wi, wf, u, p1r, p2r, g1r, g2r)


def _expert_kernel(be_ref, bv_ref, bx_ref, x_ref, gs_ref, wg_ref, wu_ref, wd_ref, y_ref, acc_ref):
    b = pl.program_id(0)
    f = pl.program_id(1)
    nf = pl.num_programs(1)
    valid = bv_ref[b] != 0

    @pl.when(valid)
    def _():
        x = x_ref[...]
        g = _dot(x, wg_ref[...])
        hcol = (g * jax.nn.sigmoid(g) * _dot(x, wu_ref[...])).astype(BF16)
        part = _dot(hcol, wd_ref[...])

        @pl.when(f == 0)
        def _():
            acc_ref[...] = part

        @pl.when(f != 0)
        def _():
            acc_ref[...] += part

        @pl.when(f == nf - 1)
        def _():
            y_ref[...] = (acc_ref[...] * gs_ref[:, 0:1]).astype(BF16)

    @pl.when(jnp.logical_and(jnp.logical_not(valid), f == nf - 1))
    def _():
        y_ref[...] = jnp.zeros(y_ref.shape, BF16)


def _experts(xs, gs, wg, wu, wd, be, bv, bx):
    n_rows = xs.shape[0]
    bm, tf = MOE_BM, MOE_TF
    nf = D_FF_EXPERT // tf

    def fidx(b, f, bv):
        return jnp.where(bv[b] != 0, f, nf - 1)

    grid_spec = pltpu.PrefetchScalarGridSpec(
        num_scalar_prefetch=3,
        grid=(n_rows // bm, nf),
        in_specs=[pl.BlockSpec((bm, D_MODEL), lambda b, f, be, bv, bx: (bx[b], 0)),
                  pl.BlockSpec((bm, LANES), lambda b, f, be, bv, bx: (bx[b], 0)),
                  pl.BlockSpec((None, D_MODEL, tf), lambda b, f, be, bv, bx: (be[b], 0, fidx(b, f, bv))),
                  pl.BlockSpec((None, D_MODEL, tf), lambda b, f, be, bv, bx: (be[b], 0, fidx(b, f, bv))),
                  pl.BlockSpec((None, tf, D_MODEL), lambda b, f, be, bv, bx: (be[b], fidx(b, f, bv), 0))],
        out_specs=pl.BlockSpec((bm, D_MODEL), lambda b, f, be, bv, bx: (b, 0)),
        scratch_shapes=[pltpu.VMEM((bm, D_MODEL), F32)],
    )
    return pl.pallas_call(
        _expert_kernel,
        grid_spec=grid_spec,
        out_shape=jax.ShapeDtypeStruct((n_rows, D_MODEL), BF16),
        compiler_params=_params("arbitrary", "arbitrary"),
    )(be, bv, bx, xs, gs, wg, wu, wd)


def _combine_kernel(wb_ref, wi_ref, wf_ref, x_ref, y_ref, p1_ref, p2_ref, gn_ref, out_ref, acc_ref,
                    *, final_norm):
    w = pl.program_id(0)
    flags = wf_ref[w]
    tt, bm = x_ref.shape[0], y_ref.shape[0]

    @pl.when((flags & FLAG_FIRST) != 0)
    def _():
        acc_ref[...] = x_ref[...]

    @pl.when((flags & FLAG_VALID) != 0)
    def _():
        cols = wb_ref[w] * bm + lax.broadcasted_iota(jnp.int32, (tt, bm), 1)
        pick = jnp.where(cols == p1_ref[...], 1.0, jnp.where(cols == p2_ref[...], 1.0, 0.0))
        acc_ref[...] += _dot(pick.astype(BF16), y_ref[...])

    @pl.when((flags & FLAG_LAST) != 0)
    def _():
        if final_norm:
            out_ref[...] = _rms(acc_ref[...], gn_ref[...])
        else:
            out_ref[...] = acc_ref[...]


def _combine(x1, y, p1c, p2c, gn, wb, wi, wf, final_norm):
    s = x1.shape[0]
    tt, bm = MOE_TT, MOE_BM
    tok = lambda w_: pl.BlockSpec((tt, w_), lambda w, wb, wi, wf: (wi[w], 0))
    grid_spec = pltpu.PrefetchScalarGridSpec(
        num_scalar_prefetch=3,
        grid=(wb.shape[0],),
        in_specs=[tok(D_MODEL),
                  pl.BlockSpec((bm, D_MODEL), lambda w, wb, wi, wf: (wb[w], 0)),
                  tok(1), tok(1),
                  pl.BlockSpec(gn.shape, lambda w, wb, wi, wf: (0, 0))],
        out_specs=tok(D_MODEL),
        scratch_shapes=[pltpu.VMEM((tt, D_MODEL), F32)],
    )
    return pl.pallas_call(
        functools.partial(_combine_kernel, final_norm=final_norm),
        grid_spec=grid_spec,
        out_shape=jax.ShapeDtypeStruct((s, D_MODEL), F32),
        compiler_params=_params("arbitrary"),
    )(wb, wi, wf, x1, y, p1c, p2c, gn)


def _moe_plan(route, counts, s):
    tt, bm = MOE_TT, MOE_BM
    nt = s // tt
    nb = (2 * s) // bm + N_EXPERTS
    n_work = nt * N_EXPERTS + nb
    i1 = route[:, 0].astype(jnp.int32)
    i2 = route[:, 1].astype(jnp.int32)
    g1, g2 = route[:, 2], route[:, 3]
    r1 = route[:, 4].astype(jnp.int32)
    r2 = route[:, 5].astype(jnp.int32)
    cnt = counts[0, :N_EXPERTS].astype(jnp.int32)
    padded = ((cnt + bm - 1) // bm) * bm
    ends = jnp.cumsum(padded)
    off = ends - padded
    p1 = off[i1] + r1
    p2 = off[i2] + r2
    nb_used = ends[-1] // bm

    eids = jnp.arange(N_EXPERTS, dtype=jnp.int32)
    sel = ((i1[:, None] == eids) | (i2[:, None] == eids)).astype(jnp.int32)
    c = sel.reshape(nt, tt, N_EXPERTS).sum(axis=1)
    start = off[None, :] + jnp.cumsum(c, axis=0) - c
    b0 = start // bm
    b1 = (start + c - 1) // bm
    tiles = jnp.broadcast_to(jnp.arange(nt, dtype=jnp.int32)[:, None], (nt, N_EXPERTS))
    cand_b = jnp.stack([b0, b1], axis=-1).reshape(-1)
    cand_i = jnp.stack([tiles, tiles], axis=-1).reshape(-1)
    cand_ok = jnp.stack([c > 0, (c > 0) & (b1 != b0)], axis=-1).reshape(-1)
    n_valid = jnp.sum(cand_ok.astype(jnp.int32))

    def work_list(major, minor, minor_range):
        key = jnp.where(cand_ok, major * minor_range + minor, jnp.iinfo(jnp.int32).max)
        order = jnp.argsort(key)[:n_work]
        idx = jnp.arange(n_work, dtype=jnp.int32)
        ok = idx < n_valid
        last_valid = jnp.maximum(n_valid - 1, 0)
        src = order[jnp.where(ok, idx, last_valid)]
        wb, wi = cand_b[src], cand_i[src]
        maj = major[src]
        prev_diff = jnp.concatenate([jnp.ones((1,), bool), maj[1:] != maj[:-1]])
        next_diff = jnp.concatenate([maj[1:] != maj[:-1], jnp.ones((1,), bool)])
        first = ok & prev_diff
        last = ok & (next_diff | (idx == last_valid))
        flags = (ok * FLAG_VALID + first * FLAG_FIRST + last * FLAG_LAST).astype(jnp.int32)
        return wb.astype(jnp.int32), wi.astype(jnp.int32), flags

    disp = work_list(cand_b, cand_i, nt)
    comb = work_list(cand_i, cand_b, nb)

    bidx = jnp.arange(nb, dtype=jnp.int32)
    bvalid = (bidx < nb_used).astype(jnp.int32)
    bsrc = jnp.minimum(bidx, jnp.maximum(nb_used - 1, 0))
    bexp = jnp.minimum(jnp.searchsorted(ends, bsrc * bm, side="right"), N_EXPERTS - 1).astype(jnp.int32)
    rowv = lambda a: a.reshape(nt, 1, tt)
    return dict(p1r=rowv(p1), p2r=rowv(p2), g1r=rowv(g1), g2r=rowv(g2),
                p1c=p1.reshape(s, 1), p2c=p2.reshape(s, 1),
                disp=disp, comb=comb, bexp=bexp, bvalid=bvalid, bsrc=bsrc, n_rows=nb * bm)


def _dil_weights(w_in, q_scale):
    cols = []
    for g in range(len(DIL_PATTERNS)):
        sl = lambda off: w_in[:, off + g * DIL_GW: off + (g + 1) * DIL_GW]
        cols += [sl(OFF_KD), sl(OFF_VD), sl(OFF_QD) * q_scale]
    return jnp.concatenate(cols, axis=-1).astype(BF16)


def kernel(x, mem, norm_mix, w_in, conv_qk, b_gate_if, mlstm_norm, norm_mem, w_mem_kv, w_br_m, w_br_d,
           w_br_x, w_out, norm_ffn, ffn_w_gate, ffn_w_up, ffn_w_down, moe_router, moe_w_gate, moe_w_up,
           moe_w_down, norm_final):
    s = x.shape[1]
    xs = x.reshape(s, D_MODEL)
    mem2 = mem.reshape(N_MEM, D_MODEL)
    row = lambda a: a.reshape(1, -1)
    q_scale = DIL_DH ** -0.5

    for layer in range(DEPTH):
        wl = w_in[layer]
        g_mix = row(norm_mix[layer])
        km, vm = _memkv(mem2, row(norm_mem[layer]), w_mem_kv[layer].astype(BF16))

        w_if = wl[:, OFF_IF:OFF_QD]
        wif = jnp.pad(w_if, ((0, 0), (0, LANES - 2 * MLSTM_HEADS))).astype(BF16)
        wift = w_if.T.astype(BF16)
        bif = jnp.pad(b_gate_if[layer], (0, LANES - 2 * MLSTM_HEADS)).reshape(1, LANES)
        bift = b_gate_if[layer].reshape(2 * MLSTM_HEADS, 1)
        (q_m, k_m, v_m, o_m, ifc, ifr, d0, d1, d2, h_x, gates) = _inproj(
            xs, g_mix, wl[:, :OFF_IF].astype(BF16), wif, wift, bif, bift, conv_qk[layer],
            _dil_weights(wl, q_scale), wl[:, OFF_QX:OFF_GATE].astype(BF16),
            wl[:, OFF_GATE:].astype(BF16), km, vm)

        h_m = _mlstm(q_m, k_m, v_m, o_m, ifc, ifr, row(mlstm_norm[layer]))

        ods, lses = [], []
        for qkv, (_, dil) in zip((d0, d1, d2), DIL_PATTERNS):
            o_g, lse_g = _dil_attn(qkv, dil)
            ods.append(o_g)
            lses.append(lse_g)

        merge_w = (w_br_m[layer].astype(BF16), w_br_d[layer].astype(BF16), w_br_x[layer].astype(BF16),
                   w_out[layer].astype(BF16), row(norm_ffn[layer]))
        if layer % 2 == 0:
            li = layer // 2
            dense_w = (ffn_w_gate[li].astype(BF16), ffn_w_up[li].astype(BF16), ffn_w_down[li].astype(BF16))
            xs = _merge(xs, h_m, h_x, gates, ods, lses, *merge_w, dense_w=dense_w)
        else:
            li = layer // 2
            wr = jnp.pad(moe_router[li], ((0, 0), (0, LANES - N_EXPERTS)))
            x1, u, route, counts = _merge(xs, h_m, h_x, gates, ods, lses, *merge_w, w_router=wr)
            plan = _moe_plan(route, counts, s)
            rows, gsort = _dispatch(u, plan["p1r"], plan["p2r"], plan["g1r"], plan["g2r"],
                                    *plan["disp"], plan["n_rows"])
            y = _experts(rows, gsort, moe_w_gate[li].astype(BF16), moe_w_up[li].astype(BF16),
                         moe_w_down[li].astype(BF16), plan["bexp"], plan["bvalid"], plan["bsrc"])
            final = layer == DEPTH - 1
            xs = _combine(x1, y, plan["p1c"], plan["p2c"], row(norm_final), *plan["comb"], final_norm=final)
    return xs.reshape(x.shape)
```

```python
import functools

import jax
import jax.numpy as jnp
from jax import lax
from jax.experimental import pallas as pl
from jax.experimental.pallas import tpu as pltpu

F32 = jnp.float32
BF16 = jnp.bfloat16

EPS = 1e-6
D_MODEL = 1024
DEPTH = 4
N_MEM = 256
MLSTM_HEADS = 4
MLSTM_DH = 128
MLSTM_W = MLSTM_HEADS * MLSTM_DH
MLSTM_CHUNK = 128
CONV_W = 4
M_INIT = -1e30
DIL_PATTERNS = ((128, 1), (512, 4), (2048, 16))
DIL_HEADS = 4
DIL_DH = 64
DIL_GW = DIL_HEADS * DIL_DH
DIL_W = 3 * DIL_GW
Q_BLOCK = 128
MEM_HEADS = 4
MEM_DH = 128
MEM_W = MEM_HEADS * MEM_DH
D_FF = 2816
N_EXPERTS = 8
D_FF_EXPERT = 3584

OFF_IF = 4 * MLSTM_W
OFF_QD = OFF_IF + 2 * MLSTM_HEADS
OFF_KD = OFF_QD + DIL_W
OFF_VD = OFF_KD + DIL_W
OFF_QX = OFF_VD + DIL_W
OFF_GATE = OFF_QX + MEM_W
IN_COLS = OFF_GATE + 3 * D_MODEL

LANES = 128
NEG = -1e30
VMEM_LIMIT = 56 * 1024 * 1024

DIL_SLABS = 3 * DIL_GW // LANES
ATT_SUPER = 2048
TOK_TILE = 512
FF_CHUNK = 256
MOE_BM = 512
MOE_TF = 1792

NT_DIMS = (((1,), (1,)), ((), ()))
TN_DIMS = (((0,), (0,)), ((), ()))


def _params(*sem):
    return pltpu.CompilerParams(dimension_semantics=sem, vmem_limit_bytes=VMEM_LIMIT)


def _dot(a, b):
    return jnp.dot(a, b, preferred_element_type=F32)


def _dot_nt(a, b):
    return lax.dot_general(a, b, NT_DIMS, preferred_element_type=F32)


def _rms(x, g):
    return x * lax.rsqrt(jnp.mean(x * x, axis=-1, keepdims=True) + EPS) * g


def _split3(x):
    hi = x.astype(BF16)
    r1 = x - hi.astype(F32)
    mid = r1.astype(BF16)
    lo = (r1 - mid.astype(F32)).astype(BF16)
    return hi, mid, lo


def _const_spec(shape):
    nd = len(shape)
    return pl.BlockSpec(shape, lambda *_: (0,) * nd, pipeline_mode=pl.Buffered(1))


def _memkv_kernel(mem_ref, g_ref, w_ref, k_ref, v_ref):
    u = _rms(mem_ref[...], g_ref[...]).astype(BF16)
    kv = _dot(u, w_ref[...])
    k_ref[...] = kv[:, :MEM_W].astype(BF16)
    v_ref[...] = kv[:, MEM_W:].astype(BF16)


def _memkv(mem, g, w_kv):
    return pl.pallas_call(
        _memkv_kernel,
        out_shape=(jax.ShapeDtypeStruct((N_MEM, MEM_W), BF16),) * 2,
        compiler_params=pltpu.CompilerParams(vmem_limit_bytes=VMEM_LIMIT),
    )(mem, g, w_kv)


def _inproj_kernel(x_ref, g_ref, wa_ref, wif_ref, wift_ref, bif_ref, bift_ref, cw_ref,
                   wd_ref, wqx_ref, wg_ref, km_ref, vm_ref,
                   q_out, k_out, v_out, o_out, if_out, ift_out, d0_out, d1_out, d2_out,
                   hx_out, gate_out, conv_buf):
    tm = x_ref.shape[0]
    u = _rms(x_ref[...], g_ref[...]).astype(BF16)

    @pl.when(pl.program_id(0) == 0)
    def _():
        conv_buf[0:8, :] = jnp.zeros((8, 2 * MLSTM_W), F32)

    conv_buf[8:tm + 8, :] = _dot(u, wa_ref[:, 0:2 * MLSTM_W])
    acc = cw_ref[0:1, :] * conv_buf[pl.ds(8 - (CONV_W - 1), tm), :]
    for j in range(1, CONV_W):
        acc = acc + cw_ref[j:j + 1, :] * conv_buf[pl.ds(8 - (CONV_W - 1) + j, tm), :]
    conv_buf[0:8, :] = conv_buf[tm:tm + 8, :]
    qk = acc * jax.nn.sigmoid(acc)
    q_out[...] = qk[:, :MLSTM_W].astype(BF16)
    k_out[...] = (qk[:, MLSTM_W:] * (MLSTM_DH ** -0.5)).astype(BF16)

    v_out[...] = _dot(u, wa_ref[:, 2 * MLSTM_W:3 * MLSTM_W]).astype(BF16)
    o_out[...] = jax.nn.sigmoid(_dot(u, wa_ref[:, 3 * MLSTM_W:4 * MLSTM_W])).astype(BF16)

    if_out[...] = _dot(u, wif_ref[...]) + bif_ref[...]
    ift_out[...] = _dot_nt(wift_ref[...], u) + bift_ref[...]

    for gi, d_out in enumerate((d0_out, d1_out, d2_out)):
        d = _dot(u, wd_ref[:, gi * 3 * DIL_GW:(gi + 1) * 3 * DIL_GW])
        for j in range(DIL_SLABS):
            d_out[j] = d[:, j * LANES:(j + 1) * LANES]

    qx = (_dot(u, wqx_ref[...]) * (MEM_DH ** -0.5)).astype(BF16)
    outs = []
    for h in range(MEM_HEADS):
        sl = slice(h * MEM_DH, (h + 1) * MEM_DH)
        s = _dot_nt(qx[:, sl], km_ref[:, sl])
        p = jnp.exp(s - jnp.max(s, axis=-1, keepdims=True))
        den = jnp.sum(p, axis=-1, keepdims=True)
        outs.append(_dot(p.astype(BF16), vm_ref[:, sl]) / den)
    hx_out[...] = jnp.concatenate(outs, axis=-1).astype(BF16)

    gate_out[...] = jax.nn.sigmoid(_dot(u, wg_ref[...])).astype(BF16)


def _inproj(x, g, wa, wif, wift, bif, bift, cw, wd0, wqx, wgate, km, vm):
    s = x.shape[0]
    tm = TOK_TILE
    row = lambda w: pl.BlockSpec((tm, w), lambda i: (i, 0))
    out_shape = (
        jax.ShapeDtypeStruct((s, MLSTM_W), BF16),
        jax.ShapeDtypeStruct((s, MLSTM_W), BF16),
        jax.ShapeDtypeStruct((s, MLSTM_W), BF16),
        jax.ShapeDtypeStruct((s, MLSTM_W), BF16),
        jax.ShapeDtypeStruct((s, LANES), F32),
        jax.ShapeDtypeStruct((8, s), F32),
        jax.ShapeDtypeStruct((DIL_SLABS, s, LANES), F32),
        jax.ShapeDtypeStruct((DIL_SLABS, s, LANES), F32),
        jax.ShapeDtypeStruct((DIL_SLABS, s, LANES), F32),
        jax.ShapeDtypeStruct((s, MEM_W), BF16),
        jax.ShapeDtypeStruct((s, 3 * D_MODEL), BF16),
    )
    slab = pl.BlockSpec((DIL_SLABS, tm, LANES), lambda i: (0, i, 0))
    out_specs = (row(MLSTM_W), row(MLSTM_W), row(MLSTM_W), row(MLSTM_W), row(LANES),
                 pl.BlockSpec((8, tm), lambda i: (0, i)),
                 slab, slab, slab, row(MEM_W), row(3 * D_MODEL))
    in_specs = [row(D_MODEL)] + [_const_spec(a.shape) for a in
                                 (g, wa, wif, wift, bif, bift, cw, wd0, wqx, wgate, km, vm)]
    return pl.pallas_call(
        _inproj_kernel,
        grid=(s // tm,),
        in_specs=in_specs,
        out_specs=out_specs,
        out_shape=out_shape,
        scratch_shapes=[pltpu.VMEM((tm + 8, 2 * MLSTM_W), F32)],
        compiler_params=_params("arbitrary"),
    )(x, g, wa, wif, wift, bif, bift, cw, wd0, wqx, wgate, km, vm)


def _log_sigmoid(x):
    return jnp.minimum(x, 0.0) - jnp.log(1.0 + jnp.exp(-jnp.abs(x)))


def _mlstm_kernel(q_ref, k_ref, v_ref, o_ref, ifc_ref, ifr_ref, g_ref, out_ref, ct_ref, m_ref):
    L = MLSTM_CHUNK
    H = MLSTM_HEADS

    @pl.when(pl.program_id(0) == 0)
    def _():
        ct_ref[...] = jnp.zeros(ct_ref.shape, F32)
        m_ref[...] = jnp.full(m_ref.shape, M_INIT, F32)

    row = lax.broadcasted_iota(jnp.int32, (L, L), 0)
    col = lax.broadcasted_iota(jnp.int32, (L, L), 1)
    causal = col <= row
    tril = jnp.where(causal, 1.0, 0.0).astype(BF16)
    triu = jnp.where(row <= col, 1.0, 0.0).astype(BF16)
    ones_col = jnp.where(col == 0, 1.0, 0.0).astype(BF16)

    ifc = ifc_ref[...]
    ifr = ifr_ref[...]
    cum_c = sum(_dot(tril, p) for p in _split3(_log_sigmoid(ifc)))
    cum_r = sum(_dot(p, triu) for p in _split3(_log_sigmoid(ifr)))

    for h in range(H):
        sl = slice(h * MLSTM_DH, (h + 1) * MLSTM_DH)
        i_c = ifc[:, h:h + 1]
        i_r = ifr[h:h + 1, :]
        cc = cum_c[:, H + h:H + h + 1]
        cr = cum_r[H + h:H + h + 1, :]
        total = cr[:, L - 1:L]
        m_prev = m_ref[h:h + 1, 0:1]

        dm = jnp.where(causal, cc - cr + i_r, -jnp.inf)
        inter = cc + m_prev
        m_row = jnp.maximum(jnp.max(dm, axis=-1, keepdims=True), inter)
        w_intra = jnp.exp(dm - m_row)
        w_inter = jnp.exp(inter - m_row)

        qh = q_ref[:, sl]
        kh = k_ref[:, sl]
        vaug = jnp.concatenate([v_ref[:, sl], ones_col], axis=-1)
        s_mat = _dot_nt(qh, kh) * w_intra
        tot = _dot(s_mat.astype(BF16), vaug) + w_inter * _dot(qh, ct_ref[h].astype(BF16))
        den = tot[:, MLSTM_DH:MLSTM_DH + 1]
        h_out = tot[:, :MLSTM_DH] / jnp.maximum(jnp.abs(den), jnp.exp(-m_row))

        g_end = total - cc + i_c
        m_new = jnp.maximum(total + m_prev, jnp.max(g_end, axis=0, keepdims=True))
        w_end = jnp.exp(g_end - m_new)
        decay = jnp.exp(total + m_prev - m_new)
        vw = (vaug.astype(F32) * w_end).astype(BF16)
        ct_ref[h] = decay * ct_ref[h] + lax.dot_general(kh, vw, TN_DIMS, preferred_element_type=F32)
        m_ref[h:h + 1, :] = jnp.broadcast_to(m_new, (1, LANES))

        mu = jnp.mean(h_out, axis=-1, keepdims=True)
        cen = h_out - mu
        var = jnp.mean(cen * cen, axis=-1, keepdims=True)
        y = cen * lax.rsqrt(var + EPS) * g_ref[:, sl] * o_ref[:, sl].astype(F32)
        out_ref[:, sl] = y.astype(BF16)


def _mlstm(q, k, v, o, ifc, ifr, g):
    s = q.shape[0]
    L = MLSTM_CHUNK
    row = pl.BlockSpec((L, MLSTM_W), lambda c: (c, 0))
    return pl.pallas_call(
        _mlstm_kernel,
        grid=(s // L,),
        in_specs=[row, row, row, row,
                  pl.BlockSpec((L, LANES), lambda c: (c, 0)),
                  pl.BlockSpec((8, L), lambda c: (0, c)),
                  _const_spec(g.shape)],
        out_specs=row,
        out_shape=jax.ShapeDtypeStruct((s, MLSTM_W), BF16),
        scratch_shapes=[pltpu.VMEM((MLSTM_HEADS, MLSTM_DH, 2 * MLSTM_DH), F32),
                        pltpu.VMEM((8, LANES), F32)],
        compiler_params=_params("arbitrary"),
    )(q, k, v, o, ifc, ifr, g)


def _dil_attn_kernel(kv_ref, q_ref, kvp_ref, o_ref, lse_ref, *, dil):
    B = Q_BLOCK
    span = B * dil
    n_sub = ATT_SUPER // span
    row = lax.broadcasted_iota(jnp.int32, (B, 2 * B), 0)
    col = lax.broadcasted_iota(jnp.int32, (B, 2 * B), 1)
    band = jnp.where(col >= row, jnp.where(col <= row + B, 0.0, NEG), NEG)
    first = jnp.where(pl.program_id(0) == 0, 1.0, 0.0)
    band_first = band + first * jnp.where(col < B, NEG, 0.0)
    lane = lax.broadcasted_iota(jnp.int32, (B, LANES), 1)
    lo = lane < DIL_DH
    hi = lane >= DIL_DH

    def rows(start):
        return pl.ds(start, B, stride=dil) if dil > 1 else pl.ds(start, B)

    def unit(cur_start, prev_ref, prev_start, bias):
        for half in range(2):
            q2 = q_ref[half, rows(cur_start), :]
            k2 = jnp.concatenate([prev_ref[half, rows(prev_start), :],
                                  kv_ref[half, rows(cur_start), :]], axis=0).astype(BF16)
            v2 = jnp.concatenate([prev_ref[2 + half, rows(prev_start), :],
                                  kv_ref[2 + half, rows(cur_start), :]], axis=0).astype(BF16)
            res = []
            for keep in (lo, hi):
                qm = jnp.where(keep, q2, 0.0).astype(BF16)
                s = _dot_nt(qm, k2) + bias
                mx = jnp.max(s, axis=-1, keepdims=True)
                p = jnp.exp(s - mx)
                den = jnp.sum(p, axis=-1, keepdims=True)
                res.append((_dot(p.astype(BF16), v2) / den, mx + jnp.log(den)))
            o_ref[half, rows(cur_start), :] = jnp.where(lo, res[0][0], res[1][0])
            lse_ref[half, rows(cur_start), :] = jnp.where(lo, res[0][1], res[1][1])

    def per_residue(r, carry):
        unit(r, kvp_ref, r, band_first)

        def per_sub(j, c):
            unit(j * span + r, kv_ref, (j - 1) * span + r, band)
            return c

        if n_sub > 1:
            lax.fori_loop(1, n_sub, per_sub, 0)
        return carry

    if dil > 1:
        lax.fori_loop(0, dil, per_residue, 0)
    else:
        per_residue(0, 0)


def _dil_attn(qkv, dil):
    s = qkv.shape[1]
    span = Q_BLOCK * dil
    n_prev = ATT_SUPER // span
    blk = lambda n: pl.BlockSpec((n, ATT_SUPER, LANES), lambda i: (0, i, 0))
    return pl.pallas_call(
        functools.partial(_dil_attn_kernel, dil=dil),
        grid=(s // ATT_SUPER,),
        in_specs=[blk(4),
                  pl.BlockSpec((2, ATT_SUPER, LANES), lambda i: (2, i, 0)),
                  pl.BlockSpec((4, span, LANES), lambda i: (0, jnp.maximum(i * n_prev - 1, 0), 0))],
        out_specs=(blk(2), blk(2)),
        out_shape=(jax.ShapeDtypeStruct((2, s, LANES), F32),) * 2,
        compiler_params=_params("arbitrary"),
    )(qkv, qkv, qkv)


def _merge_core(x_ref, hm_ref, hx_ref, gate_ref, od_refs, lse_refs, wm_ref, wdd_ref, wx_ref, wo_ref):
    wide = lambda r: jnp.concatenate([r[0], r[1]], axis=-1)
    lses = [wide(r) for r in lse_refs]
    mx = jnp.maximum(jnp.maximum(lses[0], lses[1]), lses[2])
    es = [jnp.exp(l - mx) for l in lses]
    den = es[0] + es[1] + es[2]
    hd = (es[0] * wide(od_refs[0]) + es[1] * wide(od_refs[1]) + es[2] * wide(od_refs[2])) / den
    d = D_MODEL
    merged = (gate_ref[:, 0:d].astype(F32) * _dot(hm_ref[...], wm_ref[...])
              + gate_ref[:, d:2 * d].astype(F32) * _dot(hd.astype(BF16), wdd_ref[...])
              + gate_ref[:, 2 * d:3 * d].astype(F32) * _dot(hx_ref[...], wx_ref[...]))
    return x_ref[...] + _dot(merged.astype(BF16), wo_ref[...])


def _merge_dense_kernel(x_ref, hm_ref, hx_ref, gate_ref, o0, o1, o2, l0, l1, l2,
                        wm_ref, wdd_ref, wx_ref, wo_ref, gf_ref, wg_ref, wu_ref, wdn_ref,
                        out_ref, acc_ref):
    x1 = _merge_core(x_ref, hm_ref, hx_ref, gate_ref, (o0, o1, o2), (l0, l1, l2),
                     wm_ref, wdd_ref, wx_ref, wo_ref)
    u = _rms(x1, gf_ref[...]).astype(BF16)
    acc_ref[...] = x1

    def body(c, carry):
        cols = pl.ds(pl.multiple_of(c * FF_CHUNK, FF_CHUNK), FF_CHUNK)
        g = _dot(u, wg_ref[:, cols])
        hcol = (g * jax.nn.sigmoid(g) * _dot(u, wu_ref[:, cols])).astype(BF16)
        acc_ref[...] += _dot(hcol, wdn_ref[cols, :])
        return carry

    lax.fori_loop(0, D_FF // FF_CHUNK, body, 0)
    out_ref[...] = acc_ref[...]


def _merge_moe_kernel(x_ref, hm_ref, hx_ref, gate_ref, o0, o1, o2, l0, l1, l2,
                      wm_ref, wdd_ref, wx_ref, wo_ref, gf_ref, wr_ref,
                      x1_out, u_out, route_out, cnt_out, carry_ref):
    tm = x_ref.shape[0]

    @pl.when(pl.program_id(0) == 0)
    def _():
        carry_ref[...] = jnp.zeros(carry_ref.shape, F32)

    x1 = _merge_core(x_ref, hm_ref, hx_ref, gate_ref, (o0, o1, o2), (l0, l1, l2),
                     wm_ref, wdd_ref, wx_ref, wo_ref)
    x1_out[...] = x1
    uf = _rms(x1, gf_ref[...])
    u_out[...] = uf

    uh, um, ul = _split3(uf)
    wh, wmid, wl = _split3(wr_ref[...])
    logits = (_dot(uh, wh) + (_dot(uh, wmid) + _dot(um, wh))
              + (_dot(uh, wl) + _dot(um, wmid) + _dot(ul, wh)))
    lane = lax.broadcasted_iota(jnp.int32, (tm, LANES), 1).astype(F32)
    valid = lane < N_EXPERTS
    lg = jnp.where(valid, logits, NEG)
    ex = jnp.exp(lg - jnp.max(lg, axis=-1, keepdims=True))
    probs = jnp.where(valid, ex / jnp.sum(ex, axis=-1, keepdims=True), -1.0)
    p1 = jnp.max(probs, axis=-1, keepdims=True)
    i1 = jnp.min(jnp.where(probs == p1, lane, float(LANES)), axis=-1, keepdims=True)
    rest = jnp.where(lane == i1, -1.0, probs)
    p2 = jnp.max(rest, axis=-1, keepdims=True)
    i2 = jnp.min(jnp.where(rest == p2, lane, float(LANES)), axis=-1, keepdims=True)
    g1 = p1 / (p1 + p2)
    g2 = p2 / (p1 + p2)
    sel = jnp.where(lane == i1, 1.0, jnp.where(lane == i2, 1.0, 0.0))
    row = lax.broadcasted_iota(jnp.int32, (tm, tm), 0)
    col = lax.broadcasted_iota(jnp.int32, (tm, tm), 1)
    before = jnp.where(col < row, 1.0, 0.0).astype(BF16)
    ranks = _dot(before, sel.astype(BF16)) + carry_ref[0:1, :]
    r1 = jnp.sum(jnp.where(lane == i1, ranks, 0.0), axis=-1, keepdims=True)
    r2 = jnp.sum(jnp.where(lane == i2, ranks, 0.0), axis=-1, keepdims=True)
    carry_ref[...] = carry_ref[...] + jnp.sum(sel, axis=0, keepdims=True)
    cnt_out[...] = carry_ref[...]
    route = jnp.where(lane == 0, i1, jnp.where(lane == 1, i2, jnp.where(lane == 2, g1,
            jnp.where(lane == 3, g2, jnp.where(lane == 4, r1, jnp.where(lane == 5, r2, 0.0))))))
    route_out[...] = route


def _merge(x, hm, hx, gates, ods, lses, wm, wdd, wx, wo, gf, dense_w=None, w_router=None):
    s = x.shape[0]
    tm = TOK_TILE
    row = lambda w: pl.BlockSpec((tm, w), lambda i: (i, 0))
    acts = (x, hm, hx, gates) + tuple(ods) + tuple(lses)
    slab = pl.BlockSpec((2, tm, LANES), lambda i: (0, i, 0))
    act_specs = [row(D_MODEL), row(MLSTM_W), row(MEM_W), row(3 * D_MODEL)] + [slab] * 6
    if dense_w is not None:
        consts = (wm, wdd, wx, wo, gf) + tuple(dense_w)
        return pl.pallas_call(
            _merge_dense_kernel,
            grid=(s // tm,),
            in_specs=act_specs + [_const_spec(c.shape) for c in consts],
            out_specs=row(D_MODEL),
            out_shape=jax.ShapeDtypeStruct((s, D_MODEL), F32),
            scratch_shapes=[pltpu.VMEM((tm, D_MODEL), F32)],
            compiler_params=_params("arbitrary"),
        )(*acts, *consts)
    consts = (wm, wdd, wx, wo, gf, w_router)
    return pl.pallas_call(
        _merge_moe_kernel,
        grid=(s // tm,),
        in_specs=act_specs + [_const_spec(c.shape) for c in consts],
        out_specs=(row(D_MODEL), row(D_MODEL), row(LANES), pl.BlockSpec((8, LANES), lambda i: (0, 0))),
        out_shape=(jax.ShapeDtypeStruct((s, D_MODEL), F32),
                   jax.ShapeDtypeStruct((s, D_MODEL), F32),
                   jax.ShapeDtypeStruct((s, LANES), F32),
                   jax.ShapeDtypeStruct((8, LANES), F32)),
        scratch_shapes=[pltpu.VMEM((8, LANES), F32)],
        compiler_params=_params("arbitrary"),
    )(*acts, *consts)


def _row_copies(src, src_rows, dst, dst_rows, sem, n):
    def copy(r):
        return pltpu.make_async_copy(src.at[pl.ds(src_rows(r), 1), :], dst.at[pl.ds(dst_rows(r), 1), :], sem)

    def start():
        def body(r, c):
            copy(r).start()
            return c
        lax.fori_loop(0, n, body, 0, unroll=8)

    def wait():
        def body(r, c):
            copy(r).wait()
            return c
        lax.fori_loop(0, n, body, 0, unroll=8)

    return start, wait


def _expert_kernel(be_ref, bv_ref, src_ref, srcn_ref, dst_ref, u_hbm, wg_ref, wu_ref, wd_ref, y_hbm,
                   xg_ref, xb_ref, acc_ref, ys_ref, gsem, ssem, pend_ref):
    b = pl.program_id(0)
    f = pl.program_id(1)
    nb = pl.num_programs(0)
    nf = pl.num_programs(1)
    bm = xb_ref.shape[0]
    valid = bv_ref[b] != 0
    slot = lax.rem(b, 2)
    same = lambda r: r

    def gather(idx_ref, s_):
        return _row_copies(u_hbm, lambda r: idx_ref[0, r], xg_ref.at[s_], same, gsem.at[s_], bm)

    scatter_start, scatter_wait = _row_copies(ys_ref, same, y_hbm, lambda r: dst_ref[0, r], ssem, bm)

    @pl.when(jnp.logical_and(b == 0, f == 0))
    def _():
        pend_ref[0] = 0
        gather(src_ref, 0)[0]()
        ys_ref[...] = jnp.zeros(ys_ref.shape, F32)
        spare = pltpu.make_async_copy(ys_ref, y_hbm.at[pl.ds(y_hbm.shape[0] - bm, bm), :], ssem)
        spare.start()
        spare.wait()

    @pl.when(jnp.logical_and(valid, f == 0))
    def _():
        gather(src_ref, slot)[1]()
        xb_ref[...] = xg_ref[slot].astype(BF16)

        @pl.when(jnp.logical_and(b + 1 < nb, bv_ref[jnp.minimum(b + 1, nb - 1)] != 0))
        def _():
            gather(srcn_ref, 1 - slot)[0]()

    @pl.when(valid)
    def _():
        x = xb_ref[...]
        g = _dot(x, wg_ref[...])
        hcol = (g * jax.nn.sigmoid(g) * _dot(x, wu_ref[...])).astype(BF16)
        part = _dot(hcol, wd_ref[...])

        @pl.when(f == 0)
        def _():
            acc_ref[...] = part

        @pl.when(jnp.logical_and(f != 0, f != nf - 1))
        def _():
            acc_ref[...] += part

        @pl.when(f == nf - 1)
        def _():
            @pl.when(pend_ref[0] != 0)
            def _():
                scatter_wait()

            ys_ref[...] = acc_ref[...] + part
            scatter_start()
            pend_ref[0] = 1

            @pl.when(b == nb - 1)
            def _():
                scatter_wait()
                pend_ref[0] = 0

    @pl.when(jnp.logical_and(jnp.logical_not(valid), jnp.logical_and(f == nf - 1, pend_ref[0] != 0)))
    def _():
        scatter_wait()
        pend_ref[0] = 0


def _experts(u, src, dst, wg, wu, wd, be, bv, n_out_rows):
    nb, _, bm = src.shape
    tf = MOE_TF
    nf = D_FF_EXPERT // tf
    assert nf >= 2

    def fidx(b, f, bv):
        return jnp.where(bv[b] != 0, f, nf - 1)

    smem_rows = lambda idx: pl.BlockSpec((None, 1, bm), idx, memory_space=pltpu.SMEM)
    grid_spec = pltpu.PrefetchScalarGridSpec(
        num_scalar_prefetch=2,
        grid=(nb, nf),
        in_specs=[smem_rows(lambda b, f, be, bv: (b, 0, 0)),
                  smem_rows(lambda b, f, be, bv: (jnp.minimum(b + 1, nb - 1), 0, 0)),
                  smem_rows(lambda b, f, be, bv: (b, 0, 0)),
                  pl.BlockSpec(memory_space=pl.ANY),
                  pl.BlockSpec((None, D_MODEL, tf), lambda b, f, be, bv: (be[b], 0, fidx(b, f, bv))),
                  pl.BlockSpec((None, D_MODEL, tf), lambda b, f, be, bv: (be[b], 0, fidx(b, f, bv))),
                  pl.BlockSpec((None, tf, D_MODEL), lambda b, f, be, bv: (be[b], fidx(b, f, bv), 0))],
        out_specs=pl.BlockSpec(memory_space=pl.ANY),
        scratch_shapes=[pltpu.VMEM((2, bm, D_MODEL), F32),
                        pltpu.VMEM((bm, D_MODEL), BF16),
                        pltpu.VMEM((bm, D_MODEL), F32),
                        pltpu.VMEM((bm, D_MODEL), F32),
                        pltpu.SemaphoreType.DMA((2,)),
                        pltpu.SemaphoreType.DMA(()),
                        pltpu.SMEM((1,), jnp.int32)],
    )
    return pl.pallas_call(
        _expert_kernel,
        grid_spec=grid_spec,
        out_shape=jax.ShapeDtypeStruct((n_out_rows, D_MODEL), F32),
        compiler_params=_params("arbitrary", "arbitrary"),
    )(be, bv, src, src, dst, u, wg, wu, wd)


def _combine_kernel(x_ref, y0_ref, y1_ref, route_ref, gn_ref, out_ref, *, final_norm):
    out = x_ref[...] + route_ref[:, 2:3] * y0_ref[...] + route_ref[:, 3:4] * y1_ref[...]
    out_ref[...] = _rms(out, gn_ref[...]) if final_norm else out


def _combine(x1, y, route, gn, final_norm):
    s = x1.shape[0]
    tm = TOK_TILE
    nt = s // tm
    row = lambda w: pl.BlockSpec((tm, w), lambda i: (i, 0))
    return pl.pallas_call(
        functools.partial(_combine_kernel, final_norm=final_norm),
        grid=(nt,),
        in_specs=[row(D_MODEL), row(D_MODEL), pl.BlockSpec((tm, D_MODEL), lambda i: (i + nt, 0)),
                  row(LANES), _const_spec(gn.shape)],
        out_specs=row(D_MODEL),
        out_shape=jax.ShapeDtypeStruct((s, D_MODEL), F32),
        compiler_params=_params("arbitrary"),
    )(x1, y, y, route, gn)


def _moe_plan(route, counts, s):
    bm = MOE_BM
    nb = (2 * s) // bm + N_EXPERTS
    i1 = route[:, 0].astype(jnp.int32)
    i2 = route[:, 1].astype(jnp.int32)
    r1 = route[:, 4].astype(jnp.int32)
    r2 = route[:, 5].astype(jnp.int32)
    cnt = counts[0, :N_EXPERTS].astype(jnp.int32)
    padded = ((cnt + bm - 1) // bm) * bm
    ends = jnp.cumsum(padded)
    off = ends - padded
    p1 = off[i1] + r1
    p2 = off[i2] + r2
    nb_used = ends[-1] // bm

    tok = jnp.arange(s, dtype=jnp.int32)
    rows = jnp.arange(nb * bm, dtype=jnp.int32)
    put = lambda a, idx, v: a.at[idx].set(v, unique_indices=True)
    src = put(put(jnp.zeros((nb * bm,), jnp.int32), p1, tok), p2, tok)
    dst = put(put(2 * s + rows % bm, p1, tok), p2, s + tok)

    bidx = jnp.arange(nb, dtype=jnp.int32)
    bvalid = (bidx < nb_used).astype(jnp.int32)
    blast = jnp.minimum(bidx, jnp.maximum(nb_used - 1, 0))
    bexp = jnp.minimum(jnp.searchsorted(ends, blast * bm, side="right"), N_EXPERTS - 1).astype(jnp.int32)
    return src.reshape(nb, 1, bm), dst.reshape(nb, 1, bm), bexp, bvalid


def _dil_weights(w_in, q_scale):
    cols = []
    for g in range(len(DIL_PATTERNS)):
        sl = lambda off: w_in[:, off + g * DIL_GW: off + (g + 1) * DIL_GW]
        cols += [sl(OFF_KD), sl(OFF_VD), sl(OFF_QD) * q_scale]
    return jnp.concatenate(cols, axis=-1).astype(BF16)


def kernel(x, mem, norm_mix, w_in, conv_qk, b_gate_if, mlstm_norm, norm_mem, w_mem_kv, w_br_m, w_br_d,
           w_br_x, w_out, norm_ffn, ffn_w_gate, ffn_w_up, ffn_w_down, moe_router, moe_w_gate, moe_w_up,
           moe_w_down, norm_final):
    s = x.shape[1]
    xs = x.reshape(s, D_MODEL)
    mem2 = mem.reshape(N_MEM, D_MODEL)
    row = lambda a: a.reshape(1, -1)
    q_scale = DIL_DH ** -0.5

    for layer in range(DEPTH):
        wl = w_in[layer]
        g_mix = row(norm_mix[layer])
        km, vm = _memkv(mem2, row(norm_mem[layer]), w_mem_kv[layer].astype(BF16))

        w_if = wl[:, OFF_IF:OFF_QD]
        wif = jnp.pad(w_if, ((0, 0), (0, LANES - 2 * MLSTM_HEADS))).astype(BF16)
        wift = w_if.T.astype(BF16)
        bif = jnp.pad(b_gate_if[layer], (0, LANES - 2 * MLSTM_HEADS)).reshape(1, LANES)
        bift = b_gate_if[layer].reshape(2 * MLSTM_HEADS, 1)
        (q_m, k_m, v_m, o_m, ifc, ifr, d0, d1, d2, h_x, gates) = _inproj(
            xs, g_mix, wl[:, :OFF_IF].astype(BF16), wif, wift, bif, bift, conv_qk[layer],
            _dil_weights(wl, q_scale), wl[:, OFF_QX:OFF_GATE].astype(BF16),
            wl[:, OFF_GATE:].astype(BF16), km, vm)

        h_m = _mlstm(q_m, k_m, v_m, o_m, ifc, ifr, row(mlstm_norm[layer]))

        ods, lses = [], []
        for qkv, (_, dil) in zip((d0, d1, d2), DIL_PATTERNS):
            o_g, lse_g = _dil_attn(qkv, dil)
            ods.append(o_g)
            lses.append(lse_g)

        merge_w = (w_br_m[layer].astype(BF16), w_br_d[layer].astype(BF16), w_br_x[layer].astype(BF16),
                   w_out[layer].astype(BF16), row(norm_ffn[layer]))
        if layer % 2 == 0:
            li = layer // 2
            dense_w = (ffn_w_gate[li].astype(BF16), ffn_w_up[li].astype(BF16), ffn_w_down[li].astype(BF16))
            xs = _merge(xs, h_m, h_x, gates, ods, lses, *merge_w, dense_w=dense_w)
        else:
            li = layer // 2
            wr = jnp.pad(moe_router[li], ((0, 0), (0, LANES - N_EXPERTS)))
            x1, u, route, counts = _merge(xs, h_m, h_x, gates, ods, lses, *merge_w, w_router=wr)
            src, dst, bexp, bvalid = _moe_plan(route, counts, s)
            y = _experts(u, src, dst, moe_w_gate[li].astype(BF16), moe_w_up[li].astype(BF16),
                         moe_w_down[li].astype(BF16), bexp, bvalid, 2 * s + MOE_BM)
            xs = _combine(x1, y, route, row(norm_final), final_norm=layer == DEPTH - 1)
    return xs.reshape(x.shape)
```

```python
import functools

import jax
import jax.numpy as jnp
from jax import lax
from jax.experimental import pallas as pl
from jax.experimental.pallas import tpu as pltpu

F32 = jnp.float32
BF16 = jnp.bfloat16

EPS = 1e-6
D_MODEL = 1024
DEPTH = 4
N_MEM = 256
MLSTM_HEADS = 4
MLSTM_DH = 128
MLSTM_W = MLSTM_HEADS * MLSTM_DH
MLSTM_CHUNK = 128
CONV_W = 4
M_INIT = -1e30
DIL_PATTERNS = ((128, 1), (512, 4), (2048, 16))
DIL_HEADS = 4
DIL_DH = 64
DIL_GW = DIL_HEADS * DIL_DH
DIL_W = 3 * DIL_GW
Q_BLOCK = 128
MEM_HEADS = 4
MEM_DH = 128
MEM_W = MEM_HEADS * MEM_DH
D_FF = 2816
N_EXPERTS = 8
D_FF_EXPERT = 3584

OFF_IF = 4 * MLSTM_W
OFF_QD = OFF_IF + 2 * MLSTM_HEADS
OFF_KD = OFF_QD + DIL_W
OFF_VD = OFF_KD + DIL_W
OFF_QX = OFF_VD + DIL_W
OFF_GATE = OFF_QX + MEM_W
IN_COLS = OFF_GATE + 3 * D_MODEL

LANES = 128
SUBLANES = 8
NEG = -1e30
VMEM_LIMIT = 56 * 1024 * 1024

DIL_SLABS = 3 * DIL_GW // LANES
ATT_SUPER = 2048
TOK_TILE = 512
FF_CHUNK = 256
MOE_BM = 512
MOE_TD = 2048
MOE_TC = 256
MOE_TF = 1792

NT_DIMS = (((1,), (1,)), ((), ()))
TN_DIMS = (((0,), (0,)), ((), ()))


def _params(*sem):
    return pltpu.CompilerParams(dimension_semantics=sem, vmem_limit_bytes=VMEM_LIMIT)


def _dot(a, b):
    return jnp.dot(a, b, preferred_element_type=F32)


def _dot_nt(a, b):
    return lax.dot_general(a, b, NT_DIMS, preferred_element_type=F32)


def _rms(x, g):
    return x * lax.rsqrt(jnp.mean(x * x, axis=-1, keepdims=True) + EPS) * g


def _split3(x):
    hi = x.astype(BF16)
    r1 = x - hi.astype(F32)
    mid = r1.astype(BF16)
    lo = (r1 - mid.astype(F32)).astype(BF16)
    return hi, mid, lo


def _const_spec(shape):
    nd = len(shape)
    return pl.BlockSpec(shape, lambda *_: (0,) * nd, pipeline_mode=pl.Buffered(1))


def _memkv_kernel(mem_ref, g_ref, w_ref, k_ref, v_ref):
    u = _rms(mem_ref[...], g_ref[...]).astype(BF16)
    kv = _dot(u, w_ref[...])
    k_ref[...] = kv[:, :MEM_W].astype(BF16)
    v_ref[...] = kv[:, MEM_W:].astype(BF16)


def _memkv(mem, g, w_kv):
    return pl.pallas_call(
        _memkv_kernel,
        out_shape=(jax.ShapeDtypeStruct((N_MEM, MEM_W), BF16),) * 2,
        compiler_params=pltpu.CompilerParams(vmem_limit_bytes=VMEM_LIMIT),
    )(mem, g, w_kv)


def _inproj_kernel(x_ref, g_ref, wa_ref, wif_ref, wift_ref, bif_ref, bift_ref, cw_ref,
                   wd_ref, wqx_ref, wg_ref, km_ref, vm_ref,
                   q_out, k_out, v_out, o_out, if_out, ift_out, d0_out, d1_out, d2_out,
                   hx_out, gate_out, conv_buf):
    tm = x_ref.shape[0]
    u = _rms(x_ref[...], g_ref[...]).astype(BF16)

    @pl.when(pl.program_id(0) == 0)
    def _():
        conv_buf[0:8, :] = jnp.zeros((8, 2 * MLSTM_W), F32)

    conv_buf[8:tm + 8, :] = _dot(u, wa_ref[:, 0:2 * MLSTM_W])
    acc = cw_ref[0:1, :] * conv_buf[pl.ds(8 - (CONV_W - 1), tm), :]
    for j in range(1, CONV_W):
        acc = acc + cw_ref[j:j + 1, :] * conv_buf[pl.ds(8 - (CONV_W - 1) + j, tm), :]
    conv_buf[0:8, :] = conv_buf[tm:tm + 8, :]
    qk = acc * jax.nn.sigmoid(acc)
    q_out[...] = qk[:, :MLSTM_W].astype(BF16)
    k_out[...] = (qk[:, MLSTM_W:] * (MLSTM_DH ** -0.5)).astype(BF16)

    v_out[...] = _dot(u, wa_ref[:, 2 * MLSTM_W:3 * MLSTM_W]).astype(BF16)
    o_out[...] = jax.nn.sigmoid(_dot(u, wa_ref[:, 3 * MLSTM_W:4 * MLSTM_W])).astype(BF16)

    if_out[...] = _dot(u, wif_ref[...]) + bif_ref[...]
    ift_out[...] = _dot_nt(wift_ref[...], u) + bift_ref[...]

    for gi, d_out in enumerate((d0_out, d1_out, d2_out)):
        d = _dot(u, wd_ref[:, gi * 3 * DIL_GW:(gi + 1) * 3 * DIL_GW])
        for j in range(DIL_SLABS):
            d_out[j] = d[:, j * LANES:(j + 1) * LANES]

    qx = (_dot(u, wqx_ref[...]) * (MEM_DH ** -0.5)).astype(BF16)
    outs = []
    for h in range(MEM_HEADS):
        sl = slice(h * MEM_DH, (h + 1) * MEM_DH)
        s = _dot_nt(qx[:, sl], km_ref[:, sl])
        p = jnp.exp(s - jnp.max(s, axis=-1, keepdims=True))
        den = jnp.sum(p, axis=-1, keepdims=True)
        outs.append(_dot(p.astype(BF16), vm_ref[:, sl]) / den)
    hx_out[...] = jnp.concatenate(outs, axis=-1).astype(BF16)

    gate_out[...] = jax.nn.sigmoid(_dot(u, wg_ref[...])).astype(BF16)


def _inproj(x, g, wa, wif, wift, bif, bift, cw, wd0, wqx, wgate, km, vm):
    s = x.shape[0]
    tm = TOK_TILE
    row = lambda w: pl.BlockSpec((tm, w), lambda i: (i, 0))
    out_shape = (
        jax.ShapeDtypeStruct((s, MLSTM_W), BF16),
        jax.ShapeDtypeStruct((s, MLSTM_W), BF16),
        jax.ShapeDtypeStruct((s, MLSTM_W), BF16),
        jax.ShapeDtypeStruct((s, MLSTM_W), BF16),
        jax.ShapeDtypeStruct((s, LANES), F32),
        jax.ShapeDtypeStruct((8, s), F32),
        jax.ShapeDtypeStruct((DIL_SLABS, s, LANES), F32),
        jax.ShapeDtypeStruct((DIL_SLABS, s, LANES), F32),
        jax.ShapeDtypeStruct((DIL_SLABS, s, LANES), F32),
        jax.ShapeDtypeStruct((s, MEM_W), BF16),
        jax.ShapeDtypeStruct((s, 3 * D_MODEL), BF16),
    )
    slab = pl.BlockSpec((DIL_SLABS, tm, LANES), lambda i: (0, i, 0))
    out_specs = (row(MLSTM_W), row(MLSTM_W), row(MLSTM_W), row(MLSTM_W), row(LANES),
                 pl.BlockSpec((8, tm), lambda i: (0, i)),
                 slab, slab, slab, row(MEM_W), row(3 * D_MODEL))
    in_specs = [row(D_MODEL)] + [_const_spec(a.shape) for a in
                                 (g, wa, wif, wift, bif, bift, cw, wd0, wqx, wgate, km, vm)]
    return pl.pallas_call(
        _inproj_kernel,
        grid=(s // tm,),
        in_specs=in_specs,
        out_specs=out_specs,
        out_shape=out_shape,
        scratch_shapes=[pltpu.VMEM((tm + 8, 2 * MLSTM_W), F32)],
        compiler_params=_params("arbitrary"),
    )(x, g, wa, wif, wift, bif, bift, cw, wd0, wqx, wgate, km, vm)


def _log_sigmoid(x):
    return jnp.minimum(x, 0.0) - jnp.log(1.0 + jnp.exp(-jnp.abs(x)))


def _mlstm_kernel(q_ref, k_ref, v_ref, o_ref, ifc_ref, ifr_ref, g_ref, out_ref, ct_ref, m_ref):
    L = MLSTM_CHUNK
    H = MLSTM_HEADS

    @pl.when(pl.program_id(0) == 0)
    def _():
        ct_ref[...] = jnp.zeros(ct_ref.shape, F32)
        m_ref[...] = jnp.full(m_ref.shape, M_INIT, F32)

    row = lax.broadcasted_iota(jnp.int32, (L, L), 0)
    col = lax.broadcasted_iota(jnp.int32, (L, L), 1)
    causal = col <= row
    tril = jnp.where(causal, 1.0, 0.0).astype(BF16)
    triu = jnp.where(row <= col, 1.0, 0.0).astype(BF16)
    ones_col = jnp.where(col == 0, 1.0, 0.0).astype(BF16)

    ifc = ifc_ref[...]
    ifr = ifr_ref[...]
    cum_c = sum(_dot(tril, p) for p in _split3(_log_sigmoid(ifc)))
    cum_r = sum(_dot(p, triu) for p in _split3(_log_sigmoid(ifr)))

    for h in range(H):
        sl = slice(h * MLSTM_DH, (h + 1) * MLSTM_DH)
        i_c = ifc[:, h:h + 1]
        i_r = ifr[h:h + 1, :]
        cc = cum_c[:, H + h:H + h + 1]
        cr = cum_r[H + h:H + h + 1, :]
        total = cr[:, L - 1:L]
        m_prev = m_ref[h:h + 1, 0:1]

        dm = jnp.where(causal, cc - cr + i_r, -jnp.inf)
        inter = cc + m_prev
        m_row = jnp.maximum(jnp.max(dm, axis=-1, keepdims=True), inter)
        w_intra = jnp.exp(dm - m_row)
        w_inter = jnp.exp(inter - m_row)

        qh = q_ref[:, sl]
        kh = k_ref[:, sl]
        vaug = jnp.concatenate([v_ref[:, sl], ones_col], axis=-1)
        s_mat = _dot_nt(qh, kh) * w_intra
        tot = _dot(s_mat.astype(BF16), vaug) + w_inter * _dot(qh, ct_ref[h].astype(BF16))
        den = tot[:, MLSTM_DH:MLSTM_DH + 1]
        h_out = tot[:, :MLSTM_DH] / jnp.maximum(jnp.abs(den), jnp.exp(-m_row))

        g_end = total - cc + i_c
        m_new = jnp.maximum(total + m_prev, jnp.max(g_end, axis=0, keepdims=True))
        w_end = jnp.exp(g_end - m_new)
        decay = jnp.exp(total + m_prev - m_new)
        vw = (vaug.astype(F32) * w_end).astype(BF16)
        ct_ref[h] = decay * ct_ref[h] + lax.dot_general(kh, vw, TN_DIMS, preferred_element_type=F32)
        m_ref[h:h + 1, :] = jnp.broadcast_to(m_new, (1, LANES))

        mu = jnp.mean(h_out, axis=-1, keepdims=True)
        cen = h_out - mu
        var = jnp.mean(cen * cen, axis=-1, keepdims=True)
        y = cen * lax.rsqrt(var + EPS) * g_ref[:, sl] * o_ref[:, sl].astype(F32)
        out_ref[:, sl] = y.astype(BF16)


def _mlstm(q, k, v, o, ifc, ifr, g):
    s = q.shape[0]
    L = MLSTM_CHUNK
    row = pl.BlockSpec((L, MLSTM_W), lambda c: (c, 0))
    return pl.pallas_call(
        _mlstm_kernel,
        grid=(s // L,),
        in_specs=[row, row, row, row,
                  pl.BlockSpec((L, LANES), lambda c: (c, 0)),
                  pl.BlockSpec((8, L), lambda c: (0, c)),
                  _const_spec(g.shape)],
        out_specs=row,
        out_shape=jax.ShapeDtypeStruct((s, MLSTM_W), BF16),
        scratch_shapes=[pltpu.VMEM((MLSTM_HEADS, MLSTM_DH, 2 * MLSTM_DH), F32),
                        pltpu.VMEM((8, LANES), F32)],
        compiler_params=_params("arbitrary"),
    )(q, k, v, o, ifc, ifr, g)


def _dil_attn_kernel(kv_ref, q_ref, kvp_ref, o_ref, lse_ref, *, dil):
    B = Q_BLOCK
    span = B * dil
    n_sub = ATT_SUPER // span
    row = lax.broadcasted_iota(jnp.int32, (B, 2 * B), 0)
    col = lax.broadcasted_iota(jnp.int32, (B, 2 * B), 1)
    band = jnp.where(col >= row, jnp.where(col <= row + B, 0.0, NEG), NEG)
    first = jnp.where(pl.program_id(0) == 0, 1.0, 0.0)
    band_first = band + first * jnp.where(col < B, NEG, 0.0)
    lane = lax.broadcasted_iota(jnp.int32, (B, LANES), 1)
    lo = lane < DIL_DH
    hi = lane >= DIL_DH

    def rows(start):
        return pl.ds(start, B, stride=dil) if dil > 1 else pl.ds(start, B)

    def unit(cur_start, prev_ref, prev_start, bias):
        for half in range(2):
            q2 = q_ref[half, rows(cur_start), :]
            k2 = jnp.concatenate([prev_ref[half, rows(prev_start), :],
                                  kv_ref[half, rows(cur_start), :]], axis=0).astype(BF16)
            v2 = jnp.concatenate([prev_ref[2 + half, rows(prev_start), :],
                                  kv_ref[2 + half, rows(cur_start), :]], axis=0).astype(BF16)
            res = []
            for keep in (lo, hi):
                qm = jnp.where(keep, q2, 0.0).astype(BF16)
                s = _dot_nt(qm, k2) + bias
                mx = jnp.max(s, axis=-1, keepdims=True)
                p = jnp.exp(s - mx)
                den = jnp.sum(p, axis=-1, keepdims=True)
                res.append((_dot(p.astype(BF16), v2) / den, mx + jnp.log(den)))
            o_ref[half, rows(cur_start), :] = jnp.where(lo, res[0][0], res[1][0])
            lse_ref[half, rows(cur_start), :] = jnp.where(lo, res[0][1], res[1][1])

    def per_residue(r, carry):
        unit(r, kvp_ref, r, band_first)

        def per_sub(j, c):
            unit(j * span + r, kv_ref, (j - 1) * span + r, band)
            return c

        if n_sub > 1:
            lax.fori_loop(1, n_sub, per_sub, 0)
        return carry

    if dil > 1:
        lax.fori_loop(0, dil, per_residue, 0)
    else:
        per_residue(0, 0)


def _dil_attn(qkv, dil):
    s = qkv.shape[1]
    span = Q_BLOCK * dil
    n_prev = ATT_SUPER // span
    blk = lambda n: pl.BlockSpec((n, ATT_SUPER, LANES), lambda i: (0, i, 0))
    return pl.pallas_call(
        functools.partial(_dil_attn_kernel, dil=dil),
        grid=(s // ATT_SUPER,),
        in_specs=[blk(4),
                  pl.BlockSpec((2, ATT_SUPER, LANES), lambda i: (2, i, 0)),
                  pl.BlockSpec((4, span, LANES), lambda i: (0, jnp.maximum(i * n_prev - 1, 0), 0))],
        out_specs=(blk(2), blk(2)),
        out_shape=(jax.ShapeDtypeStruct((2, s, LANES), F32),) * 2,
        compiler_params=_params("arbitrary"),
    )(qkv, qkv, qkv)


def _merge_core(x_ref, hm_ref, hx_ref, gate_ref, od_refs, lse_refs, wm_ref, wdd_ref, wx_ref, wo_ref):
    wide = lambda r: jnp.concatenate([r[0], r[1]], axis=-1)
    lses = [wide(r) for r in lse_refs]
    mx = jnp.maximum(jnp.maximum(lses[0], lses[1]), lses[2])
    es = [jnp.exp(l - mx) for l in lses]
    den = es[0] + es[1] + es[2]
    hd = (es[0] * wide(od_refs[0]) + es[1] * wide(od_refs[1]) + es[2] * wide(od_refs[2])) / den
    d = D_MODEL
    merged = (gate_ref[:, 0:d].astype(F32) * _dot(hm_ref[...], wm_ref[...])
              + gate_ref[:, d:2 * d].astype(F32) * _dot(hd.astype(BF16), wdd_ref[...])
              + gate_ref[:, 2 * d:3 * d].astype(F32) * _dot(hx_ref[...], wx_ref[...]))
    return x_ref[...] + _dot(merged.astype(BF16), wo_ref[...])


def _merge_dense_kernel(x_ref, hm_ref, hx_ref, gate_ref, o0, o1, o2, l0, l1, l2,
                        wm_ref, wdd_ref, wx_ref, wo_ref, gf_ref, wg_ref, wu_ref, wdn_ref,
                        out_ref, acc_ref):
    x1 = _merge_core(x_ref, hm_ref, hx_ref, gate_ref, (o0, o1, o2), (l0, l1, l2),
                     wm_ref, wdd_ref, wx_ref, wo_ref)
    u = _rms(x1, gf_ref[...]).astype(BF16)
    acc_ref[...] = x1

    def body(c, carry):
        cols = pl.ds(pl.multiple_of(c * FF_CHUNK, FF_CHUNK), FF_CHUNK)
        g = _dot(u, wg_ref[:, cols])
        hcol = (g * jax.nn.sigmoid(g) * _dot(u, wu_ref[:, cols])).astype(BF16)
        acc_ref[...] += _dot(hcol, wdn_ref[cols, :])
        return carry

    lax.fori_loop(0, D_FF // FF_CHUNK, body, 0)
    out_ref[...] = acc_ref[...]


def _merge_moe_kernel(x_ref, hm_ref, hx_ref, gate_ref, o0, o1, o2, l0, l1, l2,
                      wm_ref, wdd_ref, wx_ref, wo_ref, gf_ref, wr_ref,
                      x1_out, u_out, route_out, cnt_out, carry_ref):
    tm = x_ref.shape[0]

    @pl.when(pl.program_id(0) == 0)
    def _():
        carry_ref[...] = jnp.zeros(carry_ref.shape, F32)

    x1 = _merge_core(x_ref, hm_ref, hx_ref, gate_ref, (o0, o1, o2), (l0, l1, l2),
                     wm_ref, wdd_ref, wx_ref, wo_ref)
    x1_out[...] = x1
    uf = _rms(x1, gf_ref[...])
    u_out[...] = uf

    uh, um, ul = _split3(uf)
    wh, wmid, wl = _split3(wr_ref[...])
    logits = (_dot(uh, wh) + (_dot(uh, wmid) + _dot(um, wh))
              + (_dot(uh, wl) + _dot(um, wmid) + _dot(ul, wh)))
    lane = lax.broadcasted_iota(jnp.int32, (tm, LANES), 1).astype(F32)
    valid = lane < N_EXPERTS
    lg = jnp.where(valid, logits, NEG)
    ex = jnp.exp(lg - jnp.max(lg, axis=-1, keepdims=True))
    probs = jnp.where(valid, ex / jnp.sum(ex, axis=-1, keepdims=True), -1.0)
    p1 = jnp.max(probs, axis=-1, keepdims=True)
    i1 = jnp.min(jnp.where(probs == p1, lane, float(LANES)), axis=-1, keepdims=True)
    rest = jnp.where(lane == i1, -1.0, probs)
    p2 = jnp.max(rest, axis=-1, keepdims=True)
    i2 = jnp.min(jnp.where(rest == p2, lane, float(LANES)), axis=-1, keepdims=True)
    g1 = p1 / (p1 + p2)
    g2 = p2 / (p1 + p2)
    sel = jnp.where(lane == i1, 1.0, jnp.where(lane == i2, 1.0, 0.0))
    row = lax.broadcasted_iota(jnp.int32, (tm, tm), 0)
    col = lax.broadcasted_iota(jnp.int32, (tm, tm), 1)
    before = jnp.where(col < row, 1.0, 0.0).astype(BF16)
    ranks = _dot(before, sel.astype(BF16)) + carry_ref[0:1, :]
    r1 = jnp.sum(jnp.where(lane == i1, ranks, 0.0), axis=-1, keepdims=True)
    r2 = jnp.sum(jnp.where(lane == i2, ranks, 0.0), axis=-1, keepdims=True)
    carry_ref[...] = carry_ref[...] + jnp.sum(sel, axis=0, keepdims=True)
    cnt_out[...] = carry_ref[...]
    route = jnp.where(lane == 0, i1, jnp.where(lane == 1, i2, jnp.where(lane == 2, g1,
            jnp.where(lane == 3, g2, jnp.where(lane == 4, r1, jnp.where(lane == 5, r2, 0.0))))))
    route_out[...] = route


def _merge(x, hm, hx, gates, ods, lses, wm, wdd, wx, wo, gf, dense_w=None, w_router=None):
    s = x.shape[0]
    tm = TOK_TILE
    row = lambda w: pl.BlockSpec((tm, w), lambda i: (i, 0))
    acts = (x, hm, hx, gates) + tuple(ods) + tuple(lses)
    slab = pl.BlockSpec((2, tm, LANES), lambda i: (0, i, 0))
    act_specs = [row(D_MODEL), row(MLSTM_W), row(MEM_W), row(3 * D_MODEL)] + [slab] * 6
    if dense_w is not None:
        consts = (wm, wdd, wx, wo, gf) + tuple(dense_w)
        return pl.pallas_call(
            _merge_dense_kernel,
            grid=(s // tm,),
            in_specs=act_specs + [_const_spec(c.shape) for c in consts],
            out_specs=row(D_MODEL),
            out_shape=jax.ShapeDtypeStruct((s, D_MODEL), F32),
            scratch_shapes=[pltpu.VMEM((tm, D_MODEL), F32)],
            compiler_params=_params("arbitrary"),
        )(*acts, *consts)
    consts = (wm, wdd, wx, wo, gf, w_router)
    return pl.pallas_call(
        _merge_moe_kernel,
        grid=(s // tm,),
        in_specs=act_specs + [_const_spec(c.shape) for c in consts],
        out_specs=(row(D_MODEL), row(D_MODEL), row(LANES), pl.BlockSpec((8, LANES), lambda i: (0, 0))),
        out_shape=(jax.ShapeDtypeStruct((s, D_MODEL), F32),
                   jax.ShapeDtypeStruct((s, D_MODEL), F32),
                   jax.ShapeDtypeStruct((s, LANES), F32),
                   jax.ShapeDtypeStruct((8, LANES), F32)),
        scratch_shapes=[pltpu.VMEM((8, LANES), F32)],
        compiler_params=_params("arbitrary"),
    )(*acts, *consts)


def _for_rows(n, fn):
    def body(k, c):
        for j in range(SUBLANES):
            fn(k, j)
        return c
    lax.fori_loop(0, n // SUBLANES, body, 0)


def _dispatch_kernel(zb_ref, p1_ref, p2_ref, u_hbm, xs_hbm, zero_ref, sem, zsem):
    i = pl.program_id(0)
    td = p1_ref.shape[1]
    bm = zero_ref.shape[0]

    @pl.when(i == 0)
    def _():
        zero_ref[...] = jnp.zeros(zero_ref.shape, F32)

        def fill(op):
            def body(b, c):
                @pl.when(zb_ref[b] != 0)
                def _():
                    op(pltpu.make_async_copy(zero_ref, xs_hbm.at[pl.ds(b * bm, bm), :], zsem))
                return c
            lax.fori_loop(0, zb_ref.shape[0], body, 0)

        fill(lambda c: c.start())
        fill(lambda c: c.wait())

    def copies(k, j):
        r = k * SUBLANES + j
        src = u_hbm.at[pl.ds(i * td + r, 1), :]
        return (pltpu.make_async_copy(src, xs_hbm.at[pl.ds(p1_ref[0, r], 1), :], sem),
                pltpu.make_async_copy(src, xs_hbm.at[pl.ds(p2_ref[0, r], 1), :], sem))

    def start(k, j):
        a, b = copies(k, j)
        a.start()
        b.start()

    def wait(k, j):
        a, b = copies(k, j)
        a.wait()
        b.wait()

    _for_rows(td, start)
    _for_rows(td, wait)


def _dispatch(u, p1, p2, zb, n_rows):
    s = u.shape[0]
    td = MOE_TD
    tok = lambda: pl.BlockSpec((None, 1, td), lambda i, zb: (i, 0, 0), memory_space=pltpu.SMEM)
    grid_spec = pltpu.PrefetchScalarGridSpec(
        num_scalar_prefetch=1,
        grid=(s // td,),
        in_specs=[tok(), tok(), pl.BlockSpec(memory_space=pl.ANY)],
        out_specs=pl.BlockSpec(memory_space=pl.ANY),
        scratch_shapes=[pltpu.VMEM((MOE_BM, D_MODEL), F32),
                        pltpu.SemaphoreType.DMA(()), pltpu.SemaphoreType.DMA(())],
    )
    return pl.pallas_call(
        _dispatch_kernel,
        grid_spec=grid_spec,
        out_shape=jax.ShapeDtypeStruct((n_rows, D_MODEL), F32),
        compiler_params=_params("arbitrary"),
    )(zb, p1.reshape(s // td, 1, td), p2.reshape(s // td, 1, td), u)


def _expert_kernel(be_ref, bv_ref, bx_ref, x_ref, wg_ref, wu_ref, wd_ref, y_ref, xb_ref, acc_ref):
    b = pl.program_id(0)
    f = pl.program_id(1)
    nf = pl.num_programs(1)
    valid = bv_ref[b] != 0

    @pl.when(jnp.logical_and(valid, f == 0))
    def _():
        xb_ref[...] = x_ref[...].astype(BF16)

    @pl.when(valid)
    def _():
        x = xb_ref[...]
        g = _dot(x, wg_ref[...])
        hcol = (g * jax.nn.sigmoid(g) * _dot(x, wu_ref[...])).astype(BF16)
        part = _dot(hcol, wd_ref[...])

        @pl.when(f == 0)
        def _():
            acc_ref[...] = part

        @pl.when(jnp.logical_and(f != 0, f != nf - 1))
        def _():
            acc_ref[...] += part

        @pl.when(f == nf - 1)
        def _():
            y_ref[...] = acc_ref[...] + part

    @pl.when(jnp.logical_and(jnp.logical_not(valid), f == nf - 1))
    def _():
        y_ref[...] = jnp.zeros(y_ref.shape, F32)


def _experts(xs, wg, wu, wd, be, bv, bx):
    n_rows = xs.shape[0]
    bm, tf = MOE_BM, MOE_TF
    nf = D_FF_EXPERT // tf
    assert nf >= 2

    def fidx(b, f, bv):
        return jnp.where(bv[b] != 0, f, nf - 1)

    grid_spec = pltpu.PrefetchScalarGridSpec(
        num_scalar_prefetch=3,
        grid=(n_rows // bm, nf),
        in_specs=[pl.BlockSpec((bm, D_MODEL), lambda b, f, be, bv, bx: (bx[b], 0)),
                  pl.BlockSpec((None, D_MODEL, tf), lambda b, f, be, bv, bx: (be[b], 0, fidx(b, f, bv))),
                  pl.BlockSpec((None, D_MODEL, tf), lambda b, f, be, bv, bx: (be[b], 0, fidx(b, f, bv))),
                  pl.BlockSpec((None, tf, D_MODEL), lambda b, f, be, bv, bx: (be[b], fidx(b, f, bv), 0))],
        out_specs=pl.BlockSpec((bm, D_MODEL), lambda b, f, be, bv, bx: (b, 0)),
        scratch_shapes=[pltpu.VMEM((bm, D_MODEL), BF16),
                        pltpu.VMEM((bm, D_MODEL), F32)],
    )
    return pl.pallas_call(
        _expert_kernel,
        grid_spec=grid_spec,
        out_shape=jax.ShapeDtypeStruct((n_rows, D_MODEL), F32),
        compiler_params=_params("arbitrary", "arbitrary"),
    )(be, bv, bx, xs, wg, wu, wd)


def _combine_kernel(p1_ref, p2_ref, p1n_ref, p2n_ref, x_ref, route_ref, gn_ref, ys_hbm, out_ref,
                    yg_ref, sem, *, final_norm):
    i = pl.program_id(0)
    tc = x_ref.shape[0]
    slot = lax.rem(i, 2)

    def gather(pa_ref, pb_ref, s_, op):
        def rows(k, j):
            r = k * SUBLANES + j
            op(pltpu.make_async_copy(ys_hbm.at[pl.ds(pa_ref[0, r], 1), :],
                                     yg_ref.at[s_, 0, k, pl.ds(j, 1), :], sem.at[s_]))
            op(pltpu.make_async_copy(ys_hbm.at[pl.ds(pb_ref[0, r], 1), :],
                                     yg_ref.at[s_, 1, k, pl.ds(j, 1), :], sem.at[s_]))
        _for_rows(tc, rows)

    @pl.when(i == 0)
    def _():
        gather(p1_ref, p2_ref, 0, lambda c: c.start())

    gather(p1_ref, p2_ref, slot, lambda c: c.wait())

    @pl.when(i + 1 < pl.num_programs(0))
    def _():
        gather(p1n_ref, p2n_ref, 1 - slot, lambda c: c.start())

    y1 = yg_ref[slot, 0].reshape(tc, D_MODEL)
    y2 = yg_ref[slot, 1].reshape(tc, D_MODEL)
    out = x_ref[...] + route_ref[:, 2:3] * y1 + route_ref[:, 3:4] * y2
    out_ref[...] = _rms(out, gn_ref[...]) if final_norm else out


def _combine(x1, ys, p1, p2, route, gn, final_norm):
    s = x1.shape[0]
    tc = MOE_TC
    nt = s // tc
    row = lambda w: pl.BlockSpec((tc, w), lambda i: (i, 0))
    cur = lambda: pl.BlockSpec((None, 1, tc), lambda i: (i, 0, 0), memory_space=pltpu.SMEM)
    nxt = lambda: pl.BlockSpec((None, 1, tc), lambda i: (jnp.minimum(i + 1, nt - 1), 0, 0),
                               memory_space=pltpu.SMEM)
    p1 = p1.reshape(nt, 1, tc)
    p2 = p2.reshape(nt, 1, tc)
    return pl.pallas_call(
        functools.partial(_combine_kernel, final_norm=final_norm),
        grid=(nt,),
        in_specs=[cur(), cur(), nxt(), nxt(), row(D_MODEL), row(LANES), _const_spec(gn.shape),
                  pl.BlockSpec(memory_space=pl.ANY)],
        out_specs=row(D_MODEL),
        out_shape=jax.ShapeDtypeStruct((s, D_MODEL), F32),
        scratch_shapes=[pltpu.VMEM((2, 2, tc // SUBLANES, SUBLANES, D_MODEL), F32),
                        pltpu.SemaphoreType.DMA((2,))],
        compiler_params=_params("arbitrary"),
    )(p1, p2, p1, p2, x1, route, gn, ys)


def _moe_plan(route, counts, s):
    bm = MOE_BM
    nb = (2 * s) // bm + N_EXPERTS
    i1 = route[:, 0].astype(jnp.int32)
    i2 = route[:, 1].astype(jnp.int32)
    r1 = route[:, 4].astype(jnp.int32)
    r2 = route[:, 5].astype(jnp.int32)
    cnt = counts[0, :N_EXPERTS].astype(jnp.int32)
    padded = ((cnt + bm - 1) // bm) * bm
    ends = jnp.cumsum(padded)
    off = ends - padded
    p1 = off[i1] + r1
    p2 = off[i2] + r2
    nb_used = ends[-1] // bm

    bidx = jnp.arange(nb, dtype=jnp.int32)
    bvalid = (bidx < nb_used).astype(jnp.int32)
    bsrc = jnp.minimum(bidx, jnp.maximum(nb_used - 1, 0))
    bexp = jnp.sum((bsrc[:, None] * bm >= ends[None, :]).astype(jnp.int32), axis=1)
    bexp = jnp.minimum(bexp, N_EXPERTS - 1)
    real = jnp.clip(off[bexp] + cnt[bexp] - bidx * bm, 0, bm)
    bzero = ((real < bm) | (bvalid == 0)).astype(jnp.int32)
    return p1, p2, bexp, bvalid, bsrc, bzero


def _dil_weights(w_in, q_scale):
    cols = []
    for g in range(len(DIL_PATTERNS)):
        sl = lambda off: w_in[:, off + g * DIL_GW: off + (g + 1) * DIL_GW]
        cols += [sl(OFF_KD), sl(OFF_VD), sl(OFF_QD) * q_scale]
    return jnp.concatenate(cols, axis=-1).astype(BF16)


def kernel(x, mem, norm_mix, w_in, conv_qk, b_gate_if, mlstm_norm, norm_mem, w_mem_kv, w_br_m, w_br_d,
           w_br_x, w_out, norm_ffn, ffn_w_gate, ffn_w_up, ffn_w_down, moe_router, moe_w_gate, moe_w_up,
           moe_w_down, norm_final):
    s = x.shape[1]
    xs = x.reshape(s, D_MODEL)
    mem2 = mem.reshape(N_MEM, D_MODEL)
    row = lambda a: a.reshape(1, -1)
    q_scale = DIL_DH ** -0.5

    for layer in range(DEPTH):
        wl = w_in[layer]
        g_mix = row(norm_mix[layer])
        km, vm = _memkv(mem2, row(norm_mem[layer]), w_mem_kv[layer].astype(BF16))

        w_if = wl[:, OFF_IF:OFF_QD]
        wif = jnp.pad(w_if, ((0, 0), (0, LANES - 2 * MLSTM_HEADS))).astype(BF16)
        wift = w_if.T.astype(BF16)
        bif = jnp.pad(b_gate_if[layer], (0, LANES - 2 * MLSTM_HEADS)).reshape(1, LANES)
        bift = b_gate_if[layer].reshape(2 * MLSTM_HEADS, 1)
        (q_m, k_m, v_m, o_m, ifc, ifr, d0, d1, d2, h_x, gates) = _inproj(
            xs, g_mix, wl[:, :OFF_IF].astype(BF16), wif, wift, bif, bift, conv_qk[layer],
            _dil_weights(wl, q_scale), wl[:, OFF_QX:OFF_GATE].astype(BF16),
            wl[:, OFF_GATE:].astype(BF16), km, vm)

        h_m = _mlstm(q_m, k_m, v_m, o_m, ifc, ifr, row(mlstm_norm[layer]))

        ods, lses = [], []
        for qkv, (_, dil) in zip((d0, d1, d2), DIL_PATTERNS):
            o_g, lse_g = _dil_attn(qkv, dil)
            ods.append(o_g)
            lses.append(lse_g)

        merge_w = (w_br_m[layer].astype(BF16), w_br_d[layer].astype(BF16), w_br_x[layer].astype(BF16),
                   w_out[layer].astype(BF16), row(norm_ffn[layer]))
        if layer % 2 == 0:
            li = layer // 2
            dense_w = (ffn_w_gate[li].astype(BF16), ffn_w_up[li].astype(BF16), ffn_w_down[li].astype(BF16))
            xs = _merge(xs, h_m, h_x, gates, ods, lses, *merge_w, dense_w=dense_w)
        else:
            li = layer // 2
            wr = jnp.pad(moe_router[li], ((0, 0), (0, LANES - N_EXPERTS)))
            x1, u, route, counts = _merge(xs, h_m, h_x, gates, ods, lses, *merge_w, w_router=wr)
            p1, p2, bexp, bvalid, bsrc, bzero = _moe_plan(route, counts, s)
            rows = _dispatch(u, p1, p2, bzero, bexp.shape[0] * MOE_BM)
            y = _experts(rows, moe_w_gate[li].astype(BF16), moe_w_up[li].astype(BF16),
                         moe_w_down[li].astype(BF16), bexp, bvalid, bsrc)
            xs = _combine(x1, y, p1, p2, route, row(norm_final), final_norm=layer == DEPTH - 1)
    return xs.reshape(x.shape)
```

```python
import functools

import jax
import jax.numpy as jnp
from jax import lax
from jax.experimental import pallas as pl
from jax.experimental.pallas import tpu as pltpu

F32 = jnp.float32
BF16 = jnp.bfloat16

EPS = 1e-6
D_MODEL = 1024
DEPTH = 4
N_MEM = 256
MLSTM_HEADS = 4
MLSTM_DH = 128
MLSTM_W = MLSTM_HEADS * MLSTM_DH
MLSTM_CHUNK = 128
CONV_W = 4
M_INIT = -1e30
DIL_PATTERNS = ((128, 1), (512, 4), (2048, 16))
DIL_HEADS = 4
DIL_DH = 64
DIL_GW = DIL_HEADS * DIL_DH
DIL_W = 3 * DIL_GW
Q_BLOCK = 128
MEM_HEADS = 4
MEM_DH = 128
MEM_W = MEM_HEADS * MEM_DH
D_FF = 2816
N_EXPERTS = 8
D_FF_EXPERT = 3584

OFF_IF = 4 * MLSTM_W
OFF_QD = OFF_IF + 2 * MLSTM_HEADS
OFF_KD = OFF_QD + DIL_W
OFF_VD = OFF_KD + DIL_W
OFF_QX = OFF_VD + DIL_W
OFF_GATE = OFF_QX + MEM_W
IN_COLS = OFF_GATE + 3 * D_MODEL

LANES = 128
SUBLANES = 8
NEG = -1e30
VMEM_LIMIT = 56 * 1024 * 1024

DIL_SLABS = 3 * DIL_GW // LANES
MLSTM_CHUNKS_PER_STEP = 4
ATT_UNROLL = 4
ATT_SUPER = 2048
TOK_TILE = 512
FF_CHUNK = 256
MOE_BM = 512
MOE_TD = 512
MOE_TC = 256
MOE_TF = 1792

NT_DIMS = (((1,), (1,)), ((), ()))
TN_DIMS = (((0,), (0,)), ((), ()))


def _params(*sem):
    return pltpu.CompilerParams(dimension_semantics=sem, vmem_limit_bytes=VMEM_LIMIT)


def _dot(a, b):
    return jnp.dot(a, b, preferred_element_type=F32)


def _dot_nt(a, b):
    return lax.dot_general(a, b, NT_DIMS, preferred_element_type=F32)


def _rms(x, g):
    return x * lax.rsqrt(jnp.mean(x * x, axis=-1, keepdims=True) + EPS) * g


def _split3(x):
    hi = x.astype(BF16)
    r1 = x - hi.astype(F32)
    mid = r1.astype(BF16)
    lo = (r1 - mid.astype(F32)).astype(BF16)
    return hi, mid, lo


def _const_spec(shape):
    nd = len(shape)
    return pl.BlockSpec(shape, lambda *_: (0,) * nd, pipeline_mode=pl.Buffered(1))


def _memkv_kernel(mem_ref, g_ref, w_ref, k_ref, v_ref):
    u = _rms(mem_ref[...], g_ref[...]).astype(BF16)
    kv = _dot(u, w_ref[...])
    k_ref[...] = kv[:, :MEM_W].astype(BF16)
    v_ref[...] = kv[:, MEM_W:].astype(BF16)


def _memkv(mem, g, w_kv):
    return pl.pallas_call(
        _memkv_kernel,
        out_shape=(jax.ShapeDtypeStruct((N_MEM, MEM_W), BF16),) * 2,
        compiler_params=pltpu.CompilerParams(vmem_limit_bytes=VMEM_LIMIT),
    )(mem, g, w_kv)


def _inproj_kernel(x_ref, g_ref, wa_ref, wif_ref, wift_ref, bif_ref, bift_ref, cw_ref,
                   wd_ref, wqx_ref, wg_ref, km_ref, vm_ref,
                   q_out, k_out, v_out, o_out, if_out, ift_out, d0_out, d1_out, d2_out,
                   hx_out, gate_out, conv_buf):
    tm = x_ref.shape[0]
    u = _rms(x_ref[...], g_ref[...]).astype(BF16)

    @pl.when(pl.program_id(0) == 0)
    def _():
        conv_buf[0:8, :] = jnp.zeros((8, 2 * MLSTM_W), F32)

    conv_buf[8:tm + 8, :] = _dot(u, wa_ref[:, 0:2 * MLSTM_W])
    acc = cw_ref[0:1, :] * conv_buf[pl.ds(8 - (CONV_W - 1), tm), :]
    for j in range(1, CONV_W):
        acc = acc + cw_ref[j:j + 1, :] * conv_buf[pl.ds(8 - (CONV_W - 1) + j, tm), :]
    conv_buf[0:8, :] = conv_buf[tm:tm + 8, :]
    qk = acc * jax.nn.sigmoid(acc)
    q_out[...] = qk[:, :MLSTM_W].astype(BF16)
    k_out[...] = (qk[:, MLSTM_W:] * (MLSTM_DH ** -0.5)).astype(BF16)

    v_out[...] = _dot(u, wa_ref[:, 2 * MLSTM_W:3 * MLSTM_W]).astype(BF16)
    o_out[...] = jax.nn.sigmoid(_dot(u, wa_ref[:, 3 * MLSTM_W:4 * MLSTM_W])).astype(BF16)

    if_out[...] = _dot(u, wif_ref[...]) + bif_ref[...]
    ift_out[...] = _dot_nt(wift_ref[...], u) + bift_ref[...]

    for gi, d_out in enumerate((d0_out, d1_out, d2_out)):
        d = _dot(u, wd_ref[:, gi * 3 * DIL_GW:(gi + 1) * 3 * DIL_GW])
        for j in range(DIL_SLABS):
            d_out[j] = d[:, j * LANES:(j + 1) * LANES]

    qx = (_dot(u, wqx_ref[...]) * (MEM_DH ** -0.5)).astype(BF16)
    outs = []
    for h in range(MEM_HEADS):
        sl = slice(h * MEM_DH, (h + 1) * MEM_DH)
        s = _dot_nt(qx[:, sl], km_ref[:, sl])
        p = jnp.exp(s - jnp.max(s, axis=-1, keepdims=True))
        den = jnp.sum(p, axis=-1, keepdims=True)
        outs.append(_dot(p.astype(BF16), vm_ref[:, sl]) / den)
    hx_out[...] = jnp.concatenate(outs, axis=-1).astype(BF16)

    gate_out[...] = jax.nn.sigmoid(_dot(u, wg_ref[...])).astype(BF16)


def _inproj(x, g, wa, wif, wift, bif, bift, cw, wd0, wqx, wgate, km, vm):
    s = x.shape[0]
    tm = TOK_TILE
    row = lambda w: pl.BlockSpec((tm, w), lambda i: (i, 0))
    out_shape = (
        jax.ShapeDtypeStruct((s, MLSTM_W), BF16),
        jax.ShapeDtypeStruct((s, MLSTM_W), BF16),
        jax.ShapeDtypeStruct((s, MLSTM_W), BF16),
        jax.ShapeDtypeStruct((s, MLSTM_W), BF16),
        jax.ShapeDtypeStruct((s, LANES), F32),
        jax.ShapeDtypeStruct((8, s), F32),
        jax.ShapeDtypeStruct((DIL_SLABS, s, LANES), F32),
        jax.ShapeDtypeStruct((DIL_SLABS, s, LANES), F32),
        jax.ShapeDtypeStruct((DIL_SLABS, s, LANES), F32),
        jax.ShapeDtypeStruct((s, MEM_W), BF16),
        jax.ShapeDtypeStruct((s, 3 * D_MODEL), BF16),
    )
    slab = pl.BlockSpec((DIL_SLABS, tm, LANES), lambda i: (0, i, 0))
    out_specs = (row(MLSTM_W), row(MLSTM_W), row(MLSTM_W), row(MLSTM_W), row(LANES),
                 pl.BlockSpec((8, tm), lambda i: (0, i)),
                 slab, slab, slab, row(MEM_W), row(3 * D_MODEL))
    in_specs = [row(D_MODEL)] + [_const_spec(a.shape) for a in
                                 (g, wa, wif, wift, bif, bift, cw, wd0, wqx, wgate, km, vm)]
    return pl.pallas_call(
        _inproj_kernel,
        grid=(s // tm,),
        in_specs=in_specs,
        out_specs=out_specs,
        out_shape=out_shape,
        scratch_shapes=[pltpu.VMEM((tm + 8, 2 * MLSTM_W), F32)],
        compiler_params=_params("arbitrary"),
    )(x, g, wa, wif, wift, bif, bift, cw, wd0, wqx, wgate, km, vm)


def _log_sigmoid(x):
    return jnp.minimum(x, 0.0) - jnp.log(1.0 + jnp.exp(-jnp.abs(x)))


def _mlstm_kernel(q_ref, k_ref, v_ref, o_ref, ifc_ref, ifr_ref, g_ref, out_ref, ct_ref, m_ref):
    L = MLSTM_CHUNK
    H = MLSTM_HEADS

    @pl.when(pl.program_id(0) == 0)
    def _():
        ct_ref[...] = jnp.zeros(ct_ref.shape, F32)
        m_ref[...] = jnp.full(m_ref.shape, M_INIT, F32)

    row = lax.broadcasted_iota(jnp.int32, (L, L), 0)
    col = lax.broadcasted_iota(jnp.int32, (L, L), 1)
    causal = col <= row
    tril = jnp.where(causal, 1.0, 0.0).astype(BF16)
    triu = jnp.where(row <= col, 1.0, 0.0).astype(BF16)
    ones_col = jnp.where(col == 0, 1.0, 0.0).astype(BF16)

    for ci in range(q_ref.shape[0] // L):
        rows = slice(ci * L, (ci + 1) * L)
        ifc = ifc_ref[rows, :]
        ifr = ifr_ref[:, rows]
        cum_c = sum(_dot(tril, p) for p in _split3(_log_sigmoid(ifc)))
        cum_r = sum(_dot(p, triu) for p in _split3(_log_sigmoid(ifr)))

        for h in range(H):
            sl = slice(h * MLSTM_DH, (h + 1) * MLSTM_DH)
            i_c = ifc[:, h:h + 1]
            i_r = ifr[h:h + 1, :]
            cc = cum_c[:, H + h:H + h + 1]
            cr = cum_r[H + h:H + h + 1, :]
            total = cr[:, L - 1:L]
            m_prev = m_ref[h:h + 1, 0:1]

            dm = jnp.where(causal, cc - cr + i_r, -jnp.inf)
            inter = cc + m_prev
            m_row = jnp.maximum(jnp.max(dm, axis=-1, keepdims=True), inter)
            w_intra = jnp.exp(dm - m_row)
            w_inter = jnp.exp(inter - m_row)

            qh = q_ref[rows, sl]
            kh = k_ref[rows, sl]
            vaug = jnp.concatenate([v_ref[rows, sl], ones_col], axis=-1)
            s_mat = _dot_nt(qh, kh) * w_intra
            tot = _dot(s_mat.astype(BF16), vaug) + w_inter * _dot(qh, ct_ref[h].astype(BF16))
            den = tot[:, MLSTM_DH:MLSTM_DH + 1]
            h_out = tot[:, :MLSTM_DH] / jnp.maximum(jnp.abs(den), jnp.exp(-m_row))

            g_end = total - cc + i_c
            m_new = jnp.maximum(total + m_prev, jnp.max(g_end, axis=0, keepdims=True))
            w_end = jnp.exp(g_end - m_new)
            decay = jnp.exp(total + m_prev - m_new)
            vw = (vaug.astype(F32) * w_end).astype(BF16)
            ct_ref[h] = decay * ct_ref[h] + lax.dot_general(kh, vw, TN_DIMS, preferred_element_type=F32)
            m_ref[h:h + 1, :] = jnp.broadcast_to(m_new, (1, LANES))

            mu = jnp.mean(h_out, axis=-1, keepdims=True)
            cen = h_out - mu
            var = jnp.mean(cen * cen, axis=-1, keepdims=True)
            y = cen * lax.rsqrt(var + EPS) * g_ref[:, sl] * o_ref[rows, sl].astype(F32)
            out_ref[rows, sl] = y.astype(BF16)


def _mlstm(q, k, v, o, ifc, ifr, g):
    s = q.shape[0]
    L = MLSTM_CHUNK * MLSTM_CHUNKS_PER_STEP
    row = pl.BlockSpec((L, MLSTM_W), lambda c: (c, 0))
    return pl.pallas_call(
        _mlstm_kernel,
        grid=(s // L,),
        in_specs=[row, row, row, row,
                  pl.BlockSpec((L, LANES), lambda c: (c, 0)),
                  pl.BlockSpec((8, L), lambda c: (0, c)),
                  _const_spec(g.shape)],
        out_specs=row,
        out_shape=jax.ShapeDtypeStruct((s, MLSTM_W), BF16),
        scratch_shapes=[pltpu.VMEM((MLSTM_HEADS, MLSTM_DH, 2 * MLSTM_DH), F32),
                        pltpu.VMEM((8, LANES), F32)],
        compiler_params=_params("arbitrary"),
    )(q, k, v, o, ifc, ifr, g)


def _dil_attn_kernel(kv_ref, q_ref, kvp_ref, o_ref, lse_ref, *, dil):
    B = Q_BLOCK
    span = B * dil
    n_sub = ATT_SUPER // span
    row = lax.broadcasted_iota(jnp.int32, (B, 2 * B), 0)
    col = lax.broadcasted_iota(jnp.int32, (B, 2 * B), 1)
    band = jnp.where(col >= row, jnp.where(col <= row + B, 0.0, NEG), NEG)
    first = jnp.where(pl.program_id(0) == 0, 1.0, 0.0)
    band_first = band + first * jnp.where(col < B, NEG, 0.0)
    lane = lax.broadcasted_iota(jnp.int32, (B, LANES), 1)
    lo = lane < DIL_DH
    hi = lane >= DIL_DH

    def rows(start):
        return pl.ds(start, B, stride=dil) if dil > 1 else pl.ds(start, B)

    def unit(cur_start, prev_ref, prev_start, bias):
        for half in range(2):
            q2 = q_ref[half, rows(cur_start), :]
            k2 = jnp.concatenate([prev_ref[half, rows(prev_start), :],
                                  kv_ref[half, rows(cur_start), :]], axis=0).astype(BF16)
            v2 = jnp.concatenate([prev_ref[2 + half, rows(prev_start), :],
                                  kv_ref[2 + half, rows(cur_start), :]], axis=0).astype(BF16)
            res = []
            for keep in (lo, hi):
                qm = jnp.where(keep, q2, 0.0).astype(BF16)
                s = _dot_nt(qm, k2) + bias
                mx = jnp.max(s, axis=-1, keepdims=True)
                p = jnp.exp(s - mx)
                den = jnp.sum(p, axis=-1, keepdims=True)
                res.append((_dot(p.astype(BF16), v2) / den, mx + jnp.log(den)))
            o_ref[half, rows(cur_start), :] = jnp.where(lo, res[0][0], res[1][0])
            lse_ref[half, rows(cur_start), :] = jnp.where(lo, res[0][1], res[1][1])

    def per_residue(r, carry):
        unit(r, kvp_ref, r, band_first)

        def per_sub(j, c):
            unit(j * span + r, kv_ref, (j - 1) * span + r, band)
            return c

        if 1 < n_sub <= ATT_UNROLL:
            for j in range(1, n_sub):
                per_sub(j, 0)
        elif n_sub > 1:
            lax.fori_loop(1, n_sub, per_sub, 0, unroll=ATT_UNROLL)
        return carry

    if dil >= ATT_UNROLL:
        lax.fori_loop(0, dil, per_residue, 0, unroll=max(1, ATT_UNROLL // n_sub))
    else:
        for r in range(dil):
            per_residue(r, 0)


def _dil_attn(qkv, dil):
    s = qkv.shape[1]
    span = Q_BLOCK * dil
    n_prev = ATT_SUPER // span
    blk = lambda n: pl.BlockSpec((n, ATT_SUPER, LANES), lambda i: (0, i, 0))
    return pl.pallas_call(
        functools.partial(_dil_attn_kernel, dil=dil),
        grid=(s // ATT_SUPER,),
        in_specs=[blk(4),
                  pl.BlockSpec((2, ATT_SUPER, LANES), lambda i: (2, i, 0)),
                  pl.BlockSpec((4, span, LANES), lambda i: (0, jnp.maximum(i * n_prev - 1, 0), 0))],
        out_specs=(blk(2), blk(2)),
        out_shape=(jax.ShapeDtypeStruct((2, s, LANES), F32),) * 2,
        compiler_params=_params("arbitrary"),
    )(qkv, qkv, qkv)


def _merge_core(x_ref, hm_ref, hx_ref, gate_ref, od_refs, lse_refs, wm_ref, wdd_ref, wx_ref, wo_ref):
    wide = lambda r: jnp.concatenate([r[0], r[1]], axis=-1)
    lses = [wide(r) for r in lse_refs]
    mx = jnp.maximum(jnp.maximum(lses[0], lses[1]), lses[2])
    es = [jnp.exp(l - mx) for l in lses]
    den = es[0] + es[1] + es[2]
    hd = (es[0] * wide(od_refs[0]) + es[1] * wide(od_refs[1]) + es[2] * wide(od_refs[2])) / den
    d = D_MODEL
    merged = (gate_ref[:, 0:d].astype(F32) * _dot(hm_ref[...], wm_ref[...])
              + gate_ref[:, d:2 * d].astype(F32) * _dot(hd.astype(BF16), wdd_ref[...])
              + gate_ref[:, 2 * d:3 * d].astype(F32) * _dot(hx_ref[...], wx_ref[...]))
    return x_ref[...] + _dot(merged.astype(BF16), wo_ref[...])


def _merge_dense_kernel(x_ref, hm_ref, hx_ref, gate_ref, o0, o1, o2, l0, l1, l2,
                        wm_ref, wdd_ref, wx_ref, wo_ref, gf_ref, wg_ref, wu_ref, wdn_ref,
                        out_ref, acc_ref):
    x1 = _merge_core(x_ref, hm_ref, hx_ref, gate_ref, (o0, o1, o2), (l0, l1, l2),
                     wm_ref, wdd_ref, wx_ref, wo_ref)
    u = _rms(x1, gf_ref[...]).astype(BF16)
    acc_ref[...] = x1

    def body(c, carry):
        cols = pl.ds(pl.multiple_of(c * FF_CHUNK, FF_CHUNK), FF_CHUNK)
        g = _dot(u, wg_ref[:, cols])
        hcol = (g * jax.nn.sigmoid(g) * _dot(u, wu_ref[:, cols])).astype(BF16)
        acc_ref[...] += _dot(hcol, wdn_ref[cols, :])
        return carry

    lax.fori_loop(0, D_FF // FF_CHUNK, body, 0)
    out_ref[...] = acc_ref[...]


def _merge_moe_kernel(x_ref, hm_ref, hx_ref, gate_ref, o0, o1, o2, l0, l1, l2,
                      wm_ref, wdd_ref, wx_ref, wo_ref, gf_ref, wr_ref,
                      x1_out, u_out, route_out, cnt_out, carry_ref):
    tm = x_ref.shape[0]

    @pl.when(pl.program_id(0) == 0)
    def _():
        carry_ref[...] = jnp.zeros(carry_ref.shape, F32)

    x1 = _merge_core(x_ref, hm_ref, hx_ref, gate_ref, (o0, o1, o2), (l0, l1, l2),
                     wm_ref, wdd_ref, wx_ref, wo_ref)
    x1_out[...] = x1
    uf = _rms(x1, gf_ref[...])
    u_out[...] = uf

    uh, um, ul = _split3(uf)
    wh, wmid, wl = _split3(wr_ref[...])
    logits = (_dot(uh, wh) + (_dot(uh, wmid) + _dot(um, wh))
              + (_dot(uh, wl) + _dot(um, wmid) + _dot(ul, wh)))
    lane = lax.broadcasted_iota(jnp.int32, (tm, LANES), 1).astype(F32)
    valid = lane < N_EXPERTS
    lg = jnp.where(valid, logits, NEG)
    ex = jnp.exp(lg - jnp.max(lg, axis=-1, keepdims=True))
    probs = jnp.where(valid, ex / jnp.sum(ex, axis=-1, keepdims=True), -1.0)
    p1 = jnp.max(probs, axis=-1, keepdims=True)
    i1 = jnp.min(jnp.where(probs == p1, lane, float(LANES)), axis=-1, keepdims=True)
    rest = jnp.where(lane == i1, -1.0, probs)
    p2 = jnp.max(rest, axis=-1, keepdims=True)
    i2 = jnp.min(jnp.where(rest == p2, lane, float(LANES)), axis=-1, keepdims=True)
    g1 = p1 / (p1 + p2)
    g2 = p2 / (p1 + p2)
    sel = jnp.where(lane == i1, 1.0, jnp.where(lane == i2, 1.0, 0.0))
    row = lax.broadcasted_iota(jnp.int32, (tm, tm), 0)
    col = lax.broadcasted_iota(jnp.int32, (tm, tm), 1)
    before = jnp.where(col < row, 1.0, 0.0).astype(BF16)
    ranks = _dot(before, sel.astype(BF16)) + carry_ref[0:1, :]
    r1 = jnp.sum(jnp.where(lane == i1, ranks, 0.0), axis=-1, keepdims=True)
    r2 = jnp.sum(jnp.where(lane == i2, ranks, 0.0), axis=-1, keepdims=True)
    carry_ref[...] = carry_ref[...] + jnp.sum(sel, axis=0, keepdims=True)
    cnt_out[...] = carry_ref[...]
    route = jnp.where(lane == 0, i1, jnp.where(lane == 1, i2, jnp.where(lane == 2, g1,
            jnp.where(lane == 3, g2, jnp.where(lane == 4, r1, jnp.where(lane == 5, r2, 0.0))))))
    route_out[...] = route


def _merge(x, hm, hx, gates, ods, lses, wm, wdd, wx, wo, gf, dense_w=None, w_router=None):
    s = x.shape[0]
    tm = TOK_TILE
    row = lambda w: pl.BlockSpec((tm, w), lambda i: (i, 0))
    acts = (x, hm, hx, gates) + tuple(ods) + tuple(lses)
    slab = pl.BlockSpec((2, tm, LANES), lambda i: (0, i, 0))
    act_specs = [row(D_MODEL), row(MLSTM_W), row(MEM_W), row(3 * D_MODEL)] + [slab] * 6
    if dense_w is not None:
        consts = (wm, wdd, wx, wo, gf) + tuple(dense_w)
        return pl.pallas_call(
            _merge_dense_kernel,
            grid=(s // tm,),
            in_specs=act_specs + [_const_spec(c.shape) for c in consts],
            out_specs=row(D_MODEL),
            out_shape=jax.ShapeDtypeStruct((s, D_MODEL), F32),
            scratch_shapes=[pltpu.VMEM((tm, D_MODEL), F32)],
            compiler_params=_params("arbitrary"),
        )(*acts, *consts)
    consts = (wm, wdd, wx, wo, gf, w_router)
    return pl.pallas_call(
        _merge_moe_kernel,
        grid=(s // tm,),
        in_specs=act_specs + [_const_spec(c.shape) for c in consts],
        out_specs=(row(D_MODEL), row(D_MODEL), row(LANES), pl.BlockSpec((8, LANES), lambda i: (0, 0))),
        out_shape=(jax.ShapeDtypeStruct((s, D_MODEL), F32),
                   jax.ShapeDtypeStruct((s, D_MODEL), F32),
                   jax.ShapeDtypeStruct((s, LANES), F32),
                   jax.ShapeDtypeStruct((8, LANES), F32)),
        scratch_shapes=[pltpu.VMEM((8, LANES), F32)],
        compiler_params=_params("arbitrary"),
    )(*acts, *consts)


def _for_rows(n, fn):
    def body(k, c):
        for j in range(SUBLANES):
            fn(k, j)
        return c
    lax.fori_loop(0, n // SUBLANES, body, 0)


def _dispatch_kernel(zb_ref, p1_ref, p2_ref, u_ref, xs_hbm, zero_ref, sem, zsem):
    i = pl.program_id(0)
    td = p1_ref.shape[1]
    bm = zero_ref.shape[0]

    @pl.when(i == 0)
    def _():
        zero_ref[...] = jnp.zeros(zero_ref.shape, F32)

        def fill(op):
            def body(b, c):
                @pl.when(zb_ref[b] != 0)
                def _():
                    op(pltpu.make_async_copy(zero_ref, xs_hbm.at[pl.ds(b * bm, bm), :], zsem))
                return c
            lax.fori_loop(0, zb_ref.shape[0], body, 0)

        fill(lambda c: c.start())
        fill(lambda c: c.wait())

    def copies(k, j):
        r = k * SUBLANES + j
        src = u_ref.at[k, pl.ds(j, 1), :]
        return (pltpu.make_async_copy(src, xs_hbm.at[pl.ds(p1_ref[0, r], 1), :], sem),
                pltpu.make_async_copy(src, xs_hbm.at[pl.ds(p2_ref[0, r], 1), :], sem))

    def start(k, j):
        a, b = copies(k, j)
        a.start()
        b.start()

    def wait(k, j):
        a, b = copies(k, j)
        a.wait()
        b.wait()

    _for_rows(td, start)
    _for_rows(td, wait)


def _dispatch(u, p1, p2, zb, n_rows):
    s = u.shape[0]
    td = MOE_TD
    tok = lambda: pl.BlockSpec((None, 1, td), lambda i, zb: (i, 0, 0), memory_space=pltpu.SMEM)
    grid_spec = pltpu.PrefetchScalarGridSpec(
        num_scalar_prefetch=1,
        grid=(s // td,),
        in_specs=[tok(), tok(),
                  pl.BlockSpec((td // SUBLANES, SUBLANES, D_MODEL), lambda i, zb: (i, 0, 0))],
        out_specs=pl.BlockSpec(memory_space=pl.ANY),
        scratch_shapes=[pltpu.VMEM((MOE_BM, D_MODEL), F32),
                        pltpu.SemaphoreType.DMA(()), pltpu.SemaphoreType.DMA(())],
    )
    return pl.pallas_call(
        _dispatch_kernel,
        grid_spec=grid_spec,
        out_shape=jax.ShapeDtypeStruct((n_rows, D_MODEL), F32),
        compiler_params=_params("arbitrary"),
    )(zb, p1.reshape(s // td, 1, td), p2.reshape(s // td, 1, td), u.reshape(-1, SUBLANES, D_MODEL))


def _expert_kernel(be_ref, bv_ref, bx_ref, x_ref, wg_ref, wu_ref, wd_ref, y_ref, xb_ref, acc_ref):
    b = pl.program_id(0)
    f = pl.program_id(1)
    nf = pl.num_programs(1)
    valid = bv_ref[b] != 0

    @pl.when(jnp.logical_and(valid, f == 0))
    def _():
        xb_ref[...] = x_ref[...].astype(BF16)

    @pl.when(valid)
    def _():
        x = xb_ref[...]
        g = _dot(x, wg_ref[...])
        hcol = (g * jax.nn.sigmoid(g) * _dot(x, wu_ref[...])).astype(BF16)
        part = _dot(hcol, wd_ref[...])

        @pl.when(f == 0)
        def _():
            acc_ref[...] = part

        @pl.when(jnp.logical_and(f != 0, f != nf - 1))
        def _():
            acc_ref[...] += part

        @pl.when(f == nf - 1)
        def _():
            y_ref[...] = acc_ref[...] + part

    @pl.when(jnp.logical_and(jnp.logical_not(valid), f == nf - 1))
    def _():
        y_ref[...] = jnp.zeros(y_ref.shape, F32)


def _experts(xs, wg, wu, wd, be, bv, bx):
    n_rows = xs.shape[0]
    bm, tf = MOE_BM, MOE_TF
    nf = D_FF_EXPERT // tf
    assert nf >= 2

    def fidx(b, f, bv):
        return jnp.where(bv[b] != 0, f, nf - 1)

    grid_spec = pltpu.PrefetchScalarGridSpec(
        num_scalar_prefetch=3,
        grid=(n_rows // bm, nf),
        in_specs=[pl.BlockSpec((bm, D_MODEL), lambda b, f, be, bv, bx: (bx[b], 0)),
                  pl.BlockSpec((None, D_MODEL, tf), lambda b, f, be, bv, bx: (be[b], 0, fidx(b, f, bv))),
                  pl.BlockSpec((None, D_MODEL, tf), lambda b, f, be, bv, bx: (be[b], 0, fidx(b, f, bv))),
                  pl.BlockSpec((None, tf, D_MODEL), lambda b, f, be, bv, bx: (be[b], fidx(b, f, bv), 0))],
        out_specs=pl.BlockSpec((bm, D_MODEL), lambda b, f, be, bv, bx: (b, 0)),
        scratch_shapes=[pltpu.VMEM((bm, D_MODEL), BF16),
                        pltpu.VMEM((bm, D_MODEL), F32)],
    )
    return pl.pallas_call(
        _expert_kernel,
        grid_spec=grid_spec,
        out_shape=jax.ShapeDtypeStruct((n_rows, D_MODEL), F32),
        compiler_params=_params("arbitrary", "arbitrary"),
    )(be, bv, bx, xs, wg, wu, wd)


def _combine_kernel(p1_ref, p2_ref, p1n_ref, p2n_ref, x_ref, route_ref, gn_ref, ys_hbm, out_ref,
                    yg_ref, sem, *, final_norm):
    i = pl.program_id(0)
    tc = x_ref.shape[0]
    slot = lax.rem(i, 2)

    def gather(pa_ref, pb_ref, s_, op):
        def rows(k, j):
            r = k * SUBLANES + j
            op(pltpu.make_async_copy(ys_hbm.at[pl.ds(pa_ref[0, r], 1), :],
                                     yg_ref.at[s_, 0, k, pl.ds(j, 1), :], sem.at[s_]))
            op(pltpu.make_async_copy(ys_hbm.at[pl.ds(pb_ref[0, r], 1), :],
                                     yg_ref.at[s_, 1, k, pl.ds(j, 1), :], sem.at[s_]))
        _for_rows(tc, rows)

    @pl.when(i == 0)
    def _():
        gather(p1_ref, p2_ref, 0, lambda c: c.start())

    gather(p1_ref, p2_ref, slot, lambda c: c.wait())

    @pl.when(i + 1 < pl.num_programs(0))
    def _():
        gather(p1n_ref, p2n_ref, 1 - slot, lambda c: c.start())

    y1 = yg_ref[slot, 0].reshape(tc, D_MODEL)
    y2 = yg_ref[slot, 1].reshape(tc, D_MODEL)
    out = x_ref[...] + route_ref[:, 2:3] * y1 + route_ref[:, 3:4] * y2
    out_ref[...] = _rms(out, gn_ref[...]) if final_norm else out


def _combine(x1, ys, p1, p2, route, gn, final_norm):
    s = x1.shape[0]
    tc = MOE_TC
    nt = s // tc
    row = lambda w: pl.BlockSpec((tc, w), lambda i: (i, 0))
    cur = lambda: pl.BlockSpec((None, 1, tc), lambda i: (i, 0, 0), memory_space=pltpu.SMEM)
    nxt = lambda: pl.BlockSpec((None, 1, tc), lambda i: (jnp.minimum(i + 1, nt - 1), 0, 0),
                               memory_space=pltpu.SMEM)
    p1 = p1.reshape(nt, 1, tc)
    p2 = p2.reshape(nt, 1, tc)
    return pl.pallas_call(
        functools.partial(_combine_kernel, final_norm=final_norm),
        grid=(nt,),
        in_specs=[cur(), cur(), nxt(), nxt(), row(D_MODEL), row(LANES), _const_spec(gn.shape),
                  pl.BlockSpec(memory_space=pl.ANY)],
        out_specs=row(D_MODEL),
        out_shape=jax.ShapeDtypeStruct((s, D_MODEL), F32),
        scratch_shapes=[pltpu.VMEM((2, 2, tc // SUBLANES, SUBLANES, D_MODEL), F32),
                        pltpu.SemaphoreType.DMA((2,))],
        compiler_params=_params("arbitrary"),
    )(p1, p2, p1, p2, x1, route, gn, ys)


def _moe_plan(route, counts, s):
    bm = MOE_BM
    nb = (2 * s) // bm + N_EXPERTS
    i1 = route[:, 0].astype(jnp.int32)
    i2 = route[:, 1].astype(jnp.int32)
    r1 = route[:, 4].astype(jnp.int32)
    r2 = route[:, 5].astype(jnp.int32)
    cnt = counts[0, :N_EXPERTS].astype(jnp.int32)
    padded = ((cnt + bm - 1) // bm) * bm
    ends = jnp.cumsum(padded)
    off = ends - padded
    p1 = off[i1] + r1
    p2 = off[i2] + r2
    nb_used = ends[-1] // bm

    bidx = jnp.arange(nb, dtype=jnp.int32)
    bvalid = (bidx < nb_used).astype(jnp.int32)
    bsrc = jnp.minimum(bidx, jnp.maximum(nb_used - 1, 0))
    bexp = jnp.sum((bsrc[:, None] * bm >= ends[None, :]).astype(jnp.int32), axis=1)
    bexp = jnp.minimum(bexp, N_EXPERTS - 1)
    real = jnp.clip(off[bexp] + cnt[bexp] - bidx * bm, 0, bm)
    bzero = ((real < bm) | (bvalid == 0)).astype(jnp.int32)
    return p1, p2, bexp, bvalid, bsrc, bzero


def _dil_weights(w_in, q_scale):
    cols = []
    for g in range(len(DIL_PATTERNS)):
        sl = lambda off: w_in[:, off + g * DIL_GW: off + (g + 1) * DIL_GW]
        cols += [sl(OFF_KD), sl(OFF_VD), sl(OFF_QD) * q_scale]
    return jnp.concatenate(cols, axis=-1).astype(BF16)


def kernel(x, mem, norm_mix, w_in, conv_qk, b_gate_if, mlstm_norm, norm_mem, w_mem_kv, w_br_m, w_br_d,
           w_br_x, w_out, norm_ffn, ffn_w_gate, ffn_w_up, ffn_w_down, moe_router, moe_w_gate, moe_w_up,
           moe_w_down, norm_final):
    s = x.shape[1]
    xs = x.reshape(s, D_MODEL)
    mem2 = mem.reshape(N_MEM, D_MODEL)
    row = lambda a: a.reshape(1, -1)
    q_scale = DIL_DH ** -0.5

    for layer in range(DEPTH):
        wl = w_in[layer]
        g_mix = row(norm_mix[layer])
        km, vm = _memkv(mem2, row(norm_mem[layer]), w_mem_kv[layer].astype(BF16))

        w_if = wl[:, OFF_IF:OFF_QD]
        wif = jnp.pad(w_if, ((0, 0), (0, LANES - 2 * MLSTM_HEADS))).astype(BF16)
        wift = w_if.T.astype(BF16)
        bif = jnp.pad(b_gate_if[layer], (0, LANES - 2 * MLSTM_HEADS)).reshape(1, LANES)
        bift = b_gate_if[layer].reshape(2 * MLSTM_HEADS, 1)
        (q_m, k_m, v_m, o_m, ifc, ifr, d0, d1, d2, h_x, gates) = _inproj(
            xs, g_mix, wl[:, :OFF_IF].astype(BF16), wif, wift, bif, bift, conv_qk[layer],
            _dil_weights(wl, q_scale), wl[:, OFF_QX:OFF_GATE].astype(BF16),
            wl[:, OFF_GATE:].astype(BF16), km, vm)

        h_m = _mlstm(q_m, k_m, v_m, o_m, ifc, ifr, row(mlstm_norm[layer]))

        ods, lses = [], []
        for qkv, (_, dil) in zip((d0, d1, d2), DIL_PATTERNS):
            o_g, lse_g = _dil_attn(qkv, dil)
            ods.append(o_g)
            lses.append(lse_g)

        merge_w = (w_br_m[layer].astype(BF16), w_br_d[layer].astype(BF16), w_br_x[layer].astype(BF16),
                   w_out[layer].astype(BF16), row(norm_ffn[layer]))
        if layer % 2 == 0:
            li = layer // 2
            dense_w = (ffn_w_gate[li].astype(BF16), ffn_w_up[li].astype(BF16), ffn_w_down[li].astype(BF16))
            xs = _merge(xs, h_m, h_x, gates, ods, lses, *merge_w, dense_w=dense_w)
        else:
            li = layer // 2
            wr = jnp.pad(moe_router[li], ((0, 0), (0, LANES - N_EXPERTS)))
            x1, u, route, counts = _merge(xs, h_m, h_x, gates, ods, lses, *merge_w, w_router=wr)
            p1, p2, bexp, bvalid, bsrc, bzero = _moe_plan(route, counts, s)
            rows = _dispatch(u, p1, p2, bzero, bexp.shape[0] * MOE_BM)
            y = _experts(rows, moe_w_gate[li].astype(BF16), moe_w_up[li].astype(BF16),
                         moe_w_down[li].astype(BF16), bexp, bvalid, bsrc)
            xs = _combine(x1, y, p1, p2, route, row(norm_final), final_norm=layer == DEPTH - 1)
    return xs.reshape(x.shape)
```

```python
import functools

import jax
import jax.numpy as jnp
from jax import lax
from jax.experimental import pallas as pl
from jax.experimental.pallas import tpu as pltpu

F32 = jnp.float32
BF16 = jnp.bfloat16

EPS = 1e-6
D_MODEL = 1024
DEPTH = 4
N_MEM = 256
MLSTM_HEADS = 4
MLSTM_DH = 128
MLSTM_W = MLSTM_HEADS * MLSTM_DH
MLSTM_CHUNK = 128
CONV_W = 4
M_INIT = -1e30
DIL_PATTERNS = ((128, 1), (512, 4), (2048, 16))
DIL_HEADS = 4
DIL_DH = 64
DIL_GW = DIL_HEADS * DIL_DH
DIL_W = 3 * DIL_GW
Q_BLOCK = 128
MEM_HEADS = 4
MEM_DH = 128
MEM_W = MEM_HEADS * MEM_DH
D_FF = 2816
N_EXPERTS = 8
D_FF_EXPERT = 3584

OFF_IF = 4 * MLSTM_W
OFF_QD = OFF_IF + 2 * MLSTM_HEADS
OFF_KD = OFF_QD + DIL_W
OFF_VD = OFF_KD + DIL_W
OFF_QX = OFF_VD + DIL_W
OFF_GATE = OFF_QX + MEM_W
IN_COLS = OFF_GATE + 3 * D_MODEL

LANES = 128
SUBLANES = 8
NEG = -1e30
VMEM_LIMIT = 56 * 1024 * 1024

DIL_SLABS = 3 * DIL_GW // LANES
MLSTM_CHUNKS_PER_STEP = 1
ATT_UNROLL = 4
ATT_SUPER = 2048
TOK_TILE = 512
FF_CHUNK = 256
MOE_BM = 512
MOE_TD = 512
MOE_TC = 256
MOE_TF = 1792

NT_DIMS = (((1,), (1,)), ((), ()))
TN_DIMS = (((0,), (0,)), ((), ()))


def _params(*sem):
    return pltpu.CompilerParams(dimension_semantics=sem, vmem_limit_bytes=VMEM_LIMIT)


def _dot(a, b):
    return jnp.dot(a, b, preferred_element_type=F32)


def _dot_nt(a, b):
    return lax.dot_general(a, b, NT_DIMS, preferred_element_type=F32)


def _rms(x, g):
    return x * lax.rsqrt(jnp.mean(x * x, axis=-1, keepdims=True) + EPS) * g


def _split3(x):
    hi = x.astype(BF16)
    r1 = x - hi.astype(F32)
    mid = r1.astype(BF16)
    lo = (r1 - mid.astype(F32)).astype(BF16)
    return hi, mid, lo


def _to_tile_rows(ref, x):
    n = x.shape[0]
    for c in range(SUBLANES):
        ref[pl.ds(c, n, stride=SUBLANES), :] = x[:, c * LANES:(c + 1) * LANES]


def _from_tile_rows(ref, n):
    return jnp.concatenate([ref[pl.ds(c, n, stride=SUBLANES), :] for c in range(SUBLANES)], axis=-1)


def _const_spec(shape):
    nd = len(shape)
    return pl.BlockSpec(shape, lambda *_: (0,) * nd, pipeline_mode=pl.Buffered(1))


def _memkv_kernel(mem_ref, g_ref, w_ref, k_ref, v_ref):
    u = _rms(mem_ref[...], g_ref[...]).astype(BF16)
    kv = _dot(u, w_ref[...])
    k_ref[...] = kv[:, :MEM_W].astype(BF16)
    v_ref[...] = kv[:, MEM_W:].astype(BF16)


def _memkv(mem, g, w_kv):
    return pl.pallas_call(
        _memkv_kernel,
        out_shape=(jax.ShapeDtypeStruct((N_MEM, MEM_W), BF16),) * 2,
        compiler_params=pltpu.CompilerParams(vmem_limit_bytes=VMEM_LIMIT),
    )(mem, g, w_kv)


def _inproj_kernel(x_ref, g_ref, wa_ref, wif_ref, wift_ref, bif_ref, bift_ref, cw_ref,
                   wd_ref, wqx_ref, wg_ref, km_ref, vm_ref,
                   q_out, k_out, v_out, o_out, if_out, ift_out, d0_out, d1_out, d2_out,
                   hx_out, gate_out, conv_buf):
    tm = x_ref.shape[0]
    u = _rms(x_ref[...], g_ref[...]).astype(BF16)

    @pl.when(pl.program_id(0) == 0)
    def _():
        conv_buf[0:8, :] = jnp.zeros((8, 2 * MLSTM_W), F32)

    conv_buf[8:tm + 8, :] = _dot(u, wa_ref[:, 0:2 * MLSTM_W])
    acc = cw_ref[0:1, :] * conv_buf[pl.ds(8 - (CONV_W - 1), tm), :]
    for j in range(1, CONV_W):
        acc = acc + cw_ref[j:j + 1, :] * conv_buf[pl.ds(8 - (CONV_W - 1) + j, tm), :]
    conv_buf[0:8, :] = conv_buf[tm:tm + 8, :]
    qk = acc * jax.nn.sigmoid(acc)
    q_out[...] = qk[:, :MLSTM_W].astype(BF16)
    k_out[...] = (qk[:, MLSTM_W:] * (MLSTM_DH ** -0.5)).astype(BF16)

    v_out[...] = _dot(u, wa_ref[:, 2 * MLSTM_W:3 * MLSTM_W]).astype(BF16)
    o_out[...] = jax.nn.sigmoid(_dot(u, wa_ref[:, 3 * MLSTM_W:4 * MLSTM_W])).astype(BF16)

    if_out[...] = _dot(u, wif_ref[...]) + bif_ref[...]
    ift_out[...] = _dot_nt(wift_ref[...], u) + bift_ref[...]

    for gi, d_out in enumerate((d0_out, d1_out, d2_out)):
        d = _dot(u, wd_ref[:, gi * 3 * DIL_GW:(gi + 1) * 3 * DIL_GW])
        for j in range(DIL_SLABS):
            d_out[j] = d[:, j * LANES:(j + 1) * LANES]

    qx = (_dot(u, wqx_ref[...]) * (MEM_DH ** -0.5)).astype(BF16)
    outs = []
    for h in range(MEM_HEADS):
        sl = slice(h * MEM_DH, (h + 1) * MEM_DH)
        s = _dot_nt(qx[:, sl], km_ref[:, sl])
        p = jnp.exp(s - jnp.max(s, axis=-1, keepdims=True))
        den = jnp.sum(p, axis=-1, keepdims=True)
        outs.append(_dot(p.astype(BF16), vm_ref[:, sl]) / den)
    hx_out[...] = jnp.concatenate(outs, axis=-1).astype(BF16)

    gate_out[...] = jax.nn.sigmoid(_dot(u, wg_ref[...])).astype(BF16)


def _inproj(x, g, wa, wif, wift, bif, bift, cw, wd0, wqx, wgate, km, vm):
    s = x.shape[0]
    tm = TOK_TILE
    row = lambda w: pl.BlockSpec((tm, w), lambda i: (i, 0))
    out_shape = (
        jax.ShapeDtypeStruct((s, MLSTM_W), BF16),
        jax.ShapeDtypeStruct((s, MLSTM_W), BF16),
        jax.ShapeDtypeStruct((s, MLSTM_W), BF16),
        jax.ShapeDtypeStruct((s, MLSTM_W), BF16),
        jax.ShapeDtypeStruct((s, LANES), F32),
        jax.ShapeDtypeStruct((8, s), F32),
        jax.ShapeDtypeStruct((DIL_SLABS, s, LANES), F32),
        jax.ShapeDtypeStruct((DIL_SLABS, s, LANES), F32),
        jax.ShapeDtypeStruct((DIL_SLABS, s, LANES), F32),
        jax.ShapeDtypeStruct((s, MEM_W), BF16),
        jax.ShapeDtypeStruct((s, 3 * D_MODEL), BF16),
    )
    slab = pl.BlockSpec((DIL_SLABS, tm, LANES), lambda i: (0, i, 0))
    out_specs = (row(MLSTM_W), row(MLSTM_W), row(MLSTM_W), row(MLSTM_W), row(LANES),
                 pl.BlockSpec((8, tm), lambda i: (0, i)),
                 slab, slab, slab, row(MEM_W), row(3 * D_MODEL))
    in_specs = [row(D_MODEL)] + [_const_spec(a.shape) for a in
                                 (g, wa, wif, wift, bif, bift, cw, wd0, wqx, wgate, km, vm)]
    return pl.pallas_call(
        _inproj_kernel,
        grid=(s // tm,),
        in_specs=in_specs,
        out_specs=out_specs,
        out_shape=out_shape,
        scratch_shapes=[pltpu.VMEM((tm + 8, 2 * MLSTM_W), F32)],
        compiler_params=_params("arbitrary"),
    )(x, g, wa, wif, wift, bif, bift, cw, wd0, wqx, wgate, km, vm)


def _log_sigmoid(x):
    return jnp.minimum(x, 0.0) - jnp.log(1.0 + jnp.exp(-jnp.abs(x)))


def _mlstm_kernel(q_ref, k_ref, v_ref, o_ref, ifc_ref, ifr_ref, g_ref, out_ref, ct_ref, m_ref):
    L = MLSTM_CHUNK
    H = MLSTM_HEADS

    @pl.when(pl.program_id(0) == 0)
    def _():
        ct_ref[...] = jnp.zeros(ct_ref.shape, F32)
        m_ref[...] = jnp.full(m_ref.shape, M_INIT, F32)

    row = lax.broadcasted_iota(jnp.int32, (L, L), 0)
    col = lax.broadcasted_iota(jnp.int32, (L, L), 1)
    causal = col <= row
    tril = jnp.where(causal, 1.0, 0.0).astype(BF16)
    triu = jnp.where(row <= col, 1.0, 0.0).astype(BF16)
    ones_col = jnp.where(col == 0, 1.0, 0.0).astype(BF16)

    for ci in range(q_ref.shape[0] // L):
        rows = slice(ci * L, (ci + 1) * L)
        ifc = ifc_ref[rows, :]
        ifr = ifr_ref[:, rows]
        cum_c = sum(_dot(tril, p) for p in _split3(_log_sigmoid(ifc)))
        cum_r = sum(_dot(p, triu) for p in _split3(_log_sigmoid(ifr)))

        for h in range(H):
            sl = slice(h * MLSTM_DH, (h + 1) * MLSTM_DH)
            i_c = ifc[:, h:h + 1]
            i_r = ifr[h:h + 1, :]
            cc = cum_c[:, H + h:H + h + 1]
            cr = cum_r[H + h:H + h + 1, :]
            total = cr[:, L - 1:L]
            m_prev = m_ref[h:h + 1, 0:1]

            dm = jnp.where(causal, cc - cr + i_r, -jnp.inf)
            inter = cc + m_prev
            m_row = jnp.maximum(jnp.max(dm, axis=-1, keepdims=True), inter)
            w_intra = jnp.exp(dm - m_row)
            w_inter = jnp.exp(inter - m_row)

            qh = q_ref[rows, sl]
            kh = k_ref[rows, sl]
            vaug = jnp.concatenate([v_ref[rows, sl], ones_col], axis=-1)
            s_mat = _dot_nt(qh, kh) * w_intra
            tot = _dot(s_mat.astype(BF16), vaug) + w_inter * _dot(qh, ct_ref[h].astype(BF16))
            den = tot[:, MLSTM_DH:MLSTM_DH + 1]
            h_out = tot[:, :MLSTM_DH] / jnp.maximum(jnp.abs(den), jnp.exp(-m_row))

            g_end = total - cc + i_c
            m_new = jnp.maximum(total + m_prev, jnp.max(g_end, axis=0, keepdims=True))
            w_end = jnp.exp(g_end - m_new)
            decay = jnp.exp(total + m_prev - m_new)
            vw = (vaug.astype(F32) * w_end).astype(BF16)
            ct_ref[h] = decay * ct_ref[h] + lax.dot_general(kh, vw, TN_DIMS, preferred_element_type=F32)
            m_ref[h:h + 1, :] = jnp.broadcast_to(m_new, (1, LANES))

            mu = jnp.mean(h_out, axis=-1, keepdims=True)
            cen = h_out - mu
            var = jnp.mean(cen * cen, axis=-1, keepdims=True)
            y = cen * lax.rsqrt(var + EPS) * g_ref[:, sl] * o_ref[rows, sl].astype(F32)
            out_ref[rows, sl] = y.astype(BF16)


def _mlstm(q, k, v, o, ifc, ifr, g):
    s = q.shape[0]
    L = MLSTM_CHUNK * MLSTM_CHUNKS_PER_STEP
    row = pl.BlockSpec((L, MLSTM_W), lambda c: (c, 0))
    return pl.pallas_call(
        _mlstm_kernel,
        grid=(s // L,),
        in_specs=[row, row, row, row,
                  pl.BlockSpec((L, LANES), lambda c: (c, 0)),
                  pl.BlockSpec((8, L), lambda c: (0, c)),
                  _const_spec(g.shape)],
        out_specs=row,
        out_shape=jax.ShapeDtypeStruct((s, MLSTM_W), BF16),
        scratch_shapes=[pltpu.VMEM((MLSTM_HEADS, MLSTM_DH, 2 * MLSTM_DH), F32),
                        pltpu.VMEM((8, LANES), F32)],
        compiler_params=_params("arbitrary"),
    )(q, k, v, o, ifc, ifr, g)


def _dil_attn_kernel(kv_ref, q_ref, kvp_ref, o_ref, lse_ref, *, dil):
    B = Q_BLOCK
    span = B * dil
    n_sub = ATT_SUPER // span
    row = lax.broadcasted_iota(jnp.int32, (B, 2 * B), 0)
    col = lax.broadcasted_iota(jnp.int32, (B, 2 * B), 1)
    band = jnp.where(col >= row, jnp.where(col <= row + B, 0.0, NEG), NEG)
    first = jnp.where(pl.program_id(0) == 0, 1.0, 0.0)
    band_first = band + first * jnp.where(col < B, NEG, 0.0)
    lane = lax.broadcasted_iota(jnp.int32, (B, LANES), 1)
    lo = lane < DIL_DH
    hi = lane >= DIL_DH

    def rows(start):
        return pl.ds(start, B, stride=dil) if dil > 1 else pl.ds(start, B)

    def unit(cur_start, prev_ref, prev_start, bias):
        for half in range(2):
            q2 = q_ref[half, rows(cur_start), :]
            k2 = jnp.concatenate([prev_ref[half, rows(prev_start), :],
                                  kv_ref[half, rows(cur_start), :]], axis=0).astype(BF16)
            v2 = jnp.concatenate([prev_ref[2 + half, rows(prev_start), :],
                                  kv_ref[2 + half, rows(cur_start), :]], axis=0).astype(BF16)
            res = []
            for keep in (lo, hi):
                qm = jnp.where(keep, q2, 0.0).astype(BF16)
                s = _dot_nt(qm, k2) + bias
                mx = jnp.max(s, axis=-1, keepdims=True)
                p = jnp.exp(s - mx)
                den = jnp.sum(p, axis=-1, keepdims=True)
                res.append((_dot(p.astype(BF16), v2) / den, mx + jnp.log(den)))
            o_ref[half, rows(cur_start), :] = jnp.where(lo, res[0][0], res[1][0])
            lse_ref[half, rows(cur_start), :] = jnp.where(lo, res[0][1], res[1][1])

    def per_residue(r, carry):
        unit(r, kvp_ref, r, band_first)

        def per_sub(j, c):
            unit(j * span + r, kv_ref, (j - 1) * span + r, band)
            return c

        if 1 < n_sub <= ATT_UNROLL:
            for j in range(1, n_sub):
                per_sub(j, 0)
        elif n_sub > 1:
            lax.fori_loop(1, n_sub, per_sub, 0, unroll=ATT_UNROLL)
        return carry

    if dil >= ATT_UNROLL:
        lax.fori_loop(0, dil, per_residue, 0, unroll=max(1, ATT_UNROLL // n_sub))
    else:
        for r in range(dil):
            per_residue(r, 0)


def _dil_attn(qkv, dil):
    s = qkv.shape[1]
    span = Q_BLOCK * dil
    n_prev = ATT_SUPER // span
    blk = lambda n: pl.BlockSpec((n, ATT_SUPER, LANES), lambda i: (0, i, 0))
    return pl.pallas_call(
        functools.partial(_dil_attn_kernel, dil=dil),
        grid=(s // ATT_SUPER,),
        in_specs=[blk(4),
                  pl.BlockSpec((2, ATT_SUPER, LANES), lambda i: (2, i, 0)),
                  pl.BlockSpec((4, span, LANES), lambda i: (0, jnp.maximum(i * n_prev - 1, 0), 0))],
        out_specs=(blk(2), blk(2)),
        out_shape=(jax.ShapeDtypeStruct((2, s, LANES), F32),) * 2,
        compiler_params=_params("arbitrary"),
    )(qkv, qkv, qkv)


def _merge_core(x_ref, hm_ref, hx_ref, gate_ref, od_refs, lse_refs, wm_ref, wdd_ref, wx_ref, wo_ref):
    wide = lambda r: jnp.concatenate([r[0], r[1]], axis=-1)
    lses = [wide(r) for r in lse_refs]
    mx = jnp.maximum(jnp.maximum(lses[0], lses[1]), lses[2])
    es = [jnp.exp(l - mx) for l in lses]
    den = es[0] + es[1] + es[2]
    hd = (es[0] * wide(od_refs[0]) + es[1] * wide(od_refs[1]) + es[2] * wide(od_refs[2])) / den
    d = D_MODEL
    merged = (gate_ref[:, 0:d].astype(F32) * _dot(hm_ref[...], wm_ref[...])
              + gate_ref[:, d:2 * d].astype(F32) * _dot(hd.astype(BF16), wdd_ref[...])
              + gate_ref[:, 2 * d:3 * d].astype(F32) * _dot(hx_ref[...], wx_ref[...]))
    return x_ref[...] + _dot(merged.astype(BF16), wo_ref[...])


def _merge_dense_kernel(x_ref, hm_ref, hx_ref, gate_ref, o0, o1, o2, l0, l1, l2,
                        wm_ref, wdd_ref, wx_ref, wo_ref, gf_ref, wg_ref, wu_ref, wdn_ref,
                        out_ref, acc_ref):
    x1 = _merge_core(x_ref, hm_ref, hx_ref, gate_ref, (o0, o1, o2), (l0, l1, l2),
                     wm_ref, wdd_ref, wx_ref, wo_ref)
    u = _rms(x1, gf_ref[...]).astype(BF16)
    acc_ref[...] = x1

    def body(c, carry):
        cols = pl.ds(pl.multiple_of(c * FF_CHUNK, FF_CHUNK), FF_CHUNK)
        g = _dot(u, wg_ref[:, cols])
        hcol = (g * jax.nn.sigmoid(g) * _dot(u, wu_ref[:, cols])).astype(BF16)
        acc_ref[...] += _dot(hcol, wdn_ref[cols, :])
        return carry

    lax.fori_loop(0, D_FF // FF_CHUNK, body, 0)
    out_ref[...] = acc_ref[...]


def _merge_moe_kernel(x_ref, hm_ref, hx_ref, gate_ref, o0, o1, o2, l0, l1, l2,
                      wm_ref, wdd_ref, wx_ref, wo_ref, gf_ref, wr_ref,
                      x1_out, u_out, route_out, cnt_out, carry_ref):
    tm = x_ref.shape[0]

    @pl.when(pl.program_id(0) == 0)
    def _():
        carry_ref[...] = jnp.zeros(carry_ref.shape, F32)

    x1 = _merge_core(x_ref, hm_ref, hx_ref, gate_ref, (o0, o1, o2), (l0, l1, l2),
                     wm_ref, wdd_ref, wx_ref, wo_ref)
    x1_out[...] = x1
    uf = _rms(x1, gf_ref[...])
    _to_tile_rows(u_out, uf)

    uh, um, ul = _split3(uf)
    wh, wmid, wl = _split3(wr_ref[...])
    logits = (_dot(uh, wh) + (_dot(uh, wmid) + _dot(um, wh))
              + (_dot(uh, wl) + _dot(um, wmid) + _dot(ul, wh)))
    lane = lax.broadcasted_iota(jnp.int32, (tm, LANES), 1).astype(F32)
    valid = lane < N_EXPERTS
    lg = jnp.where(valid, logits, NEG)
    ex = jnp.exp(lg - jnp.max(lg, axis=-1, keepdims=True))
    probs = jnp.where(valid, ex / jnp.sum(ex, axis=-1, keepdims=True), -1.0)
    p1 = jnp.max(probs, axis=-1, keepdims=True)
    i1 = jnp.min(jnp.where(probs == p1, lane, float(LANES)), axis=-1, keepdims=True)
    rest = jnp.where(lane == i1, -1.0, probs)
    p2 = jnp.max(rest, axis=-1, keepdims=True)
    i2 = jnp.min(jnp.where(rest == p2, lane, float(LANES)), axis=-1, keepdims=True)
    g1 = p1 / (p1 + p2)
    g2 = p2 / (p1 + p2)
    sel = jnp.where(lane == i1, 1.0, jnp.where(lane == i2, 1.0, 0.0))
    row = lax.broadcasted_iota(jnp.int32, (tm, tm), 0)
    col = lax.broadcasted_iota(jnp.int32, (tm, tm), 1)
    before = jnp.where(col < row, 1.0, 0.0).astype(BF16)
    ranks = _dot(before, sel.astype(BF16)) + carry_ref[0:1, :]
    r1 = jnp.sum(jnp.where(lane == i1, ranks, 0.0), axis=-1, keepdims=True)
    r2 = jnp.sum(jnp.where(lane == i2, ranks, 0.0), axis=-1, keepdims=True)
    carry_ref[...] = carry_ref[...] + jnp.sum(sel, axis=0, keepdims=True)
    cnt_out[...] = carry_ref[...]
    route = jnp.where(lane == 0, i1, jnp.where(lane == 1, i2, jnp.where(lane == 2, g1,
            jnp.where(lane == 3, g2, jnp.where(lane == 4, r1, jnp.where(lane == 5, r2, 0.0))))))
    route_out[...] = route


def _merge(x, hm, hx, gates, ods, lses, wm, wdd, wx, wo, gf, dense_w=None, w_router=None):
    s = x.shape[0]
    tm = TOK_TILE
    row = lambda w: pl.BlockSpec((tm, w), lambda i: (i, 0))
    acts = (x, hm, hx, gates) + tuple(ods) + tuple(lses)
    slab = pl.BlockSpec((2, tm, LANES), lambda i: (0, i, 0))
    act_specs = [row(D_MODEL), row(MLSTM_W), row(MEM_W), row(3 * D_MODEL)] + [slab] * 6
    if dense_w is not None:
        consts = (wm, wdd, wx, wo, gf) + tuple(dense_w)
        return pl.pallas_call(
            _merge_dense_kernel,
            grid=(s // tm,),
            in_specs=act_specs + [_const_spec(c.shape) for c in consts],
            out_specs=row(D_MODEL),
            out_shape=jax.ShapeDtypeStruct((s, D_MODEL), F32),
            scratch_shapes=[pltpu.VMEM((tm, D_MODEL), F32)],
            compiler_params=_params("arbitrary"),
        )(*acts, *consts)
    consts = (wm, wdd, wx, wo, gf, w_router)
    return pl.pallas_call(
        _merge_moe_kernel,
        grid=(s // tm,),
        in_specs=act_specs + [_const_spec(c.shape) for c in consts],
        out_specs=(row(D_MODEL), pl.BlockSpec((tm * SUBLANES, LANES), lambda i: (i, 0)), row(LANES),
                   pl.BlockSpec((8, LANES), lambda i: (0, 0))),
        out_shape=(jax.ShapeDtypeStruct((s, D_MODEL), F32),
                   jax.ShapeDtypeStruct((s * SUBLANES, LANES), F32),
                   jax.ShapeDtypeStruct((s, LANES), F32),
                   jax.ShapeDtypeStruct((8, LANES), F32)),
        scratch_shapes=[pltpu.VMEM((8, LANES), F32)],
        compiler_params=_params("arbitrary"),
    )(*acts, *consts)


def _for_rows(n, fn):
    def body(k, c):
        for j in range(SUBLANES):
            fn(k, j)
        return c
    lax.fori_loop(0, n // SUBLANES, body, 0)


def _dispatch_kernel(zb_ref, p1_ref, p2_ref, u_ref, xs_hbm, zero_ref, sem, zsem):
    i = pl.program_id(0)
    td = p1_ref.shape[1]
    bmt = zero_ref.shape[0]

    @pl.when(i == 0)
    def _():
        zero_ref[...] = jnp.zeros(zero_ref.shape, F32)

        def fill(op):
            def body(b, c):
                @pl.when(zb_ref[b] != 0)
                def _():
                    op(pltpu.make_async_copy(zero_ref, xs_hbm.at[pl.ds(b * bmt, bmt), :], zsem))
                return c
            lax.fori_loop(0, zb_ref.shape[0], body, 0)

        fill(lambda c: c.start())
        fill(lambda c: c.wait())

    tile = lambda ref, t: ref.at[pl.ds(pl.multiple_of(t * SUBLANES, SUBLANES), SUBLANES), :]

    def copies(k, j):
        r = k * SUBLANES + j
        src = tile(u_ref, r)
        return (pltpu.make_async_copy(src, tile(xs_hbm, p1_ref[0, r]), sem),
                pltpu.make_async_copy(src, tile(xs_hbm, p2_ref[0, r]), sem))

    def start(k, j):
        a, b = copies(k, j)
        a.start()
        b.start()

    def wait(k, j):
        a, b = copies(k, j)
        a.wait()
        b.wait()

    _for_rows(td, start)
    _for_rows(td, wait)


def _dispatch(u, p1, p2, zb, n_rows):
    s = p1.shape[0]
    td = MOE_TD
    tok = lambda: pl.BlockSpec((None, 1, td), lambda i, zb: (i, 0, 0), memory_space=pltpu.SMEM)
    grid_spec = pltpu.PrefetchScalarGridSpec(
        num_scalar_prefetch=1,
        grid=(s // td,),
        in_specs=[tok(), tok(), pl.BlockSpec((td * SUBLANES, LANES), lambda i, zb: (i, 0))],
        out_specs=pl.BlockSpec(memory_space=pl.ANY),
        scratch_shapes=[pltpu.VMEM((MOE_BM * SUBLANES, LANES), F32),
                        pltpu.SemaphoreType.DMA(()), pltpu.SemaphoreType.DMA(())],
    )
    return pl.pallas_call(
        _dispatch_kernel,
        grid_spec=grid_spec,
        out_shape=jax.ShapeDtypeStruct((n_rows * SUBLANES, LANES), F32),
        compiler_params=_params("arbitrary"),
    )(zb, p1.reshape(s // td, 1, td), p2.reshape(s // td, 1, td), u)


def _expert_kernel(be_ref, bv_ref, bx_ref, x_ref, wg_ref, wu_ref, wd_ref, y_ref, xb_ref, acc_ref):
    b = pl.program_id(0)
    f = pl.program_id(1)
    nf = pl.num_programs(1)
    valid = bv_ref[b] != 0

    @pl.when(jnp.logical_and(valid, f == 0))
    def _():
        xb_ref[...] = _from_tile_rows(x_ref, xb_ref.shape[0]).astype(BF16)

    @pl.when(valid)
    def _():
        x = xb_ref[...]
        g = _dot(x, wg_ref[...])
        hcol = (g * jax.nn.sigmoid(g) * _dot(x, wu_ref[...])).astype(BF16)
        part = _dot(hcol, wd_ref[...])

        @pl.when(f == 0)
        def _():
            acc_ref[...] = part

        @pl.when(jnp.logical_and(f != 0, f != nf - 1))
        def _():
            acc_ref[...] += part

        @pl.when(f == nf - 1)
        def _():
            _to_tile_rows(y_ref, acc_ref[...] + part)

    @pl.when(jnp.logical_and(jnp.logical_not(valid), f == nf - 1))
    def _():
        y_ref[...] = jnp.zeros(y_ref.shape, F32)


def _experts(xs, wg, wu, wd, be, bv, bx):
    n_rows = xs.shape[0] // SUBLANES
    bm, tf = MOE_BM, MOE_TF
    nf = D_FF_EXPERT // tf
    assert nf >= 2

    def fidx(b, f, bv):
        return jnp.where(bv[b] != 0, f, nf - 1)

    grid_spec = pltpu.PrefetchScalarGridSpec(
        num_scalar_prefetch=3,
        grid=(n_rows // bm, nf),
        in_specs=[pl.BlockSpec((bm * SUBLANES, LANES), lambda b, f, be, bv, bx: (bx[b], 0)),
                  pl.BlockSpec((None, D_MODEL, tf), lambda b, f, be, bv, bx: (be[b], 0, fidx(b, f, bv))),
                  pl.BlockSpec((None, D_MODEL, tf), lambda b, f, be, bv, bx: (be[b], 0, fidx(b, f, bv))),
                  pl.BlockSpec((None, tf, D_MODEL), lambda b, f, be, bv, bx: (be[b], fidx(b, f, bv), 0))],
        out_specs=pl.BlockSpec((bm * SUBLANES, LANES), lambda b, f, be, bv, bx: (b, 0)),
        scratch_shapes=[pltpu.VMEM((bm, D_MODEL), BF16),
                        pltpu.VMEM((bm, D_MODEL), F32)],
    )
    return pl.pallas_call(
        _expert_kernel,
        grid_spec=grid_spec,
        out_shape=jax.ShapeDtypeStruct((n_rows * SUBLANES, LANES), F32),
        compiler_params=_params("arbitrary", "arbitrary"),
    )(be, bv, bx, xs, wg, wu, wd)


def _combine_kernel(p1_ref, p2_ref, p1n_ref, p2n_ref, x_ref, route_ref, gn_ref, ys_hbm, out_ref,
                    yg_ref, sem, *, final_norm):
    i = pl.program_id(0)
    tc = x_ref.shape[0]
    slot = lax.rem(i, 2)

    tile = lambda ref, t: ref.at[pl.ds(pl.multiple_of(t * SUBLANES, SUBLANES), SUBLANES), :]

    def gather(pa_ref, pb_ref, s_, op):
        def rows(k, j):
            r = k * SUBLANES + j
            op(pltpu.make_async_copy(tile(ys_hbm, pa_ref[0, r]), tile(yg_ref.at[s_, 0], r), sem.at[s_]))
            op(pltpu.make_async_copy(tile(ys_hbm, pb_ref[0, r]), tile(yg_ref.at[s_, 1], r), sem.at[s_]))
        _for_rows(tc, rows)

    @pl.when(i == 0)
    def _():
        gather(p1_ref, p2_ref, 0, lambda c: c.start())

    gather(p1_ref, p2_ref, slot, lambda c: c.wait())

    @pl.when(i + 1 < pl.num_programs(0))
    def _():
        gather(p1n_ref, p2n_ref, 1 - slot, lambda c: c.start())

    y1 = _from_tile_rows(yg_ref.at[slot, 0], tc)
    y2 = _from_tile_rows(yg_ref.at[slot, 1], tc)
    out = x_ref[...] + route_ref[:, 2:3] * y1 + route_ref[:, 3:4] * y2
    out_ref[...] = _rms(out, gn_ref[...]) if final_norm else out


def _combine(x1, ys, p1, p2, route, gn, final_norm):
    s = x1.shape[0]
    tc = MOE_TC
    nt = s // tc
    row = lambda w: pl.BlockSpec((tc, w), lambda i: (i, 0))
    cur = lambda: pl.BlockSpec((None, 1, tc), lambda i: (i, 0, 0), memory_space=pltpu.SMEM)
    nxt = lambda: pl.BlockSpec((None, 1, tc), lambda i: (jnp.minimum(i + 1, nt - 1), 0, 0),
                               memory_space=pltpu.SMEM)
    p1 = p1.reshape(nt, 1, tc)
    p2 = p2.reshape(nt, 1, tc)
    return pl.pallas_call(
        functools.partial(_combine_kernel, final_norm=final_norm),
        grid=(nt,),
        in_specs=[cur(), cur(), nxt(), nxt(), row(D_MODEL), row(LANES), _const_spec(gn.shape),
                  pl.BlockSpec(memory_space=pl.ANY)],
        out_specs=row(D_MODEL),
        out_shape=jax.ShapeDtypeStruct((s, D_MODEL), F32),
        scratch_shapes=[pltpu.VMEM((2, 2, tc * SUBLANES, LANES), F32),
                        pltpu.SemaphoreType.DMA((2,))],
        compiler_params=_params("arbitrary"),
    )(p1, p2, p1, p2, x1, route, gn, ys)


def _moe_plan(route, counts, s):
    bm = MOE_BM
    nb = (2 * s) // bm + N_EXPERTS
    i1 = route[:, 0].astype(jnp.int32)
    i2 = route[:, 1].astype(jnp.int32)
    r1 = route[:, 4].astype(jnp.int32)
    r2 = route[:, 5].astype(jnp.int32)
    cnt = counts[0, :N_EXPERTS].astype(jnp.int32)
    padded = ((cnt + bm - 1) // bm) * bm
    ends = jnp.cumsum(padded)
    off = ends - padded
    p1 = off[i1] + r1
    p2 = off[i2] + r2
    nb_used = ends[-1] // bm

    bidx = jnp.arange(nb, dtype=jnp.int32)
    bvalid = (bidx < nb_used).astype(jnp.int32)
    bsrc = jnp.minimum(bidx, jnp.maximum(nb_used - 1, 0))
    bexp = jnp.sum((bsrc[:, None] * bm >= ends[None, :]).astype(jnp.int32), axis=1)
    bexp = jnp.minimum(bexp, N_EXPERTS - 1)
    real = jnp.clip(off[bexp] + cnt[bexp] - bidx * bm, 0, bm)
    bzero = ((real < bm) | (bvalid == 0)).astype(jnp.int32)
    return p1, p2, bexp, bvalid, bsrc, bzero


def _dil_weights(w_in, q_scale):
    cols = []
    for g in range(len(DIL_PATTERNS)):
        sl = lambda off: w_in[:, off + g * DIL_GW: off + (g + 1) * DIL_GW]
        cols += [sl(OFF_KD), sl(OFF_VD), sl(OFF_QD) * q_scale]
    return jnp.concatenate(cols, axis=-1).astype(BF16)


def kernel(x, mem, norm_mix, w_in, conv_qk, b_gate_if, mlstm_norm, norm_mem, w_mem_kv, w_br_m, w_br_d,
           w_br_x, w_out, norm_ffn, ffn_w_gate, ffn_w_up, ffn_w_down, moe_router, moe_w_gate, moe_w_up,
           moe_w_down, norm_final):
    s = x.shape[1]
    xs = x.reshape(s, D_MODEL)
    mem2 = mem.reshape(N_MEM, D_MODEL)
    row = lambda a: a.reshape(1, -1)
    q_scale = DIL_DH ** -0.5

    for layer in range(DEPTH):
        wl = w_in[layer]
        g_mix = row(norm_mix[layer])
        km, vm = _memkv(mem2, row(norm_mem[layer]), w_mem_kv[layer].astype(BF16))

        w_if = wl[:, OFF_IF:OFF_QD]
        wif = jnp.pad(w_if, ((0, 0), (0, LANES - 2 * MLSTM_HEADS))).astype(BF16)
        wift = w_if.T.astype(BF16)
        bif = jnp.pad(b_gate_if[layer], (0, LANES - 2 * MLSTM_HEADS)).reshape(1, LANES)
        bift = b_gate_if[layer].reshape(2 * MLSTM_HEADS, 1)
        (q_m, k_m, v_m, o_m, ifc, ifr, d0, d1, d2, h_x, gates) = _inproj(
            xs, g_mix, wl[:, :OFF_IF].astype(BF16), wif, wift, bif, bift, conv_qk[layer],
            _dil_weights(wl, q_scale), wl[:, OFF_QX:OFF_GATE].astype(BF16),
            wl[:, OFF_GATE:].astype(BF16), km, vm)

        h_m = _mlstm(q_m, k_m, v_m, o_m, ifc, ifr, row(mlstm_norm[layer]))

        ods, lses = [], []
        for qkv, (_, dil) in zip((d0, d1, d2), DIL_PATTERNS):
            o_g, lse_g = _dil_attn(qkv, dil)
            ods.append(o_g)
            lses.append(lse_g)

        merge_w = (w_br_m[layer].astype(BF16), w_br_d[layer].astype(BF16), w_br_x[layer].astype(BF16),
                   w_out[layer].astype(BF16), row(norm_ffn[layer]))
        if layer % 2 == 0:
            li = layer // 2
            dense_w = (ffn_w_gate[li].astype(BF16), ffn_w_up[li].astype(BF16), ffn_w_down[li].astype(BF16))
            xs = _merge(xs, h_m, h_x, gates, ods, lses, *merge_w, dense_w=dense_w)
        else:
            li = layer // 2
            wr = jnp.pad(moe_router[li], ((0, 0), (0, LANES - N_EXPERTS)))
            x1, u, route, counts = _merge(xs, h_m, h_x, gates, ods, lses, *merge_w, w_router=wr)
            p1, p2, bexp, bvalid, bsrc, bzero = _moe_plan(route, counts, s)
            rows = _dispatch(u, p1, p2, bzero, bexp.shape[0] * MOE_BM)
            y = _experts(rows, moe_w_gate[li].astype(BF16), moe_w_up[li].astype(BF16),
                         moe_w_down[li].astype(BF16), bexp, bvalid, bsrc)
            xs = _combine(x1, y, p1, p2, route, row(norm_final), final_norm=layer == DEPTH - 1)
    return xs.reshape(x.shape)
```

```python
import functools

import jax
import jax.numpy as jnp
from jax import lax
from jax.experimental import pallas as pl
from jax.experimental.pallas import tpu as pltpu

F32 = jnp.float32
BF16 = jnp.bfloat16

EPS = 1e-6
D_MODEL = 1024
DEPTH = 4
N_MEM = 256
MLSTM_HEADS = 4
MLSTM_DH = 128
MLSTM_W = MLSTM_HEADS * MLSTM_DH
MLSTM_CHUNK = 128
CONV_W = 4
M_INIT = -1e30
DIL_PATTERNS = ((128, 1), (512, 4), (2048, 16))
DIL_HEADS = 4
DIL_DH = 64
DIL_GW = DIL_HEADS * DIL_DH
DIL_W = 3 * DIL_GW
Q_BLOCK = 128
MEM_HEADS = 4
MEM_DH = 128
MEM_W = MEM_HEADS * MEM_DH
D_FF = 2816
N_EXPERTS = 8
D_FF_EXPERT = 3584

OFF_IF = 4 * MLSTM_W
OFF_QD = OFF_IF + 2 * MLSTM_HEADS
OFF_KD = OFF_QD + DIL_W
OFF_VD = OFF_KD + DIL_W
OFF_QX = OFF_VD + DIL_W
OFF_GATE = OFF_QX + MEM_W
IN_COLS = OFF_GATE + 3 * D_MODEL

LANES = 128
SUBLANES = 8
NEG = -1e30
VMEM_LIMIT = 56 * 1024 * 1024

DIL_SLABS = 3 * DIL_GW // LANES

WCOL_IF = 4 * MLSTM_W
WCOL_DIL = WCOL_IF + LANES
WCOL_QX = WCOL_DIL + 3 * DIL_W
WCOL_GATE = WCOL_QX + MEM_W
WCOL_END = WCOL_GATE + 3 * D_MODEL
MLSTM_CHUNKS_PER_STEP = 1
ATT_UNROLL = 4
ATT_SUPER = 2048
TOK_TILE = 512
FF_CHUNK = 256
MOE_BM = 512
MOE_TD = 512
MOE_TC = 256
MOE_TF = 1792

NT_DIMS = (((1,), (1,)), ((), ()))
TN_DIMS = (((0,), (0,)), ((), ()))


def _params(*sem):
    return pltpu.CompilerParams(dimension_semantics=sem, vmem_limit_bytes=VMEM_LIMIT)


def _dot(a, b):
    return jnp.dot(a, b, preferred_element_type=F32)


def _dot_nt(a, b):
    return lax.dot_general(a, b, NT_DIMS, preferred_element_type=F32)


def _rms(x, g):
    return x * lax.rsqrt(jnp.mean(x * x, axis=-1, keepdims=True) + EPS) * g


def _split3(x):
    hi = x.astype(BF16)
    r1 = x - hi.astype(F32)
    mid = r1.astype(BF16)
    lo = (r1 - mid.astype(F32)).astype(BF16)
    return hi, mid, lo


def _const_spec(shape):
    nd = len(shape)
    return pl.BlockSpec(shape, lambda *_: (0,) * nd, pipeline_mode=pl.Buffered(1))


def _memkv_kernel(mem_ref, g_ref, w_ref, k_ref, v_ref):
    u = _rms(mem_ref[...], g_ref[...]).astype(BF16)
    kv = _dot(u, w_ref[...])
    k_ref[...] = kv[:, :MEM_W].astype(BF16)
    v_ref[...] = kv[:, MEM_W:].astype(BF16)


def _memkv(mem, g, w_kv):
    return pl.pallas_call(
        _memkv_kernel,
        out_shape=(jax.ShapeDtypeStruct((N_MEM, MEM_W), BF16),) * 2,
        compiler_params=pltpu.CompilerParams(vmem_limit_bytes=VMEM_LIMIT),
    )(mem, g, w_kv)


def _inproj_kernel(x_ref, g_ref, w_ref, wift_ref, bif_ref, bift_ref, cw_ref, km_ref, vm_ref,
                   q_out, k_out, v_out, o_out, if_out, ift_out, d0_out, d1_out, d2_out,
                   hx_out, gate_out, conv_buf):
    tm = x_ref.shape[0]
    u = _rms(x_ref[...], g_ref[...]).astype(BF16)

    @pl.when(pl.program_id(0) == 0)
    def _():
        conv_buf[0:8, :] = jnp.zeros((8, 2 * MLSTM_W), F32)

    conv_buf[8:tm + 8, :] = _dot(u, w_ref[:, 0:2 * MLSTM_W])
    acc = cw_ref[0:1, :] * conv_buf[pl.ds(8 - (CONV_W - 1), tm), :]
    for j in range(1, CONV_W):
        acc = acc + cw_ref[j:j + 1, :] * conv_buf[pl.ds(8 - (CONV_W - 1) + j, tm), :]
    conv_buf[0:8, :] = conv_buf[tm:tm + 8, :]
    qk = acc * jax.nn.sigmoid(acc)
    q_out[...] = qk[:, :MLSTM_W].astype(BF16)
    k_out[...] = (qk[:, MLSTM_W:] * (MLSTM_DH ** -0.5)).astype(BF16)

    v_out[...] = _dot(u, w_ref[:, 2 * MLSTM_W:3 * MLSTM_W]).astype(BF16)
    o_out[...] = jax.nn.sigmoid(_dot(u, w_ref[:, 3 * MLSTM_W:4 * MLSTM_W])).astype(BF16)

    if_out[...] = _dot(u, w_ref[:, WCOL_IF:WCOL_DIL]) + bif_ref[...]
    ift_out[...] = _dot_nt(wift_ref[...], u) + bift_ref[...]

    for gi, d_out in enumerate((d0_out, d1_out, d2_out)):
        d = _dot(u, w_ref[:, WCOL_DIL + gi * 3 * DIL_GW:WCOL_DIL + (gi + 1) * 3 * DIL_GW])
        for j in range(DIL_SLABS):
            d_out[j] = d[:, j * LANES:(j + 1) * LANES]

    qx = (_dot(u, w_ref[:, WCOL_QX:WCOL_GATE]) * (MEM_DH ** -0.5)).astype(BF16)
    outs = []
    for h in range(MEM_HEADS):
        sl = slice(h * MEM_DH, (h + 1) * MEM_DH)
        s = _dot_nt(qx[:, sl], km_ref[:, sl])
        p = jnp.exp(s - jnp.max(s, axis=-1, keepdims=True))
        den = jnp.sum(p, axis=-1, keepdims=True)
        outs.append(_dot(p.astype(BF16), vm_ref[:, sl]) / den)
    hx_out[...] = jnp.concatenate(outs, axis=-1).astype(BF16)

    gate_out[...] = jax.nn.sigmoid(_dot(u, w_ref[:, WCOL_GATE:WCOL_END])).astype(BF16)


def _inproj(x, g, w, wift, bif, bift, cw, km, vm):
    s = x.shape[0]
    tm = TOK_TILE
    row = lambda w: pl.BlockSpec((tm, w), lambda i: (i, 0))
    out_shape = (
        jax.ShapeDtypeStruct((s, MLSTM_W), BF16),
        jax.ShapeDtypeStruct((s, MLSTM_W), BF16),
        jax.ShapeDtypeStruct((s, MLSTM_W), BF16),
        jax.ShapeDtypeStruct((s, MLSTM_W), BF16),
        jax.ShapeDtypeStruct((s, LANES), F32),
        jax.ShapeDtypeStruct((8, s), F32),
        jax.ShapeDtypeStruct((DIL_SLABS, s, LANES), F32),
        jax.ShapeDtypeStruct((DIL_SLABS, s, LANES), F32),
        jax.ShapeDtypeStruct((DIL_SLABS, s, LANES), F32),
        jax.ShapeDtypeStruct((s, MEM_W), BF16),
        jax.ShapeDtypeStruct((s, 3 * D_MODEL), BF16),
    )
    slab = pl.BlockSpec((DIL_SLABS, tm, LANES), lambda i: (0, i, 0))
    out_specs = (row(MLSTM_W), row(MLSTM_W), row(MLSTM_W), row(MLSTM_W), row(LANES),
                 pl.BlockSpec((8, tm), lambda i: (0, i)),
                 slab, slab, slab, row(MEM_W), row(3 * D_MODEL))
    in_specs = [row(D_MODEL)] + [_const_spec(a.shape) for a in (g, w, wift, bif, bift, cw, km, vm)]
    return pl.pallas_call(
        _inproj_kernel,
        grid=(s // tm,),
        in_specs=in_specs,
        out_specs=out_specs,
        out_shape=out_shape,
        scratch_shapes=[pltpu.VMEM((tm + 8, 2 * MLSTM_W), F32)],
        compiler_params=_params("arbitrary"),
    )(x, g, w, wift, bif, bift, cw, km, vm)


def _log_sigmoid(x):
    return jnp.minimum(x, 0.0) - jnp.log(1.0 + jnp.exp(-jnp.abs(x)))


def _mlstm_kernel(q_ref, k_ref, v_ref, o_ref, ifc_ref, ifr_ref, g_ref, out_ref, ct_ref, m_ref):
    L = MLSTM_CHUNK
    H = MLSTM_HEADS

    @pl.when(pl.program_id(0) == 0)
    def _():
        ct_ref[...] = jnp.zeros(ct_ref.shape, F32)
        m_ref[...] = jnp.full(m_ref.shape, M_INIT, F32)

    row = lax.broadcasted_iota(jnp.int32, (L, L), 0)
    col = lax.broadcasted_iota(jnp.int32, (L, L), 1)
    causal = col <= row
    tril = jnp.where(causal, 1.0, 0.0).astype(BF16)
    triu = jnp.where(row <= col, 1.0, 0.0).astype(BF16)
    ones_col = jnp.where(col == 0, 1.0, 0.0).astype(BF16)

    for ci in range(q_ref.shape[0] // L):
        rows = slice(ci * L, (ci + 1) * L)
        ifc = ifc_ref[rows, :]
        ifr = ifr_ref[:, rows]
        cum_c = sum(_dot(tril, p) for p in _split3(_log_sigmoid(ifc)))
        cum_r = sum(_dot(p, triu) for p in _split3(_log_sigmoid(ifr)))

        for h in range(H):
            sl = slice(h * MLSTM_DH, (h + 1) * MLSTM_DH)
            i_c = ifc[:, h:h + 1]
            i_r = ifr[h:h + 1, :]
            cc = cum_c[:, H + h:H + h + 1]
            cr = cum_r[H + h:H + h + 1, :]
            total = cr[:, L - 1:L]
            m_prev = m_ref[h:h + 1, 0:1]

            dm = jnp.where(causal, cc - cr + i_r, -jnp.inf)
            inter = cc + m_prev
            m_row = jnp.maximum(jnp.max(dm, axis=-1, keepdims=True), inter)
            w_intra = jnp.exp(dm - m_row)
            w_inter = jnp.exp(inter - m_row)

            qh = q_ref[rows, sl]
            kh = k_ref[rows, sl]
            vaug = jnp.concatenate([v_ref[rows, sl], ones_col], axis=-1)
            s_mat = _dot_nt(qh, kh) * w_intra
            tot = _dot(s_mat.astype(BF16), vaug) + w_inter * _dot(qh, ct_ref[h].astype(BF16))
            den = tot[:, MLSTM_DH:MLSTM_DH + 1]
            h_out = tot[:, :MLSTM_DH] / jnp.maximum(jnp.abs(den), jnp.exp(-m_row))

            g_end = total - cc + i_c
            m_new = jnp.maximum(total + m_prev, jnp.max(g_end, axis=0, keepdims=True))
            w_end = jnp.exp(g_end - m_new)
            decay = jnp.exp(total + m_prev - m_new)
            vw = (vaug.astype(F32) * w_end).astype(BF16)
            ct_ref[h] = decay * ct_ref[h] + lax.dot_general(kh, vw, TN_DIMS, preferred_element_type=F32)
            m_ref[h:h + 1, :] = jnp.broadcast_to(m_new, (1, LANES))

            mu = jnp.mean(h_out, axis=-1, keepdims=True)
            cen = h_out - mu
            var = jnp.mean(cen * cen, axis=-1, keepdims=True)
            y = cen * lax.rsqrt(var + EPS) * g_ref[:, sl] * o_ref[rows, sl].astype(F32)
            out_ref[rows, sl] = y.astype(BF16)


def _mlstm(q, k, v, o, ifc, ifr, g):
    s = q.shape[0]
    L = MLSTM_CHUNK * MLSTM_CHUNKS_PER_STEP
    row = pl.BlockSpec((L, MLSTM_W), lambda c: (c, 0))
    return pl.pallas_call(
        _mlstm_kernel,
        grid=(s // L,),
        in_specs=[row, row, row, row,
                  pl.BlockSpec((L, LANES), lambda c: (c, 0)),
                  pl.BlockSpec((8, L), lambda c: (0, c)),
                  _const_spec(g.shape)],
        out_specs=row,
        out_shape=jax.ShapeDtypeStruct((s, MLSTM_W), BF16),
        scratch_shapes=[pltpu.VMEM((MLSTM_HEADS, MLSTM_DH, 2 * MLSTM_DH), F32),
                        pltpu.VMEM((8, LANES), F32)],
        compiler_params=_params("arbitrary"),
    )(q, k, v, o, ifc, ifr, g)


def _dil_attn_kernel(kv_ref, q_ref, kvp_ref, o_ref, lse_ref, *, dil):
    B = Q_BLOCK
    span = B * dil
    n_sub = ATT_SUPER // span
    row = lax.broadcasted_iota(jnp.int32, (B, 2 * B), 0)
    col = lax.broadcasted_iota(jnp.int32, (B, 2 * B), 1)
    band = jnp.where(col >= row, jnp.where(col <= row + B, 0.0, NEG), NEG)
    first = jnp.where(pl.program_id(0) == 0, 1.0, 0.0)
    band_first = band + first * jnp.where(col < B, NEG, 0.0)
    lane = lax.broadcasted_iota(jnp.int32, (B, LANES), 1)
    lo = lane < DIL_DH
    hi = lane >= DIL_DH

    def rows(start):
        return pl.ds(start, B, stride=dil) if dil > 1 else pl.ds(start, B)

    def unit(cur_start, prev_ref, prev_start, bias):
        for half in range(2):
            q2 = q_ref[half, rows(cur_start), :]
            k2 = jnp.concatenate([prev_ref[half, rows(prev_start), :],
                                  kv_ref[half, rows(cur_start), :]], axis=0).astype(BF16)
            v2 = jnp.concatenate([prev_ref[2 + half, rows(prev_start), :],
                                  kv_ref[2 + half, rows(cur_start), :]], axis=0).astype(BF16)
            res = []
            for keep in (lo, hi):
                qm = jnp.where(keep, q2, 0.0).astype(BF16)
                s = _dot_nt(qm, k2) + bias
                mx = jnp.max(s, axis=-1, keepdims=True)
                p = jnp.exp(s - mx)
                den = jnp.sum(p, axis=-1, keepdims=True)
                res.append((_dot(p.astype(BF16), v2) / den, mx + jnp.log(den)))
            o_ref[half, rows(cur_start), :] = jnp.where(lo, res[0][0], res[1][0])
            lse_ref[half, rows(cur_start), :] = jnp.where(lo, res[0][1], res[1][1])

    def per_residue(r, carry):
        unit(r, kvp_ref, r, band_first)

        def per_sub(j, c):
            unit(j * span + r, kv_ref, (j - 1) * span + r, band)
            return c

        if 1 < n_sub <= ATT_UNROLL:
            for j in range(1, n_sub):
                per_sub(j, 0)
        elif n_sub > 1:
            lax.fori_loop(1, n_sub, per_sub, 0, unroll=ATT_UNROLL)
        return carry

    if dil >= ATT_UNROLL:
        lax.fori_loop(0, dil, per_residue, 0, unroll=max(1, ATT_UNROLL // n_sub))
    else:
        for r in range(dil):
            per_residue(r, 0)


def _dil_attn(qkv, dil):
    s = qkv.shape[1]
    span = Q_BLOCK * dil
    n_prev = ATT_SUPER // span
    blk = lambda n: pl.BlockSpec((n, ATT_SUPER, LANES), lambda i: (0, i, 0))
    return pl.pallas_call(
        functools.partial(_dil_attn_kernel, dil=dil),
        grid=(s // ATT_SUPER,),
        in_specs=[blk(4),
                  pl.BlockSpec((2, ATT_SUPER, LANES), lambda i: (2, i, 0)),
                  pl.BlockSpec((4, span, LANES), lambda i: (0, jnp.maximum(i * n_prev - 1, 0), 0))],
        out_specs=(blk(2), blk(2)),
        out_shape=(jax.ShapeDtypeStruct((2, s, LANES), F32),) * 2,
        compiler_params=_params("arbitrary"),
    )(qkv, qkv, qkv)


def _merge_core(x_ref, hm_ref, hx_ref, gate_ref, od_refs, lse_refs, wm_ref, wdd_ref, wx_ref, wo_ref):
    wide = lambda r: jnp.concatenate([r[0], r[1]], axis=-1)
    lses = [wide(r) for r in lse_refs]
    mx = jnp.maximum(jnp.maximum(lses[0], lses[1]), lses[2])
    es = [jnp.exp(l - mx) for l in lses]
    den = es[0] + es[1] + es[2]
    hd = (es[0] * wide(od_refs[0]) + es[1] * wide(od_refs[1]) + es[2] * wide(od_refs[2])) / den
    d = D_MODEL
    merged = (gate_ref[:, 0:d].astype(F32) * _dot(hm_ref[...], wm_ref[...])
              + gate_ref[:, d:2 * d].astype(F32) * _dot(hd.astype(BF16), wdd_ref[...])
              + gate_ref[:, 2 * d:3 * d].astype(F32) * _dot(hx_ref[...], wx_ref[...]))
    return x_ref[...] + _dot(merged.astype(BF16), wo_ref[...])


def _merge_dense_kernel(x_ref, hm_ref, hx_ref, gate_ref, o0, o1, o2, l0, l1, l2,
                        wm_ref, wdd_ref, wx_ref, wo_ref, gf_ref, wg_ref, wu_ref, wdn_ref,
                        out_ref, acc_ref):
    x1 = _merge_core(x_ref, hm_ref, hx_ref, gate_ref, (o0, o1, o2), (l0, l1, l2),
                     wm_ref, wdd_ref, wx_ref, wo_ref)
    u = _rms(x1, gf_ref[...]).astype(BF16)
    acc_ref[...] = x1

    def body(c, carry):
        cols = pl.ds(pl.multiple_of(c * FF_CHUNK, FF_CHUNK), FF_CHUNK)
        g = _dot(u, wg_ref[:, cols])
        hcol = (g * jax.nn.sigmoid(g) * _dot(u, wu_ref[:, cols])).astype(BF16)
        acc_ref[...] += _dot(hcol, wdn_ref[cols, :])
        return carry

    lax.fori_loop(0, D_FF // FF_CHUNK, body, 0)
    out_ref[...] = acc_ref[...]


def _merge_moe_kernel(x_ref, hm_ref, hx_ref, gate_ref, o0, o1, o2, l0, l1, l2,
                      wm_ref, wdd_ref, wx_ref, wo_ref, gf_ref, wr_ref,
                      x1_out, u_out, route_out, cnt_out, carry_ref):
    tm = x_ref.shape[0]

    @pl.when(pl.program_id(0) == 0)
    def _():
        carry_ref[...] = jnp.zeros(carry_ref.shape, F32)

    x1 = _merge_core(x_ref, hm_ref, hx_ref, gate_ref, (o0, o1, o2), (l0, l1, l2),
                     wm_ref, wdd_ref, wx_ref, wo_ref)
    x1_out[...] = x1
    uf = _rms(x1, gf_ref[...])
    u_out[...] = uf

    uh, um, ul = _split3(uf)
    wh, wmid, wl = _split3(wr_ref[...])
    logits = (_dot(uh, wh) + (_dot(uh, wmid) + _dot(um, wh))
              + (_dot(uh, wl) + _dot(um, wmid) + _dot(ul, wh)))
    lane = lax.broadcasted_iota(jnp.int32, (tm, LANES), 1).astype(F32)
    valid = lane < N_EXPERTS
    lg = jnp.where(valid, logits, NEG)
    ex = jnp.exp(lg - jnp.max(lg, axis=-1, keepdims=True))
    probs = jnp.where(valid, ex / jnp.sum(ex, axis=-1, keepdims=True), -1.0)
    p1 = jnp.max(probs, axis=-1, keepdims=True)
    i1 = jnp.min(jnp.where(probs == p1, lane, float(LANES)), axis=-1, keepdims=True)
    rest = jnp.where(lane == i1, -1.0, probs)
    p2 = jnp.max(rest, axis=-1, keepdims=True)
    i2 = jnp.min(jnp.where(rest == p2, lane, float(LANES)), axis=-1, keepdims=True)
    g1 = p1 / (p1 + p2)
    g2 = p2 / (p1 + p2)
    sel = jnp.where(lane == i1, 1.0, jnp.where(lane == i2, 1.0, 0.0))
    row = lax.broadcasted_iota(jnp.int32, (tm, tm), 0)
    col = lax.broadcasted_iota(jnp.int32, (tm, tm), 1)
    before = jnp.where(col < row, 1.0, 0.0).astype(BF16)
    ranks = _dot(before, sel.astype(BF16)) + carry_ref[0:1, :]
    r1 = jnp.sum(jnp.where(lane == i1, ranks, 0.0), axis=-1, keepdims=True)
    r2 = jnp.sum(jnp.where(lane == i2, ranks, 0.0), axis=-1, keepdims=True)
    carry_ref[...] = carry_ref[...] + jnp.sum(sel, axis=0, keepdims=True)
    cnt_out[...] = carry_ref[...]
    route = jnp.where(lane == 0, i1, jnp.where(lane == 1, i2, jnp.where(lane == 2, g1,
            jnp.where(lane == 3, g2, jnp.where(lane == 4, r1, jnp.where(lane == 5, r2, 0.0))))))
    route_out[...] = route


def _merge(x, hm, hx, gates, ods, lses, wm, wdd, wx, wo, gf, dense_w=None, w_router=None):
    s = x.shape[0]
    tm = TOK_TILE
    row = lambda w: pl.BlockSpec((tm, w), lambda i: (i, 0))
    acts = (x, hm, hx, gates) + tuple(ods) + tuple(lses)
    slab = pl.BlockSpec((2, tm, LANES), lambda i: (0, i, 0))
    act_specs = [row(D_MODEL), row(MLSTM_W), row(MEM_W), row(3 * D_MODEL)] + [slab] * 6
    if dense_w is not None:
        consts = (wm, wdd, wx, wo, gf) + tuple(dense_w)
        return pl.pallas_call(
            _merge_dense_kernel,
            grid=(s // tm,),
            in_specs=act_specs + [_const_spec(c.shape) for c in consts],
            out_specs=row(D_MODEL),
            out_shape=jax.ShapeDtypeStruct((s, D_MODEL), F32),
            scratch_shapes=[pltpu.VMEM((tm, D_MODEL), F32)],
            compiler_params=_params("arbitrary"),
        )(*acts, *consts)
    consts = (wm, wdd, wx, wo, gf, w_router)
    return pl.pallas_call(
        _merge_moe_kernel,
        grid=(s // tm,),
        in_specs=act_specs + [_const_spec(c.shape) for c in consts],
        out_specs=(row(D_MODEL), row(D_MODEL), row(LANES), pl.BlockSpec((8, LANES), lambda i: (0, 0))),
        out_shape=(jax.ShapeDtypeStruct((s, D_MODEL), F32),
                   jax.ShapeDtypeStruct((s, D_MODEL), F32),
                   jax.ShapeDtypeStruct((s, LANES), F32),
                   jax.ShapeDtypeStruct((8, LANES), F32)),
        scratch_shapes=[pltpu.VMEM((8, LANES), F32)],
        compiler_params=_params("arbitrary"),
    )(*acts, *consts)


def _for_rows(n, fn):
    def body(k, c):
        for j in range(SUBLANES):
            fn(k, j)
        return c
    lax.fori_loop(0, n // SUBLANES, body, 0)


def _dispatch_kernel(zb_ref, p1_ref, p2_ref, u_ref, xs_hbm, zero_ref, sem, zsem):
    i = pl.program_id(0)
    td = p1_ref.shape[1]
    bm = zero_ref.shape[0]

    @pl.when(i == 0)
    def _():
        zero_ref[...] = jnp.zeros(zero_ref.shape, F32)

        def fill(op):
            def body(b, c):
                @pl.when(zb_ref[b] != 0)
                def _():
                    op(pltpu.make_async_copy(zero_ref, xs_hbm.at[pl.ds(b * bm, bm), :], zsem))
                return c
            lax.fori_loop(0, zb_ref.shape[0], body, 0)

        fill(lambda c: c.start())
        fill(lambda c: c.wait())

    def copies(k, j):
        r = k * SUBLANES + j
        src = u_ref.at[k, pl.ds(j, 1), :]
        return (pltpu.make_async_copy(src, xs_hbm.at[pl.ds(p1_ref[0, r], 1), :], sem),
                pltpu.make_async_copy(src, xs_hbm.at[pl.ds(p2_ref[0, r], 1), :], sem))

    def start(k, j):
        a, b = copies(k, j)
        a.start(priority=0)
        b.start(priority=1)

    def wait(k, j):
        a, b = copies(k, j)
        a.wait()
        b.wait()

    _for_rows(td, start)
    _for_rows(td, wait)


def _dispatch(u, p1, p2, zb, n_rows):
    s = u.shape[0]
    td = MOE_TD
    tok = lambda: pl.BlockSpec((None, 1, td), lambda i, zb: (i, 0, 0), memory_space=pltpu.SMEM)
    grid_spec = pltpu.PrefetchScalarGridSpec(
        num_scalar_prefetch=1,
        grid=(s // td,),
        in_specs=[tok(), tok(),
                  pl.BlockSpec((td // SUBLANES, SUBLANES, D_MODEL), lambda i, zb: (i, 0, 0))],
        out_specs=pl.BlockSpec(memory_space=pl.ANY),
        scratch_shapes=[pltpu.VMEM((MOE_BM, D_MODEL), F32),
                        pltpu.SemaphoreType.DMA(()), pltpu.SemaphoreType.DMA(())],
    )
    return pl.pallas_call(
        _dispatch_kernel,
        grid_spec=grid_spec,
        out_shape=jax.ShapeDtypeStruct((n_rows, D_MODEL), F32),
        compiler_params=_params("arbitrary"),
    )(zb, p1.reshape(s // td, 1, td), p2.reshape(s // td, 1, td), u.reshape(-1, SUBLANES, D_MODEL))


def _expert_kernel(be_ref, bv_ref, bx_ref, x_ref, wg_ref, wu_ref, wd_ref, y_ref, xb_ref, acc_ref):
    b = pl.program_id(0)
    f = pl.program_id(1)
    nf = pl.num_programs(1)
    valid = bv_ref[b] != 0

    @pl.when(jnp.logical_and(valid, f == 0))
    def _():
        xb_ref[...] = x_ref[...].astype(BF16)

    @pl.when(valid)
    def _():
        x = xb_ref[...]
        g = _dot(x, wg_ref[...])
        hcol = (g * jax.nn.sigmoid(g) * _dot(x, wu_ref[...])).astype(BF16)
        part = _dot(hcol, wd_ref[...])

        @pl.when(f == 0)
        def _():
            acc_ref[...] = part

        @pl.when(jnp.logical_and(f != 0, f != nf - 1))
        def _():
            acc_ref[...] += part

        @pl.when(f == nf - 1)
        def _():
            y_ref[...] = acc_ref[...] + part

    @pl.when(jnp.logical_and(jnp.logical_not(valid), f == nf - 1))
    def _():
        y_ref[...] = jnp.zeros(y_ref.shape, F32)


def _experts(xs, wg, wu, wd, be, bv, bx):
    n_rows = xs.shape[0]
    bm, tf = MOE_BM, MOE_TF
    nf = D_FF_EXPERT // tf
    assert nf >= 2

    def fidx(b, f, bv):
        return jnp.where(bv[b] != 0, f, nf - 1)

    grid_spec = pltpu.PrefetchScalarGridSpec(
        num_scalar_prefetch=3,
        grid=(n_rows // bm, nf),
        in_specs=[pl.BlockSpec((bm, D_MODEL), lambda b, f, be, bv, bx: (bx[b], 0)),
                  pl.BlockSpec((None, D_MODEL, tf), lambda b, f, be, bv, bx: (be[b], 0, fidx(b, f, bv))),
                  pl.BlockSpec((None, D_MODEL, tf), lambda b, f, be, bv, bx: (be[b], 0, fidx(b, f, bv))),
                  pl.BlockSpec((None, tf, D_MODEL), lambda b, f, be, bv, bx: (be[b], fidx(b, f, bv), 0))],
        out_specs=pl.BlockSpec((bm, D_MODEL), lambda b, f, be, bv, bx: (b, 0)),
        scratch_shapes=[pltpu.VMEM((bm, D_MODEL), BF16),
                        pltpu.VMEM((bm, D_MODEL), F32)],
    )
    return pl.pallas_call(
        _expert_kernel,
        grid_spec=grid_spec,
        out_shape=jax.ShapeDtypeStruct((n_rows, D_MODEL), F32),
        compiler_params=_params("arbitrary", "arbitrary"),
    )(be, bv, bx, xs, wg, wu, wd)


def _combine_kernel(p1_ref, p2_ref, p1n_ref, p2n_ref, x_ref, route_ref, gn_ref, ys_hbm, out_ref,
                    yg_ref, sem, *, final_norm):
    i = pl.program_id(0)
    tc = x_ref.shape[0]
    slot = lax.rem(i, 2)

    def gather(pa_ref, pb_ref, s_, op):
        def rows(k, j):
            r = k * SUBLANES + j
            op(pltpu.make_async_copy(ys_hbm.at[pl.ds(pa_ref[0, r], 1), :],
                                     yg_ref.at[s_, 0, k, pl.ds(j, 1), :], sem.at[s_]), 0)
            op(pltpu.make_async_copy(ys_hbm.at[pl.ds(pb_ref[0, r], 1), :],
                                     yg_ref.at[s_, 1, k, pl.ds(j, 1), :], sem.at[s_]), 1)
        _for_rows(tc, rows)

    start = lambda c, queue: c.start(priority=queue)
    wait = lambda c, queue: c.wait()

    @pl.when(i == 0)
    def _():
        gather(p1_ref, p2_ref, 0, start)

    gather(p1_ref, p2_ref, slot, wait)

    @pl.when(i + 1 < pl.num_programs(0))
    def _():
        gather(p1n_ref, p2n_ref, 1 - slot, start)

    y1 = yg_ref[slot, 0].reshape(tc, D_MODEL)
    y2 = yg_ref[slot, 1].reshape(tc, D_MODEL)
    out = x_ref[...] + route_ref[:, 2:3] * y1 + route_ref[:, 3:4] * y2
    out_ref[...] = _rms(out, gn_ref[...]) if final_norm else out


def _combine(x1, ys, p1, p2, route, gn, final_norm):
    s = x1.shape[0]
    tc = MOE_TC
    nt = s // tc
    row = lambda w: pl.BlockSpec((tc, w), lambda i: (i, 0))
    cur = lambda: pl.BlockSpec((None, 1, tc), lambda i: (i, 0, 0), memory_space=pltpu.SMEM)
    nxt = lambda: pl.BlockSpec((None, 1, tc), lambda i: (jnp.minimum(i + 1, nt - 1), 0, 0),
                               memory_space=pltpu.SMEM)
    p1 = p1.reshape(nt, 1, tc)
    p2 = p2.reshape(nt, 1, tc)
    return pl.pallas_call(
        functools.partial(_combine_kernel, final_norm=final_norm),
        grid=(nt,),
        in_specs=[cur(), cur(), nxt(), nxt(), row(D_MODEL), row(LANES), _const_spec(gn.shape),
                  pl.BlockSpec(memory_space=pl.ANY)],
        out_specs=row(D_MODEL),
        out_shape=jax.ShapeDtypeStruct((s, D_MODEL), F32),
        scratch_shapes=[pltpu.VMEM((2, 2, tc // SUBLANES, SUBLANES, D_MODEL), F32),
                        pltpu.SemaphoreType.DMA((2,))],
        compiler_params=_params("arbitrary"),
    )(p1, p2, p1, p2, x1, route, gn, ys)


def _moe_plan(route, counts, s):
    bm = MOE_BM
    nb = (2 * s) // bm + N_EXPERTS
    i1 = route[:, 0].astype(jnp.int32)
    i2 = route[:, 1].astype(jnp.int32)
    r1 = route[:, 4].astype(jnp.int32)
    r2 = route[:, 5].astype(jnp.int32)
    cnt = counts[0, :N_EXPERTS].astype(jnp.int32)
    padded = ((cnt + bm - 1) // bm) * bm
    ends = jnp.cumsum(padded)
    off = ends - padded
    p1 = off[i1] + r1
    p2 = off[i2] + r2
    nb_used = ends[-1] // bm

    bidx = jnp.arange(nb, dtype=jnp.int32)
    bvalid = (bidx < nb_used).astype(jnp.int32)
    bsrc = jnp.minimum(bidx, jnp.maximum(nb_used - 1, 0))
    bexp = jnp.sum((bsrc[:, None] * bm >= ends[None, :]).astype(jnp.int32), axis=1)
    bexp = jnp.minimum(bexp, N_EXPERTS - 1)
    real = jnp.clip(off[bexp] + cnt[bexp] - bidx * bm, 0, bm)
    bzero = ((real < bm) | (bvalid == 0)).astype(jnp.int32)
    return p1, p2, bexp, bvalid, bsrc, bzero


def _inproj_weight(w_in, q_scale):
    cols = [w_in[:, :OFF_QD], jnp.zeros((D_MODEL, LANES - 2 * MLSTM_HEADS), w_in.dtype)]
    for g in range(len(DIL_PATTERNS)):
        sl = lambda off: w_in[:, off + g * DIL_GW: off + (g + 1) * DIL_GW]
        cols += [sl(OFF_KD), sl(OFF_VD), sl(OFF_QD) * q_scale]
    cols.append(w_in[:, OFF_QX:])
    return jnp.concatenate(cols, axis=-1).astype(BF16)


def kernel(x, mem, norm_mix, w_in, conv_qk, b_gate_if, mlstm_norm, norm_mem, w_mem_kv, w_br_m, w_br_d,
           w_br_x, w_out, norm_ffn, ffn_w_gate, ffn_w_up, ffn_w_down, moe_router, moe_w_gate, moe_w_up,
           moe_w_down, norm_final):
    s = x.shape[1]
    xs = x.reshape(s, D_MODEL)
    mem2 = mem.reshape(N_MEM, D_MODEL)
    row = lambda a: a.reshape(1, -1)
    q_scale = DIL_DH ** -0.5

    for layer in range(DEPTH):
        wl = w_in[layer]
        g_mix = row(norm_mix[layer])
        km, vm = _memkv(mem2, row(norm_mem[layer]), w_mem_kv[layer].astype(BF16))

        wift = wl[:, OFF_IF:OFF_QD].T.astype(BF16)
        bif = jnp.pad(b_gate_if[layer], (0, LANES - 2 * MLSTM_HEADS)).reshape(1, LANES)
        bift = b_gate_if[layer].reshape(2 * MLSTM_HEADS, 1)
        (q_m, k_m, v_m, o_m, ifc, ifr, d0, d1, d2, h_x, gates) = _inproj(
            xs, g_mix, _inproj_weight(wl, q_scale), wift, bif, bift, conv_qk[layer], km, vm)

        h_m = _mlstm(q_m, k_m, v_m, o_m, ifc, ifr, row(mlstm_norm[layer]))

        ods, lses = [], []
        for qkv, (_, dil) in zip((d0, d1, d2), DIL_PATTERNS):
            o_g, lse_g = _dil_attn(qkv, dil)
            ods.append(o_g)
            lses.append(lse_g)

        merge_w = (w_br_m[layer].astype(BF16), w_br_d[layer].astype(BF16), w_br_x[layer].astype(BF16),
                   w_out[layer].astype(BF16), row(norm_ffn[layer]))
        if layer % 2 == 0:
            li = layer // 2
            dense_w = (ffn_w_gate[li].astype(BF16), ffn_w_up[li].astype(BF16), ffn_w_down[li].astype(BF16))
            xs = _merge(xs, h_m, h_x, gates, ods, lses, *merge_w, dense_w=dense_w)
        else:
            li = layer // 2
            wr = jnp.pad(moe_router[li], ((0, 0), (0, LANES - N_EXPERTS)))
            x1, u, route, counts = _merge(xs, h_m, h_x, gates, ods, lses, *merge_w, w_router=wr)
            p1, p2, bexp, bvalid, bsrc, bzero = _moe_plan(route, counts, s)
            rows = _dispatch(u, p1, p2, bzero, bexp.shape[0] * MOE_BM)
            y = _experts(rows, moe_w_gate[li].astype(BF16), moe_w_up[li].astype(BF16),
                         moe_w_down[li].astype(BF16), bexp, bvalid, bsrc)
            xs = _combine(x1, y, p1, p2, route, row(norm_final), final_norm=layer == DEPTH - 1)
    return xs.reshape(x.shape)
```

```python
import functools

import jax
import jax.numpy as jnp
from jax import lax
from jax.experimental import pallas as pl
from jax.experimental.pallas import tpu as pltpu

F32 = jnp.float32
BF16 = jnp.bfloat16

EPS = 1e-6
D_MODEL = 1024
DEPTH = 4
N_MEM = 256
MLSTM_HEADS = 4
MLSTM_DH = 128
MLSTM_W = MLSTM_HEADS * MLSTM_DH
MLSTM_CHUNK = 128
CONV_W = 4
M_INIT = -1e30
DIL_PATTERNS = ((128, 1), (512, 4), (2048, 16))
DIL_HEADS = 4
DIL_DH = 64
DIL_GW = DIL_HEADS * DIL_DH
DIL_W = 3 * DIL_GW
Q_BLOCK = 128
MEM_HEADS = 4
MEM_DH = 128
MEM_W = MEM_HEADS * MEM_DH
D_FF = 2816
N_EXPERTS = 8
D_FF_EXPERT = 3584

OFF_IF = 4 * MLSTM_W
OFF_QD = OFF_IF + 2 * MLSTM_HEADS
OFF_KD = OFF_QD + DIL_W
OFF_VD = OFF_KD + DIL_W
OFF_QX = OFF_VD + DIL_W
OFF_GATE = OFF_QX + MEM_W
IN_COLS = OFF_GATE + 3 * D_MODEL

LANES = 128
SUBLANES = 8
NEG = -1e30
VMEM_LIMIT = 56 * 1024 * 1024

DIL_SLABS = 3 * DIL_GW // LANES

WCOL_IF = 4 * MLSTM_W
WCOL_DIL = WCOL_IF + LANES
WCOL_QX = WCOL_DIL + 3 * DIL_W
WCOL_GATE = WCOL_QX + MEM_W
WCOL_END = WCOL_GATE + 3 * D_MODEL
MLSTM_CHUNKS_PER_STEP = 1
ATT_UNROLL = 4
ATT_SUPER = 2048
TOK_TILE = 512
FF_CHUNK = 256
MOE_BM = 512
MOE_TD = 512
MOE_TC = 256
MOE_TF = 1792

NT_DIMS = (((1,), (1,)), ((), ()))
TN_DIMS = (((0,), (0,)), ((), ()))


def _params(*sem):
    return pltpu.CompilerParams(dimension_semantics=sem, vmem_limit_bytes=VMEM_LIMIT)


def _dot(a, b):
    return jnp.dot(a, b, preferred_element_type=F32)


def _dot_nt(a, b):
    return lax.dot_general(a, b, NT_DIMS, preferred_element_type=F32)


def _rms(x, g):
    return x * lax.rsqrt(jnp.mean(x * x, axis=-1, keepdims=True) + EPS) * g


def _split3(x):
    hi = x.astype(BF16)
    r1 = x - hi.astype(F32)
    mid = r1.astype(BF16)
    lo = (r1 - mid.astype(F32)).astype(BF16)
    return hi, mid, lo


def _const_spec(shape):
    nd = len(shape)
    return pl.BlockSpec(shape, lambda *_: (0,) * nd, pipeline_mode=pl.Buffered(1))


def _memkv_kernel(mem_ref, g_ref, w_ref, k_ref, v_ref):
    u = _rms(mem_ref[...], g_ref[...]).astype(BF16)
    kv = _dot(u, w_ref[...])
    k_ref[...] = kv[:, :MEM_W].astype(BF16)
    v_ref[...] = kv[:, MEM_W:].astype(BF16)


def _memkv(mem, g, w_kv):
    return pl.pallas_call(
        _memkv_kernel,
        out_shape=(jax.ShapeDtypeStruct((N_MEM, MEM_W), BF16),) * 2,
        compiler_params=pltpu.CompilerParams(vmem_limit_bytes=VMEM_LIMIT),
    )(mem, g, w_kv)


def _inproj_kernel(x_ref, g_ref, w_ref, bif_ref, cw_ref, km_ref, vm_ref,
                   q_out, k_out, v_out, o_out, if_out, ift_out, d0_out, d1_out, d2_out,
                   hx_out, gate_out, conv_buf):
    tm = x_ref.shape[0]
    u = _rms(x_ref[...], g_ref[...]).astype(BF16)

    @pl.when(pl.program_id(0) == 0)
    def _():
        conv_buf[0:8, :] = jnp.zeros((8, 2 * MLSTM_W), F32)

    conv_buf[8:tm + 8, :] = _dot(u, w_ref[:, 0:2 * MLSTM_W])
    acc = cw_ref[0:1, :] * conv_buf[pl.ds(8 - (CONV_W - 1), tm), :]
    for j in range(1, CONV_W):
        acc = acc + cw_ref[j:j + 1, :] * conv_buf[pl.ds(8 - (CONV_W - 1) + j, tm), :]
    conv_buf[0:8, :] = conv_buf[tm:tm + 8, :]
    qk = acc * jax.nn.sigmoid(acc)
    q_out[...] = qk[:, :MLSTM_W].astype(BF16)
    k_out[...] = (qk[:, MLSTM_W:] * (MLSTM_DH ** -0.5)).astype(BF16)

    v_out[...] = _dot(u, w_ref[:, 2 * MLSTM_W:3 * MLSTM_W]).astype(BF16)
    o_out[...] = jax.nn.sigmoid(_dot(u, w_ref[:, 3 * MLSTM_W:4 * MLSTM_W])).astype(BF16)

    if_pre = _dot(u, w_ref[:, WCOL_IF:WCOL_DIL]) + bif_ref[...]
    if_out[...] = if_pre
    ift_out[...] = if_pre.T[0:ift_out.shape[0], :]

    for gi, d_out in enumerate((d0_out, d1_out, d2_out)):
        d = _dot(u, w_ref[:, WCOL_DIL + gi * 3 * DIL_GW:WCOL_DIL + (gi + 1) * 3 * DIL_GW])
        for j in range(DIL_SLABS):
            d_out[j] = d[:, j * LANES:(j + 1) * LANES]

    qx = (_dot(u, w_ref[:, WCOL_QX:WCOL_GATE]) * (MEM_DH ** -0.5)).astype(BF16)
    outs = []
    for h in range(MEM_HEADS):
        sl = slice(h * MEM_DH, (h + 1) * MEM_DH)
        s = _dot_nt(qx[:, sl], km_ref[:, sl])
        p = jnp.exp(s - jnp.max(s, axis=-1, keepdims=True))
        den = jnp.sum(p, axis=-1, keepdims=True)
        outs.append(_dot(p.astype(BF16), vm_ref[:, sl]) / den)
    hx_out[...] = jnp.concatenate(outs, axis=-1).astype(BF16)

    gate_out[...] = jax.nn.sigmoid(_dot(u, w_ref[:, WCOL_GATE:WCOL_END])).astype(BF16)


def _inproj(x, g, w, bif, cw, km, vm):
    s = x.shape[0]
    tm = TOK_TILE
    row = lambda w: pl.BlockSpec((tm, w), lambda i: (i, 0))
    out_shape = (
        jax.ShapeDtypeStruct((s, MLSTM_W), BF16),
        jax.ShapeDtypeStruct((s, MLSTM_W), BF16),
        jax.ShapeDtypeStruct((s, MLSTM_W), BF16),
        jax.ShapeDtypeStruct((s, MLSTM_W), BF16),
        jax.ShapeDtypeStruct((s, LANES), F32),
        jax.ShapeDtypeStruct((8, s), F32),
        jax.ShapeDtypeStruct((DIL_SLABS, s, LANES), F32),
        jax.ShapeDtypeStruct((DIL_SLABS, s, LANES), F32),
        jax.ShapeDtypeStruct((DIL_SLABS, s, LANES), F32),
        jax.ShapeDtypeStruct((s, MEM_W), BF16),
        jax.ShapeDtypeStruct((s, 3 * D_MODEL), BF16),
    )
    slab = pl.BlockSpec((DIL_SLABS, tm, LANES), lambda i: (0, i, 0))
    out_specs = (row(MLSTM_W), row(MLSTM_W), row(MLSTM_W), row(MLSTM_W), row(LANES),
                 pl.BlockSpec((8, tm), lambda i: (0, i)),
                 slab, slab, slab, row(MEM_W), row(3 * D_MODEL))
    in_specs = [row(D_MODEL)] + [_const_spec(a.shape) for a in (g, w, bif, cw, km, vm)]
    return pl.pallas_call(
        _inproj_kernel,
        grid=(s // tm,),
        in_specs=in_specs,
        out_specs=out_specs,
        out_shape=out_shape,
        scratch_shapes=[pltpu.VMEM((tm + 8, 2 * MLSTM_W), F32)],
        compiler_params=_params("arbitrary"),
    )(x, g, w, bif, cw, km, vm)


def _log_sigmoid(x):
    return jnp.minimum(x, 0.0) - jnp.log(1.0 + jnp.exp(-jnp.abs(x)))


def _mlstm_kernel(q_ref, k_ref, v_ref, o_ref, ifc_ref, ifr_ref, g_ref, out_ref, ct_ref, m_ref):
    L = MLSTM_CHUNK
    H = MLSTM_HEADS

    @pl.when(pl.program_id(0) == 0)
    def _():
        ct_ref[...] = jnp.zeros(ct_ref.shape, F32)
        m_ref[...] = jnp.full(m_ref.shape, M_INIT, F32)

    row = lax.broadcasted_iota(jnp.int32, (L, L), 0)
    col = lax.broadcasted_iota(jnp.int32, (L, L), 1)
    causal = col <= row
    tril = jnp.where(causal, 1.0, 0.0).astype(BF16)
    triu = jnp.where(row <= col, 1.0, 0.0).astype(BF16)
    ones_col = jnp.where(col == 0, 1.0, 0.0).astype(BF16)

    for ci in range(q_ref.shape[0] // L):
        rows = slice(ci * L, (ci + 1) * L)
        ifc = ifc_ref[rows, :]
        ifr = ifr_ref[:, rows]
        cum_c = sum(_dot(tril, p) for p in _split3(_log_sigmoid(ifc)))
        cum_r = sum(_dot(p, triu) for p in _split3(_log_sigmoid(ifr)))

        for h in range(H):
            sl = slice(h * MLSTM_DH, (h + 1) * MLSTM_DH)
            i_c = ifc[:, h:h + 1]
            i_r = ifr[h:h + 1, :]
            cc = cum_c[:, H + h:H + h + 1]
            cr = cum_r[H + h:H + h + 1, :]
            total = cr[:, L - 1:L]
            m_prev = m_ref[h:h + 1, 0:1]

            dm = jnp.where(causal, cc - cr + i_r, -jnp.inf)
            inter = cc + m_prev
            m_row = jnp.maximum(jnp.max(dm, axis=-1, keepdims=True), inter)
            w_intra = jnp.exp(dm - m_row)
            w_inter = jnp.exp(inter - m_row)

            qh = q_ref[rows, sl]
            kh = k_ref[rows, sl]
            vaug = jnp.concatenate([v_ref[rows, sl], ones_col], axis=-1)
            s_mat = _dot_nt(qh, kh) * w_intra
            tot = _dot(s_mat.astype(BF16), vaug) + w_inter * _dot(qh, ct_ref[h].astype(BF16))
            den = tot[:, MLSTM_DH:MLSTM_DH + 1]
            h_out = tot[:, :MLSTM_DH] / jnp.maximum(jnp.abs(den), jnp.exp(-m_row))

            g_end = total - cc + i_c
            m_new = jnp.maximum(total + m_prev, jnp.max(g_end, axis=0, keepdims=True))
            w_end = jnp.exp(g_end - m_new)
            decay = jnp.exp(total + m_prev - m_new)
            vw = (vaug.astype(F32) * w_end).astype(BF16)
            ct_ref[h] = decay * ct_ref[h] + lax.dot_general(kh, vw, TN_DIMS, preferred_element_type=F32)
            m_ref[h:h + 1, :] = jnp.broadcast_to(m_new, (1, LANES))

            mu = jnp.mean(h_out, axis=-1, keepdims=True)
            cen = h_out - mu
            var = jnp.mean(cen * cen, axis=-1, keepdims=True)
            y = cen * lax.rsqrt(var + EPS) * g_ref[:, sl] * o_ref[rows, sl].astype(F32)
            out_ref[rows, sl] = y.astype(BF16)


def _mlstm(q, k, v, o, ifc, ifr, g):
    s = q.shape[0]
    L = MLSTM_CHUNK * MLSTM_CHUNKS_PER_STEP
    row = pl.BlockSpec((L, MLSTM_W), lambda c: (c, 0))
    return pl.pallas_call(
        _mlstm_kernel,
        grid=(s // L,),
        in_specs=[row, row, row, row,
                  pl.BlockSpec((L, LANES), lambda c: (c, 0)),
                  pl.BlockSpec((8, L), lambda c: (0, c)),
                  _const_spec(g.shape)],
        out_specs=row,
        out_shape=jax.ShapeDtypeStruct((s, MLSTM_W), BF16),
        scratch_shapes=[pltpu.VMEM((MLSTM_HEADS, MLSTM_DH, 2 * MLSTM_DH), F32),
                        pltpu.VMEM((8, LANES), F32)],
        compiler_params=_params("arbitrary"),
    )(q, k, v, o, ifc, ifr, g)


def _dil_attn_kernel(kv_ref, q_ref, kvp_ref, o_ref, lse_ref, *, dil):
    B = Q_BLOCK
    span = B * dil
    n_sub = ATT_SUPER // span
    row = lax.broadcasted_iota(jnp.int32, (B, 2 * B), 0)
    col = lax.broadcasted_iota(jnp.int32, (B, 2 * B), 1)
    band = jnp.where(col >= row, jnp.where(col <= row + B, 0.0, NEG), NEG)
    first = jnp.where(pl.program_id(0) == 0, 1.0, 0.0)
    band_first = band + first * jnp.where(col < B, NEG, 0.0)
    lane = lax.broadcasted_iota(jnp.int32, (B, LANES), 1)
    lo = lane < DIL_DH
    hi = lane >= DIL_DH

    def rows(start):
        return pl.ds(start, B, stride=dil) if dil > 1 else pl.ds(start, B)

    def unit(cur_start, prev_ref, prev_start, bias):
        for half in range(2):
            q2 = q_ref[half, rows(cur_start), :]
            k2 = jnp.concatenate([prev_ref[half, rows(prev_start), :],
                                  kv_ref[half, rows(cur_start), :]], axis=0).astype(BF16)
            v2 = jnp.concatenate([prev_ref[2 + half, rows(prev_start), :],
                                  kv_ref[2 + half, rows(cur_start), :]], axis=0).astype(BF16)
            res = []
            for keep in (lo, hi):
                qm = jnp.where(keep, q2, 0.0).astype(BF16)
                s = _dot_nt(qm, k2) + bias
                mx = jnp.max(s, axis=-1, keepdims=True)
                p = jnp.exp(s - mx)
                den = jnp.sum(p, axis=-1, keepdims=True)
                res.append((_dot(p.astype(BF16), v2) / den, mx + jnp.log(den)))
            o_ref[half, rows(cur_start), :] = jnp.where(lo, res[0][0], res[1][0])
            lse_ref[half, rows(cur_start), :] = jnp.where(lo, res[0][1], res[1][1])

    def per_residue(r, carry):
        unit(r, kvp_ref, r, band_first)

        def per_sub(j, c):
            unit(j * span + r, kv_ref, (j - 1) * span + r, band)
            return c

        if 1 < n_sub <= ATT_UNROLL:
            for j in range(1, n_sub):
                per_sub(j, 0)
        elif n_sub > 1:
            lax.fori_loop(1, n_sub, per_sub, 0, unroll=ATT_UNROLL)
        return carry

    if dil >= ATT_UNROLL:
        lax.fori_loop(0, dil, per_residue, 0, unroll=max(1, ATT_UNROLL // n_sub))
    else:
        for r in range(dil):
            per_residue(r, 0)


def _dil_attn(qkv, dil):
    s = qkv.shape[1]
    span = Q_BLOCK * dil
    n_prev = ATT_SUPER // span
    blk = lambda n: pl.BlockSpec((n, ATT_SUPER, LANES), lambda i: (0, i, 0))
    return pl.pallas_call(
        functools.partial(_dil_attn_kernel, dil=dil),
        grid=(s // ATT_SUPER,),
        in_specs=[blk(4),
                  pl.BlockSpec((2, ATT_SUPER, LANES), lambda i: (2, i, 0)),
                  pl.BlockSpec((4, span, LANES), lambda i: (0, jnp.maximum(i * n_prev - 1, 0), 0))],
        out_specs=(blk(2), blk(2)),
        out_shape=(jax.ShapeDtypeStruct((2, s, LANES), F32),) * 2,
        compiler_params=_params("arbitrary"),
    )(qkv, qkv, qkv)


def _merge_core(x_ref, hm_ref, hx_ref, gate_ref, od_refs, lse_refs, wm_ref, wdd_ref, wx_ref, wo_ref):
    wide = lambda r: jnp.concatenate([r[0], r[1]], axis=-1)
    lses = [wide(r) for r in lse_refs]
    mx = jnp.maximum(jnp.maximum(lses[0], lses[1]), lses[2])
    es = [jnp.exp(l - mx) for l in lses]
    den = es[0] + es[1] + es[2]
    hd = (es[0] * wide(od_refs[0]) + es[1] * wide(od_refs[1]) + es[2] * wide(od_refs[2])) / den
    d = D_MODEL
    merged = (gate_ref[:, 0:d].astype(F32) * _dot(hm_ref[...], wm_ref[...])
              + gate_ref[:, d:2 * d].astype(F32) * _dot(hd.astype(BF16), wdd_ref[...])
              + gate_ref[:, 2 * d:3 * d].astype(F32) * _dot(hx_ref[...], wx_ref[...]))
    return x_ref[...] + _dot(merged.astype(BF16), wo_ref[...])


def _merge_dense_kernel(x_ref, hm_ref, hx_ref, gate_ref, o0, o1, o2, l0, l1, l2,
                        wm_ref, wdd_ref, wx_ref, wo_ref, gf_ref, wg_ref, wu_ref, wdn_ref,
                        out_ref, acc_ref):
    x1 = _merge_core(x_ref, hm_ref, hx_ref, gate_ref, (o0, o1, o2), (l0, l1, l2),
                     wm_ref, wdd_ref, wx_ref, wo_ref)
    u = _rms(x1, gf_ref[...]).astype(BF16)
    acc_ref[...] = x1

    def body(c, carry):
        cols = pl.ds(pl.multiple_of(c * FF_CHUNK, FF_CHUNK), FF_CHUNK)
        g = _dot(u, wg_ref[:, cols])
        hcol = (g * jax.nn.sigmoid(g) * _dot(u, wu_ref[:, cols])).astype(BF16)
        acc_ref[...] += _dot(hcol, wdn_ref[cols, :])
        return carry

    lax.fori_loop(0, D_FF // FF_CHUNK, body, 0)
    out_ref[...] = acc_ref[...]


def _merge_moe_kernel(x_ref, hm_ref, hx_ref, gate_ref, o0, o1, o2, l0, l1, l2,
                      wm_ref, wdd_ref, wx_ref, wo_ref, gf_ref, wr_ref,
                      x1_out, u_out, route_out, cnt_out, carry_ref):
    tm = x_ref.shape[0]

    @pl.when(pl.program_id(0) == 0)
    def _():
        carry_ref[...] = jnp.zeros(carry_ref.shape, F32)

    x1 = _merge_core(x_ref, hm_ref, hx_ref, gate_ref, (o0, o1, o2), (l0, l1, l2),
                     wm_ref, wdd_ref, wx_ref, wo_ref)
    x1_out[...] = x1
    uf = _rms(x1, gf_ref[...])
    u_out[...] = uf

    uh, um, ul = _split3(uf)
    wh, wmid, wl = _split3(wr_ref[...])
    logits = (_dot(uh, wh) + (_dot(uh, wmid) + _dot(um, wh))
              + (_dot(uh, wl) + _dot(um, wmid) + _dot(ul, wh)))
    lane = lax.broadcasted_iota(jnp.int32, (tm, LANES), 1).astype(F32)
    valid = lane < N_EXPERTS
    lg = jnp.where(valid, logits, NEG)
    ex = jnp.exp(lg - jnp.max(lg, axis=-1, keepdims=True))
    probs = jnp.where(valid, ex / jnp.sum(ex, axis=-1, keepdims=True), -1.0)
    p1 = jnp.max(probs, axis=-1, keepdims=True)
    i1 = jnp.min(jnp.where(probs == p1, lane, float(LANES)), axis=-1, keepdims=True)
    rest = jnp.where(lane == i1, -1.0, probs)
    p2 = jnp.max(rest, axis=-1, keepdims=True)
    i2 = jnp.min(jnp.where(rest == p2, lane, float(LANES)), axis=-1, keepdims=True)
    g1 = p1 / (p1 + p2)
    g2 = p2 / (p1 + p2)
    sel = jnp.where(lane == i1, 1.0, jnp.where(lane == i2, 1.0, 0.0))
    row = lax.broadcasted_iota(jnp.int32, (tm, tm), 0)
    col = lax.broadcasted_iota(jnp.int32, (tm, tm), 1)
    before = jnp.where(col < row, 1.0, 0.0).astype(BF16)
    ranks = _dot(before, sel.astype(BF16)) + carry_ref[0:1, :]
    r1 = jnp.sum(jnp.where(lane == i1, ranks, 0.0), axis=-1, keepdims=True)
    r2 = jnp.sum(jnp.where(lane == i2, ranks, 0.0), axis=-1, keepdims=True)
    carry_ref[...] = carry_ref[...] + jnp.sum(sel, axis=0, keepdims=True)
    cnt_out[...] = carry_ref[...]
    route = jnp.where(lane == 0, i1, jnp.where(lane == 1, i2, jnp.where(lane == 2, g1,
            jnp.where(lane == 3, g2, jnp.where(lane == 4, r1, jnp.where(lane == 5, r2, 0.0))))))
    route_out[...] = route


def _merge(x, hm, hx, gates, ods, lses, wm, wdd, wx, wo, gf, dense_w=None, w_router=None):
    s = x.shape[0]
    tm = TOK_TILE
    row = lambda w: pl.BlockSpec((tm, w), lambda i: (i, 0))
    acts = (x, hm, hx, gates) + tuple(ods) + tuple(lses)
    slab = pl.BlockSpec((2, tm, LANES), lambda i: (0, i, 0))
    act_specs = [row(D_MODEL), row(MLSTM_W), row(MEM_W), row(3 * D_MODEL)] + [slab] * 6
    if dense_w is not None:
        consts = (wm, wdd, wx, wo, gf) + tuple(dense_w)
        return pl.pallas_call(
            _merge_dense_kernel,
            grid=(s // tm,),
            in_specs=act_specs + [_const_spec(c.shape) for c in consts],
            out_specs=row(D_MODEL),
            out_shape=jax.ShapeDtypeStruct((s, D_MODEL), F32),
            scratch_shapes=[pltpu.VMEM((tm, D_MODEL), F32)],
            compiler_params=_params("arbitrary"),
        )(*acts, *consts)
    consts = (wm, wdd, wx, wo, gf, w_router)
    return pl.pallas_call(
        _merge_moe_kernel,
        grid=(s // tm,),
        in_specs=act_specs + [_const_spec(c.shape) for c in consts],
        out_specs=(row(D_MODEL), row(D_MODEL), row(LANES), pl.BlockSpec((8, LANES), lambda i: (0, 0))),
        out_shape=(jax.ShapeDtypeStruct((s, D_MODEL), F32),
                   jax.ShapeDtypeStruct((s, D_MODEL), F32),
                   jax.ShapeDtypeStruct((s, LANES), F32),
                   jax.ShapeDtypeStruct((8, LANES), F32)),
        scratch_shapes=[pltpu.VMEM((8, LANES), F32)],
        compiler_params=_params("arbitrary"),
    )(*acts, *consts)


def _for_rows(n, fn):
    def body(k, c):
        for j in range(SUBLANES):
            fn(k, j)
        return c
    lax.fori_loop(0, n // SUBLANES, body, 0)


def _dispatch_kernel(zb_ref, p1_ref, p2_ref, u_ref, xs_hbm, zero_ref, sem, zsem):
    i = pl.program_id(0)
    td = p1_ref.shape[1]
    bm = zero_ref.shape[0]

    @pl.when(i == 0)
    def _():
        zero_ref[...] = jnp.zeros(zero_ref.shape, F32)

        def fill(op):
            def body(b, c):
                @pl.when(zb_ref[b] != 0)
                def _():
                    op(pltpu.make_async_copy(zero_ref, xs_hbm.at[pl.ds(b * bm, bm), :], zsem))
                return c
            lax.fori_loop(0, zb_ref.shape[0], body, 0)

        fill(lambda c: c.start())
        fill(lambda c: c.wait())

    def copies(k, j):
        r = k * SUBLANES + j
        src = u_ref.at[k, pl.ds(j, 1), :]
        return (pltpu.make_async_copy(src, xs_hbm.at[pl.ds(p1_ref[0, r], 1), :], sem),
                pltpu.make_async_copy(src, xs_hbm.at[pl.ds(p2_ref[0, r], 1), :], sem))

    def start(k, j):
        a, b = copies(k, j)
        a.start(priority=0)
        b.start(priority=1)

    def wait(k, j):
        a, b = copies(k, j)
        a.wait()
        b.wait()

    _for_rows(td, start)
    _for_rows(td, wait)


def _dispatch(u, p1, p2, zb, n_rows):
    s = u.shape[0]
    td = MOE_TD
    tok = lambda: pl.BlockSpec((None, 1, td), lambda i, zb: (i, 0, 0), memory_space=pltpu.SMEM)
    grid_spec = pltpu.PrefetchScalarGridSpec(
        num_scalar_prefetch=1,
        grid=(s // td,),
        in_specs=[tok(), tok(),
                  pl.BlockSpec((td // SUBLANES, SUBLANES, D_MODEL), lambda i, zb: (i, 0, 0))],
        out_specs=pl.BlockSpec(memory_space=pl.ANY),
        scratch_shapes=[pltpu.VMEM((MOE_BM, D_MODEL), F32),
                        pltpu.SemaphoreType.DMA(()), pltpu.SemaphoreType.DMA(())],
    )
    return pl.pallas_call(
        _dispatch_kernel,
        grid_spec=grid_spec,
        out_shape=jax.ShapeDtypeStruct((n_rows, D_MODEL), F32),
        compiler_params=_params("arbitrary"),
    )(zb, p1.reshape(s // td, 1, td), p2.reshape(s // td, 1, td), u.reshape(-1, SUBLANES, D_MODEL))


def _expert_kernel(be_ref, bv_ref, bx_ref, x_ref, wg_ref, wu_ref, wd_ref, y_ref, xb_ref, acc_ref):
    b = pl.program_id(0)
    f = pl.program_id(1)
    nf = pl.num_programs(1)
    valid = bv_ref[b] != 0

    @pl.when(jnp.logical_and(valid, f == 0))
    def _():
        xb_ref[...] = x_ref[...].astype(BF16)

    @pl.when(valid)
    def _():
        x = xb_ref[...]
        g = _dot(x, wg_ref[...])
        hcol = (g * jax.nn.sigmoid(g) * _dot(x, wu_ref[...])).astype(BF16)
        part = _dot(hcol, wd_ref[...])

        @pl.when(f == 0)
        def _():
            acc_ref[...] = part

        @pl.when(jnp.logical_and(f != 0, f != nf - 1))
        def _():
            acc_ref[...] += part

        @pl.when(f == nf - 1)
        def _():
            y_ref[...] = acc_ref[...] + part

    @pl.when(jnp.logical_and(jnp.logical_not(valid), f == nf - 1))
    def _():
        y_ref[...] = jnp.zeros(y_ref.shape, F32)


def _experts(xs, wg, wu, wd, li, be, bv, bx):
    n_rows = xs.shape[0]
    bm, tf = MOE_BM, MOE_TF
    nf = D_FF_EXPERT // tf
    assert nf >= 2

    def fidx(b, f, bv):
        return jnp.where(bv[b] != 0, f, nf - 1)

    grid_spec = pltpu.PrefetchScalarGridSpec(
        num_scalar_prefetch=3,
        grid=(n_rows // bm, nf),
        in_specs=[pl.BlockSpec((bm, D_MODEL), lambda b, f, be, bv, bx: (bx[b], 0)),
                  pl.BlockSpec((None, None, D_MODEL, tf),
                               lambda b, f, be, bv, bx: (li, be[b], 0, fidx(b, f, bv))),
                  pl.BlockSpec((None, None, D_MODEL, tf),
                               lambda b, f, be, bv, bx: (li, be[b], 0, fidx(b, f, bv))),
                  pl.BlockSpec((None, None, tf, D_MODEL),
                               lambda b, f, be, bv, bx: (li, be[b], fidx(b, f, bv), 0))],
        out_specs=pl.BlockSpec((bm, D_MODEL), lambda b, f, be, bv, bx: (b, 0)),
        scratch_shapes=[pltpu.VMEM((bm, D_MODEL), BF16),
                        pltpu.VMEM((bm, D_MODEL), F32)],
    )
    return pl.pallas_call(
        _expert_kernel,
        grid_spec=grid_spec,
        out_shape=jax.ShapeDtypeStruct((n_rows, D_MODEL), F32),
        compiler_params=_params("arbitrary", "arbitrary"),
    )(be, bv, bx, xs, wg, wu, wd)


def _combine_kernel(p1_ref, p2_ref, p1n_ref, p2n_ref, x_ref, route_ref, gn_ref, ys_hbm, out_ref,
                    yg_ref, sem, *, final_norm):
    i = pl.program_id(0)
    tc = x_ref.shape[0]
    slot = lax.rem(i, 2)

    def gather(pa_ref, pb_ref, s_, op):
        def rows(k, j):
            r = k * SUBLANES + j
            op(pltpu.make_async_copy(ys_hbm.at[pl.ds(pa_ref[0, r], 1), :],
                                     yg_ref.at[s_, 0, k, pl.ds(j, 1), :], sem.at[s_]), 0)
            op(pltpu.make_async_copy(ys_hbm.at[pl.ds(pb_ref[0, r], 1), :],
                                     yg_ref.at[s_, 1, k, pl.ds(j, 1), :], sem.at[s_]), 1)
        _for_rows(tc, rows)

    start = lambda c, queue: c.start(priority=queue)
    wait = lambda c, queue: c.wait()

    @pl.when(i == 0)
    def _():
        gather(p1_ref, p2_ref, 0, start)

    gather(p1_ref, p2_ref, slot, wait)

    @pl.when(i + 1 < pl.num_programs(0))
    def _():
        gather(p1n_ref, p2n_ref, 1 - slot, start)

    y1 = yg_ref[slot, 0].reshape(tc, D_MODEL)
    y2 = yg_ref[slot, 1].reshape(tc, D_MODEL)
    out = x_ref[...] + route_ref[:, 2:3] * y1 + route_ref[:, 3:4] * y2
    out_ref[...] = _rms(out, gn_ref[...]) if final_norm else out


def _combine(x1, ys, p1, p2, route, gn, final_norm):
    s = x1.shape[0]
    tc = MOE_TC
    nt = s // tc
    row = lambda w: pl.BlockSpec((tc, w), lambda i: (i, 0))
    cur = lambda: pl.BlockSpec((None, 1, tc), lambda i: (i, 0, 0), memory_space=pltpu.SMEM)
    nxt = lambda: pl.BlockSpec((None, 1, tc), lambda i: (jnp.minimum(i + 1, nt - 1), 0, 0),
                               memory_space=pltpu.SMEM)
    p1 = p1.reshape(nt, 1, tc)
    p2 = p2.reshape(nt, 1, tc)
    return pl.pallas_call(
        functools.partial(_combine_kernel, final_norm=final_norm),
        grid=(nt,),
        in_specs=[cur(), cur(), nxt(), nxt(), row(D_MODEL), row(LANES), _const_spec(gn.shape),
                  pl.BlockSpec(memory_space=pl.ANY)],
        out_specs=row(D_MODEL),
        out_shape=jax.ShapeDtypeStruct((s, D_MODEL), F32),
        scratch_shapes=[pltpu.VMEM((2, 2, tc // SUBLANES, SUBLANES, D_MODEL), F32),
                        pltpu.SemaphoreType.DMA((2,))],
        compiler_params=_params("arbitrary"),
    )(p1, p2, p1, p2, x1, route, gn, ys)


def _moe_plan(route, counts, s):
    bm = MOE_BM
    nb = (2 * s) // bm + N_EXPERTS
    i1 = route[:, 0].astype(jnp.int32)
    i2 = route[:, 1].astype(jnp.int32)
    r1 = route[:, 4].astype(jnp.int32)
    r2 = route[:, 5].astype(jnp.int32)
    cnt = counts[0, :N_EXPERTS].astype(jnp.int32)
    padded = ((cnt + bm - 1) // bm) * bm
    ends = jnp.cumsum(padded)
    off = ends - padded
    p1 = off[i1] + r1
    p2 = off[i2] + r2
    nb_used = ends[-1] // bm

    bidx = jnp.arange(nb, dtype=jnp.int32)
    bvalid = (bidx < nb_used).astype(jnp.int32)
    bsrc = jnp.minimum(bidx, jnp.maximum(nb_used - 1, 0))
    bexp = jnp.sum((bsrc[:, None] * bm >= ends[None, :]).astype(jnp.int32), axis=1)
    bexp = jnp.minimum(bexp, N_EXPERTS - 1)
    real = jnp.clip(off[bexp] + cnt[bexp] - bidx * bm, 0, bm)
    bzero = ((real < bm) | (bvalid == 0)).astype(jnp.int32)
    return p1, p2, bexp, bvalid, bsrc, bzero


def _inproj_weight(w_in, q_scale):
    cols = [w_in[:, :OFF_QD], jnp.zeros((D_MODEL, LANES - 2 * MLSTM_HEADS), w_in.dtype)]
    for g in range(len(DIL_PATTERNS)):
        sl = lambda off: w_in[:, off + g * DIL_GW: off + (g + 1) * DIL_GW]
        cols += [sl(OFF_KD), sl(OFF_VD), sl(OFF_QD) * q_scale]
    cols.append(w_in[:, OFF_QX:])
    return jnp.concatenate(cols, axis=-1).astype(BF16)


def kernel(x, mem, norm_mix, w_in, conv_qk, b_gate_if, mlstm_norm, norm_mem, w_mem_kv, w_br_m, w_br_d,
           w_br_x, w_out, norm_ffn, ffn_w_gate, ffn_w_up, ffn_w_down, moe_router, moe_w_gate, moe_w_up,
           moe_w_down, norm_final):
    s = x.shape[1]
    xs = x.reshape(s, D_MODEL)
    mem2 = mem.reshape(N_MEM, D_MODEL)
    row = lambda a: a.reshape(1, -1)
    q_scale = DIL_DH ** -0.5
    moe_w = (moe_w_gate.astype(BF16), moe_w_up.astype(BF16), moe_w_down.astype(BF16))

    for layer in range(DEPTH):
        wl = w_in[layer]
        g_mix = row(norm_mix[layer])
        km, vm = _memkv(mem2, row(norm_mem[layer]), w_mem_kv[layer].astype(BF16))

        bif = jnp.pad(b_gate_if[layer], (0, LANES - 2 * MLSTM_HEADS)).reshape(1, LANES)
        (q_m, k_m, v_m, o_m, ifc, ifr, d0, d1, d2, h_x, gates) = _inproj(
            xs, g_mix, _inproj_weight(wl, q_scale), bif, conv_qk[layer], km, vm)

        h_m = _mlstm(q_m, k_m, v_m, o_m, ifc, ifr, row(mlstm_norm[layer]))

        ods, lses = [], []
        for qkv, (_, dil) in zip((d0, d1, d2), DIL_PATTERNS):
            o_g, lse_g = _dil_attn(qkv, dil)
            ods.append(o_g)
            lses.append(lse_g)

        merge_w = (w_br_m[layer].astype(BF16), w_br_d[layer].astype(BF16), w_br_x[layer].astype(BF16),
                   w_out[layer].astype(BF16), row(norm_ffn[layer]))
        if layer % 2 == 0:
            li = layer // 2
            dense_w = (ffn_w_gate[li].astype(BF16), ffn_w_up[li].astype(BF16), ffn_w_down[li].astype(BF16))
            xs = _merge(xs, h_m, h_x, gates, ods, lses, *merge_w, dense_w=dense_w)
        else:
            li = layer // 2
            wr = jnp.pad(moe_router[li], ((0, 0), (0, LANES - N_EXPERTS)))
            x1, u, route, counts = _merge(xs, h_m, h_x, gates, ods, lses, *merge_w, w_router=wr)
            p1, p2, bexp, bvalid, bsrc, bzero = _moe_plan(route, counts, s)
            rows = _dispatch(u, p1, p2, bzero, bexp.shape[0] * MOE_BM)
            y = _experts(rows, *moe_w, li, bexp, bvalid, bsrc)
            xs = _combine(x1, y, p1, p2, route, row(norm_final), final_norm=layer == DEPTH - 1)
    return xs.reshape(x.shape)
```

```python
import functools

import jax
import jax.numpy as jnp
from jax import lax
from jax.experimental import pallas as pl
from jax.experimental.pallas import tpu as pltpu

F32 = jnp.float32
BF16 = jnp.bfloat16

EPS = 1e-6
D_MODEL = 1024
DEPTH = 4
N_MEM = 256
MLSTM_HEADS = 4
MLSTM_DH = 128
MLSTM_W = MLSTM_HEADS * MLSTM_DH
MLSTM_CHUNK = 128
CONV_W = 4
M_INIT = -1e30
DIL_PATTERNS = ((128, 1), (512, 4), (2048, 16))
DIL_HEADS = 4
DIL_DH = 64
DIL_GW = DIL_HEADS * DIL_DH
DIL_W = 3 * DIL_GW
Q_BLOCK = 128
MEM_HEADS = 4
MEM_DH = 128
MEM_W = MEM_HEADS * MEM_DH
D_FF = 2816
N_EXPERTS = 8
D_FF_EXPERT = 3584

OFF_IF = 4 * MLSTM_W
OFF_QD = OFF_IF + 2 * MLSTM_HEADS
OFF_KD = OFF_QD + DIL_W
OFF_VD = OFF_KD + DIL_W
OFF_QX = OFF_VD + DIL_W
OFF_GATE = OFF_QX + MEM_W
IN_COLS = OFF_GATE + 3 * D_MODEL

LANES = 128
SUBLANES = 8
NEG = -1e30
VMEM_LIMIT = 56 * 1024 * 1024

DIL_SLABS = 3 * DIL_GW // LANES

WCOL_IF = 4 * MLSTM_W
WCOL_DIL = WCOL_IF + LANES
WCOL_QX = WCOL_DIL + 3 * DIL_W
WCOL_GATE = WCOL_QX + MEM_W
WCOL_END = WCOL_GATE + 3 * D_MODEL
MLSTM_CHUNKS_PER_STEP = 1
ATT_UNROLL = 4
ATT_SUPER = 2048
TOK_TILE = 512
FF_CHUNK = 256
MOE_BM = 512
MOE_TD = 512
MOE_TC = 256
MOE_TF = 1792

NT_DIMS = (((1,), (1,)), ((), ()))
TN_DIMS = (((0,), (0,)), ((), ()))


def _params(*sem):
    return pltpu.CompilerParams(dimension_semantics=sem, vmem_limit_bytes=VMEM_LIMIT)


def _dot(a, b):
    return jnp.dot(a, b, preferred_element_type=F32)


def _dot_nt(a, b):
    return lax.dot_general(a, b, NT_DIMS, preferred_element_type=F32)


def _rms(x, g):
    return x * lax.rsqrt(jnp.mean(x * x, axis=-1, keepdims=True) + EPS) * g


def _split3(x):
    hi = x.astype(BF16)
    r1 = x - hi.astype(F32)
    mid = r1.astype(BF16)
    lo = (r1 - mid.astype(F32)).astype(BF16)
    return hi, mid, lo


def _const_spec(shape):
    nd = len(shape)
    return pl.BlockSpec(shape, lambda *_: (0,) * nd, pipeline_mode=pl.Buffered(1))


def _memkv_kernel(mem_ref, g_ref, w_ref, k_ref, v_ref):
    u = _rms(mem_ref[...], g_ref[...]).astype(BF16)
    kv = _dot(u, w_ref[...])
    k_ref[...] = kv[:, :MEM_W].astype(BF16)
    v_ref[...] = kv[:, MEM_W:].astype(BF16)


def _memkv(mem, g, w_kv):
    return pl.pallas_call(
        _memkv_kernel,
        out_shape=(jax.ShapeDtypeStruct((N_MEM, MEM_W), BF16),) * 2,
        compiler_params=pltpu.CompilerParams(vmem_limit_bytes=VMEM_LIMIT),
    )(mem, g, w_kv)


def _inproj_kernel(x_ref, g_ref, w_ref, bif_ref, cw_ref, km_ref, vm_ref,
                   q_out, k_out, v_out, o_out, if_out, ift_out, d0_out, d1_out, d2_out,
                   hx_out, gate_out, conv_buf):
    tm = x_ref.shape[0]
    u = _rms(x_ref[...], g_ref[...]).astype(BF16)

    @pl.when(pl.program_id(0) == 0)
    def _():
        conv_buf[0:8, :] = jnp.zeros((8, 2 * MLSTM_W), F32)

    conv_buf[8:tm + 8, :] = _dot(u, w_ref[:, 0:2 * MLSTM_W])
    acc = cw_ref[0:1, :] * conv_buf[pl.ds(8 - (CONV_W - 1), tm), :]
    for j in range(1, CONV_W):
        acc = acc + cw_ref[j:j + 1, :] * conv_buf[pl.ds(8 - (CONV_W - 1) + j, tm), :]
    conv_buf[0:8, :] = conv_buf[tm:tm + 8, :]
    qk = acc * jax.nn.sigmoid(acc)
    q_out[...] = qk[:, :MLSTM_W].astype(BF16)
    k_out[...] = (qk[:, MLSTM_W:] * (MLSTM_DH ** -0.5)).astype(BF16)

    v_out[...] = _dot(u, w_ref[:, 2 * MLSTM_W:3 * MLSTM_W]).astype(BF16)
    o_out[...] = jax.nn.sigmoid(_dot(u, w_ref[:, 3 * MLSTM_W:4 * MLSTM_W])).astype(BF16)

    if_pre = _dot(u, w_ref[:, WCOL_IF:WCOL_DIL]) + bif_ref[...]
    if_out[...] = if_pre
    ift_out[...] = if_pre.T[0:ift_out.shape[0], :]

    for gi, d_out in enumerate((d0_out, d1_out, d2_out)):
        d = _dot(u, w_ref[:, WCOL_DIL + gi * 3 * DIL_GW:WCOL_DIL + (gi + 1) * 3 * DIL_GW])
        for j in range(DIL_SLABS):
            d_out[j] = d[:, j * LANES:(j + 1) * LANES]

    qx = (_dot(u, w_ref[:, WCOL_QX:WCOL_GATE]) * (MEM_DH ** -0.5)).astype(BF16)
    outs = []
    for h in range(MEM_HEADS):
        sl = slice(h * MEM_DH, (h + 1) * MEM_DH)
        s = _dot_nt(qx[:, sl], km_ref[:, sl])
        p = jnp.exp(s - jnp.max(s, axis=-1, keepdims=True))
        den = jnp.sum(p, axis=-1, keepdims=True)
        outs.append(_dot(p.astype(BF16), vm_ref[:, sl]) / den)
    hx_out[...] = jnp.concatenate(outs, axis=-1).astype(BF16)

    gate_out[...] = jax.nn.sigmoid(_dot(u, w_ref[:, WCOL_GATE:WCOL_END])).astype(BF16)


def _inproj(x, g, w, bif, cw, km, vm):
    s = x.shape[0]
    tm = TOK_TILE
    row = lambda w: pl.BlockSpec((tm, w), lambda i: (i, 0))
    out_shape = (
        jax.ShapeDtypeStruct((s, MLSTM_W), BF16),
        jax.ShapeDtypeStruct((s, MLSTM_W), BF16),
        jax.ShapeDtypeStruct((s, MLSTM_W), BF16),
        jax.ShapeDtypeStruct((s, MLSTM_W), BF16),
        jax.ShapeDtypeStruct((s, LANES), F32),
        jax.ShapeDtypeStruct((8, s), F32),
        jax.ShapeDtypeStruct((DIL_SLABS, s, LANES), F32),
        jax.ShapeDtypeStruct((DIL_SLABS, s, LANES), F32),
        jax.ShapeDtypeStruct((DIL_SLABS, s, LANES), F32),
        jax.ShapeDtypeStruct((s, MEM_W), BF16),
        jax.ShapeDtypeStruct((s, 3 * D_MODEL), BF16),
    )
    slab = pl.BlockSpec((DIL_SLABS, tm, LANES), lambda i: (0, i, 0))
    out_specs = (row(MLSTM_W), row(MLSTM_W), row(MLSTM_W), row(MLSTM_W), row(LANES),
                 pl.BlockSpec((8, tm), lambda i: (0, i)),
                 slab, slab, slab, row(MEM_W), row(3 * D_MODEL))
    in_specs = [row(D_MODEL)] + [_const_spec(a.shape) for a in (g, w, bif, cw, km, vm)]
    return pl.pallas_call(
        _inproj_kernel,
        grid=(s // tm,),
        in_specs=in_specs,
        out_specs=out_specs,
        out_shape=out_shape,
        scratch_shapes=[pltpu.VMEM((tm + 8, 2 * MLSTM_W), F32)],
        compiler_params=_params("arbitrary"),
    )(x, g, w, bif, cw, km, vm)


def _log_sigmoid(x):
    return jnp.minimum(x, 0.0) - jnp.log(1.0 + jnp.exp(-jnp.abs(x)))


def _mlstm_kernel(q_ref, k_ref, v_ref, o_ref, ifc_ref, ifr_ref, g_ref, out_ref, ct_ref, m_ref):
    L = MLSTM_CHUNK
    H = MLSTM_HEADS

    @pl.when(pl.program_id(0) == 0)
    def _():
        ct_ref[...] = jnp.zeros(ct_ref.shape, F32)
        m_ref[...] = jnp.full(m_ref.shape, M_INIT, F32)

    row = lax.broadcasted_iota(jnp.int32, (L, L), 0)
    col = lax.broadcasted_iota(jnp.int32, (L, L), 1)
    causal = col <= row
    tril = jnp.where(causal, 1.0, 0.0).astype(BF16)
    triu = jnp.where(row <= col, 1.0, 0.0).astype(BF16)
    ones_col = jnp.where(col == 0, 1.0, 0.0).astype(BF16)

    for ci in range(q_ref.shape[0] // L):
        rows = slice(ci * L, (ci + 1) * L)
        ifc = ifc_ref[rows, :]
        ifr = ifr_ref[:, rows]
        cum_c = sum(_dot(tril, p) for p in _split3(_log_sigmoid(ifc)))
        cum_r = sum(_dot(p, triu) for p in _split3(_log_sigmoid(ifr)))

        for h in range(H):
            sl = slice(h * MLSTM_DH, (h + 1) * MLSTM_DH)
            i_c = ifc[:, h:h + 1]
            i_r = ifr[h:h + 1, :]
            cc = cum_c[:, H + h:H + h + 1]
            cr = cum_r[H + h:H + h + 1, :]
            total = cr[:, L - 1:L]
            m_prev = m_ref[h:h + 1, 0:1]

            dm = jnp.where(causal, cc - cr + i_r, -jnp.inf)
            inter = cc + m_prev
            m_row = jnp.maximum(jnp.max(dm, axis=-1, keepdims=True), inter)
            w_intra = jnp.exp(dm - m_row)
            w_inter = jnp.exp(inter - m_row)

            qh = q_ref[rows, sl]
            kh = k_ref[rows, sl]
            vaug = jnp.concatenate([v_ref[rows, sl], ones_col], axis=-1)
            s_mat = _dot_nt(qh, kh) * w_intra
            tot = _dot(s_mat.astype(BF16), vaug) + w_inter * _dot(qh, ct_ref[h].astype(BF16))
            den = tot[:, MLSTM_DH:MLSTM_DH + 1]
            h_out = tot[:, :MLSTM_DH] / jnp.maximum(jnp.abs(den), jnp.exp(-m_row))

            g_end = total - cc + i_c
            m_new = jnp.maximum(total + m_prev, jnp.max(g_end, axis=0, keepdims=True))
            w_end = jnp.exp(g_end - m_new)
            decay = jnp.exp(total + m_prev - m_new)
            vw = (vaug.astype(F32) * w_end).astype(BF16)
            ct_ref[h] = decay * ct_ref[h] + lax.dot_general(kh, vw, TN_DIMS, preferred_element_type=F32)
            m_ref[h:h + 1, :] = jnp.broadcast_to(m_new, (1, LANES))

            mu = jnp.mean(h_out, axis=-1, keepdims=True)
            cen = h_out - mu
            var = jnp.mean(cen * cen, axis=-1, keepdims=True)
            y = cen * lax.rsqrt(var + EPS) * g_ref[:, sl] * o_ref[rows, sl].astype(F32)
            out_ref[rows, sl] = y.astype(BF16)


def _mlstm(q, k, v, o, ifc, ifr, g):
    s = q.shape[0]
    L = MLSTM_CHUNK * MLSTM_CHUNKS_PER_STEP
    row = pl.BlockSpec((L, MLSTM_W), lambda c: (c, 0))
    return pl.pallas_call(
        _mlstm_kernel,
        grid=(s // L,),
        in_specs=[row, row, row, row,
                  pl.BlockSpec((L, LANES), lambda c: (c, 0)),
                  pl.BlockSpec((8, L), lambda c: (0, c)),
                  _const_spec(g.shape)],
        out_specs=row,
        out_shape=jax.ShapeDtypeStruct((s, MLSTM_W), BF16),
        scratch_shapes=[pltpu.VMEM((MLSTM_HEADS, MLSTM_DH, 2 * MLSTM_DH), F32),
                        pltpu.VMEM((8, LANES), F32)],
        compiler_params=_params("arbitrary"),
    )(q, k, v, o, ifc, ifr, g)


def _dil_attn_kernel(kv_ref, q_ref, kvp_ref, o_ref, lse_ref, *, dil):
    B = Q_BLOCK
    span = B * dil
    n_sub = ATT_SUPER // span
    row = lax.broadcasted_iota(jnp.int32, (B, 2 * B), 0)
    col = lax.broadcasted_iota(jnp.int32, (B, 2 * B), 1)
    band = jnp.where(col >= row, jnp.where(col <= row + B, 0.0, NEG), NEG)
    first = jnp.where(pl.program_id(0) == 0, 1.0, 0.0)
    band_first = band + first * jnp.where(col < B, NEG, 0.0)
    lane = lax.broadcasted_iota(jnp.int32, (B, LANES), 1)
    lo = lane < DIL_DH
    hi = lane >= DIL_DH

    def rows(start):
        return pl.ds(start, B, stride=dil) if dil > 1 else pl.ds(start, B)

    def unit(cur_start, prev_ref, prev_start, bias):
        for half in range(2):
            q2 = q_ref[half, rows(cur_start), :]
            k2 = jnp.concatenate([prev_ref[half, rows(prev_start), :],
                                  kv_ref[half, rows(cur_start), :]], axis=0).astype(BF16)
            v2 = jnp.concatenate([prev_ref[2 + half, rows(prev_start), :],
                                  kv_ref[2 + half, rows(cur_start), :]], axis=0).astype(BF16)
            res = []
            for keep in (lo, hi):
                qm = jnp.where(keep, q2, 0.0).astype(BF16)
                s = _dot_nt(qm, k2) + bias
                mx = jnp.max(s, axis=-1, keepdims=True)
                p = jnp.exp(s - mx)
                den = jnp.sum(p, axis=-1, keepdims=True)
                res.append((_dot(p.astype(BF16), v2) / den, mx + jnp.log(den)))
            o_ref[half, rows(cur_start), :] = jnp.where(lo, res[0][0], res[1][0])
            lse_ref[half, rows(cur_start), :] = jnp.where(lo, res[0][1], res[1][1])

    def per_residue(r, carry):
        unit(r, kvp_ref, r, band_first)

        def per_sub(j, c):
            unit(j * span + r, kv_ref, (j - 1) * span + r, band)
            return c

        if 1 < n_sub <= ATT_UNROLL:
            for j in range(1, n_sub):
                per_sub(j, 0)
        elif n_sub > 1:
            lax.fori_loop(1, n_sub, per_sub, 0, unroll=ATT_UNROLL)
        return carry

    if dil >= ATT_UNROLL:
        lax.fori_loop(0, dil, per_residue, 0, unroll=max(1, ATT_UNROLL // n_sub))
    else:
        for r in range(dil):
            per_residue(r, 0)


def _dil_attn(qkv, dil):
    s = qkv.shape[1]
    span = Q_BLOCK * dil
    n_prev = ATT_SUPER // span
    blk = lambda n: pl.BlockSpec((n, ATT_SUPER, LANES), lambda i: (0, i, 0))
    return pl.pallas_call(
        functools.partial(_dil_attn_kernel, dil=dil),
        grid=(s // ATT_SUPER,),
        in_specs=[blk(4),
                  pl.BlockSpec((2, ATT_SUPER, LANES), lambda i: (2, i, 0)),
                  pl.BlockSpec((4, span, LANES), lambda i: (0, jnp.maximum(i * n_prev - 1, 0), 0))],
        out_specs=(blk(2), blk(2)),
        out_shape=(jax.ShapeDtypeStruct((2, s, LANES), F32),) * 2,
        compiler_params=_params("arbitrary"),
    )(qkv, qkv, qkv)


def _merge_core(x_ref, hm_ref, hx_ref, gate_ref, od_refs, lse_refs, wm_ref, wdd_ref, wx_ref, wo_ref):
    wide = lambda r: jnp.concatenate([r[0], r[1]], axis=-1)
    lses = [wide(r) for r in lse_refs]
    mx = jnp.maximum(jnp.maximum(lses[0], lses[1]), lses[2])
    es = [jnp.exp(l - mx) for l in lses]
    den = es[0] + es[1] + es[2]
    hd = (es[0] * wide(od_refs[0]) + es[1] * wide(od_refs[1]) + es[2] * wide(od_refs[2])) / den
    d = D_MODEL
    merged = (gate_ref[:, 0:d].astype(F32) * _dot(hm_ref[...], wm_ref[...])
              + gate_ref[:, d:2 * d].astype(F32) * _dot(hd.astype(BF16), wdd_ref[...])
              + gate_ref[:, 2 * d:3 * d].astype(F32) * _dot(hx_ref[...], wx_ref[...]))
    return x_ref[...] + _dot(merged.astype(BF16), wo_ref[...])


def _merge_dense_kernel(x_ref, hm_ref, hx_ref, gate_ref, o0, o1, o2, l0, l1, l2,
                        wm_ref, wdd_ref, wx_ref, wo_ref, gf_ref, wg_ref, wu_ref, wdn_ref,
                        out_ref, acc_ref):
    x1 = _merge_core(x_ref, hm_ref, hx_ref, gate_ref, (o0, o1, o2), (l0, l1, l2),
                     wm_ref, wdd_ref, wx_ref, wo_ref)
    u = _rms(x1, gf_ref[...]).astype(BF16)
    acc_ref[...] = x1

    def body(c, carry):
        cols = pl.ds(pl.multiple_of(c * FF_CHUNK, FF_CHUNK), FF_CHUNK)
        g = _dot(u, wg_ref[:, cols])
        hcol = (g * jax.nn.sigmoid(g) * _dot(u, wu_ref[:, cols])).astype(BF16)
        acc_ref[...] += _dot(hcol, wdn_ref[cols, :])
        return carry

    lax.fori_loop(0, D_FF // FF_CHUNK, body, 0)
    out_ref[...] = acc_ref[...]


def _merge_moe_kernel(x_ref, hm_ref, hx_ref, gate_ref, o0, o1, o2, l0, l1, l2,
                      wm_ref, wdd_ref, wx_ref, wo_ref, gf_ref, wr_ref,
                      x1_out, u_out, route_out, cnt_out, carry_ref):
    tm = x_ref.shape[0]

    @pl.when(pl.program_id(0) == 0)
    def _():
        carry_ref[...] = jnp.zeros(carry_ref.shape, F32)

    x1 = _merge_core(x_ref, hm_ref, hx_ref, gate_ref, (o0, o1, o2), (l0, l1, l2),
                     wm_ref, wdd_ref, wx_ref, wo_ref)
    x1_out[...] = x1
    uf = _rms(x1, gf_ref[...])
    u_out[...] = uf

    uh, um, ul = _split3(uf)
    wh, wmid, wl = _split3(wr_ref[...])
    logits = (_dot(uh, wh) + (_dot(uh, wmid) + _dot(um, wh))
              + (_dot(uh, wl) + _dot(um, wmid) + _dot(ul, wh)))
    lane = lax.broadcasted_iota(jnp.int32, (tm, LANES), 1).astype(F32)
    valid = lane < N_EXPERTS
    lg = jnp.where(valid, logits, NEG)
    ex = jnp.exp(lg - jnp.max(lg, axis=-1, keepdims=True))
    probs = jnp.where(valid, ex / jnp.sum(ex, axis=-1, keepdims=True), -1.0)
    p1 = jnp.max(probs, axis=-1, keepdims=True)
    i1 = jnp.min(jnp.where(probs == p1, lane, float(LANES)), axis=-1, keepdims=True)
    rest = jnp.where(lane == i1, -1.0, probs)
    p2 = jnp.max(rest, axis=-1, keepdims=True)
    i2 = jnp.min(jnp.where(rest == p2, lane, float(LANES)), axis=-1, keepdims=True)
    g1 = p1 / (p1 + p2)
    g2 = p2 / (p1 + p2)
    sel = jnp.where(lane == i1, 1.0, jnp.where(lane == i2, 1.0, 0.0))
    row = lax.broadcasted_iota(jnp.int32, (tm, tm), 0)
    col = lax.broadcasted_iota(jnp.int32, (tm, tm), 1)
    before = jnp.where(col < row, 1.0, 0.0).astype(BF16)
    ranks = _dot(before, sel.astype(BF16)) + carry_ref[0:1, :]
    r1 = jnp.sum(jnp.where(lane == i1, ranks, 0.0), axis=-1, keepdims=True)
    r2 = jnp.sum(jnp.where(lane == i2, ranks, 0.0), axis=-1, keepdims=True)
    carry_ref[...] = carry_ref[...] + jnp.sum(sel, axis=0, keepdims=True)
    cnt_out[...] = carry_ref[...]
    route = jnp.where(lane == 0, i1, jnp.where(lane == 1, i2, jnp.where(lane == 2, g1,
            jnp.where(lane == 3, g2, jnp.where(lane == 4, r1, jnp.where(lane == 5, r2, 0.0))))))
    route_out[...] = route


def _merge(x, hm, hx, gates, ods, lses, wm, wdd, wx, wo, gf, dense_w=None, w_router=None):
    s = x.shape[0]
    tm = TOK_TILE
    row = lambda w: pl.BlockSpec((tm, w), lambda i: (i, 0))
    acts = (x, hm, hx, gates) + tuple(ods) + tuple(lses)
    slab = pl.BlockSpec((2, tm, LANES), lambda i: (0, i, 0))
    act_specs = [row(D_MODEL), row(MLSTM_W), row(MEM_W), row(3 * D_MODEL)] + [slab] * 6
    if dense_w is not None:
        consts = (wm, wdd, wx, wo, gf) + tuple(dense_w)
        return pl.pallas_call(
            _merge_dense_kernel,
            grid=(s // tm,),
            in_specs=act_specs + [_const_spec(c.shape) for c in consts],
            out_specs=row(D_MODEL),
            out_shape=jax.ShapeDtypeStruct((s, D_MODEL), F32),
            scratch_shapes=[pltpu.VMEM((tm, D_MODEL), F32)],
            compiler_params=_params("arbitrary"),
        )(*acts, *consts)
    consts = (wm, wdd, wx, wo, gf, w_router)
    return pl.pallas_call(
        _merge_moe_kernel,
        grid=(s // tm,),
        in_specs=act_specs + [_const_spec(c.shape) for c in consts],
        out_specs=(row(D_MODEL), row(D_MODEL), row(LANES), pl.BlockSpec((8, LANES), lambda i: (0, 0))),
        out_shape=(jax.ShapeDtypeStruct((s, D_MODEL), F32),
                   jax.ShapeDtypeStruct((s, D_MODEL), F32),
                   jax.ShapeDtypeStruct((s, LANES), F32),
                   jax.ShapeDtypeStruct((8, LANES), F32)),
        scratch_shapes=[pltpu.VMEM((8, LANES), F32)],
        compiler_params=_params("arbitrary"),
    )(*acts, *consts)


def _for_rows(n, fn):
    def body(k, c):
        for j in range(SUBLANES):
            fn(k, j)
        return c
    lax.fori_loop(0, n // SUBLANES, body, 0)


def _dispatch_kernel(zb_ref, p1_ref, p2_ref, u_ref, wg_ref, wu_ref, wd_ref,
                     xs_hbm, wgb_ref, wub_ref, wdb_ref, zero_ref, sem, zsem):
    i = pl.program_id(0)
    td = p1_ref.shape[1]
    bm = zero_ref.shape[0]

    wgb_ref[...] = wg_ref[...].astype(BF16)
    wub_ref[...] = wu_ref[...].astype(BF16)
    wdb_ref[...] = wd_ref[...].astype(BF16)

    @pl.when(i == 0)
    def _():
        zero_ref[...] = jnp.zeros(zero_ref.shape, F32)

        def fill(op):
            def body(b, c):
                @pl.when(zb_ref[b] != 0)
                def _():
                    op(pltpu.make_async_copy(zero_ref, xs_hbm.at[pl.ds(b * bm, bm), :], zsem))
                return c
            lax.fori_loop(0, zb_ref.shape[0], body, 0)

        fill(lambda c: c.start())
        fill(lambda c: c.wait())

    def copies(k, j):
        r = k * SUBLANES + j
        src = u_ref.at[k, pl.ds(j, 1), :]
        return (pltpu.make_async_copy(src, xs_hbm.at[pl.ds(p1_ref[0, r], 1), :], sem),
                pltpu.make_async_copy(src, xs_hbm.at[pl.ds(p2_ref[0, r], 1), :], sem))

    def start(k, j):
        a, b = copies(k, j)
        a.start(priority=0)
        b.start(priority=1)

    def wait(k, j):
        a, b = copies(k, j)
        a.wait()
        b.wait()

    _for_rows(td, start)
    _for_rows(td, wait)


def _dispatch(u, p1, p2, zb, n_rows, wg, wu, wd, li):
    s = u.shape[0]
    td = MOE_TD
    steps = s // td
    assert steps % N_EXPERTS == 0
    parts = steps // N_EXPERTS
    tok = lambda: pl.BlockSpec((None, 1, td), lambda i, zb: (i, 0, 0), memory_space=pltpu.SMEM)

    def w_in_spec(w):
        rows = w.shape[2] // parts
        return pl.BlockSpec((None, None, rows, w.shape[3]), lambda i, zb: (li, i // parts, i % parts, 0))

    def w_out_spec(w):
        rows = w.shape[2] // parts
        return pl.BlockSpec((None, rows, w.shape[3]), lambda i, zb: (i // parts, i % parts, 0))

    grid_spec = pltpu.PrefetchScalarGridSpec(
        num_scalar_prefetch=1,
        grid=(steps,),
        in_specs=[tok(), tok(),
                  pl.BlockSpec((td // SUBLANES, SUBLANES, D_MODEL), lambda i, zb: (i, 0, 0)),
                  w_in_spec(wg), w_in_spec(wu), w_in_spec(wd)],
        out_specs=(pl.BlockSpec(memory_space=pl.ANY), w_out_spec(wg), w_out_spec(wu), w_out_spec(wd)),
        scratch_shapes=[pltpu.VMEM((MOE_BM, D_MODEL), F32),
                        pltpu.SemaphoreType.DMA(()), pltpu.SemaphoreType.DMA(())],
    )
    return pl.pallas_call(
        _dispatch_kernel,
        grid_spec=grid_spec,
        out_shape=(jax.ShapeDtypeStruct((n_rows, D_MODEL), F32),)
        + tuple(jax.ShapeDtypeStruct(w.shape[1:], BF16) for w in (wg, wu, wd)),
        compiler_params=_params("arbitrary"),
    )(zb, p1.reshape(steps, 1, td), p2.reshape(steps, 1, td), u.reshape(-1, SUBLANES, D_MODEL), wg, wu, wd)


def _expert_kernel(be_ref, bv_ref, bx_ref, x_ref, wg_ref, wu_ref, wd_ref, y_ref, xb_ref, acc_ref):
    b = pl.program_id(0)
    f = pl.program_id(1)
    nf = pl.num_programs(1)
    valid = bv_ref[b] != 0

    @pl.when(jnp.logical_and(valid, f == 0))
    def _():
        xb_ref[...] = x_ref[...].astype(BF16)

    @pl.when(valid)
    def _():
        x = xb_ref[...]
        g = _dot(x, wg_ref[...])
        hcol = (g * jax.nn.sigmoid(g) * _dot(x, wu_ref[...])).astype(BF16)
        part = _dot(hcol, wd_ref[...])

        @pl.when(f == 0)
        def _():
            acc_ref[...] = part

        @pl.when(jnp.logical_and(f != 0, f != nf - 1))
        def _():
            acc_ref[...] += part

        @pl.when(f == nf - 1)
        def _():
            y_ref[...] = acc_ref[...] + part

    @pl.when(jnp.logical_and(jnp.logical_not(valid), f == nf - 1))
    def _():
        y_ref[...] = jnp.zeros(y_ref.shape, F32)


def _experts(xs, wg, wu, wd, be, bv, bx):
    n_rows = xs.shape[0]
    bm, tf = MOE_BM, MOE_TF
    nf = D_FF_EXPERT // tf
    assert nf >= 2

    def fidx(b, f, bv):
        return jnp.where(bv[b] != 0, f, nf - 1)

    grid_spec = pltpu.PrefetchScalarGridSpec(
        num_scalar_prefetch=3,
        grid=(n_rows // bm, nf),
        in_specs=[pl.BlockSpec((bm, D_MODEL), lambda b, f, be, bv, bx: (bx[b], 0)),
                  pl.BlockSpec((None, D_MODEL, tf), lambda b, f, be, bv, bx: (be[b], 0, fidx(b, f, bv))),
                  pl.BlockSpec((None, D_MODEL, tf), lambda b, f, be, bv, bx: (be[b], 0, fidx(b, f, bv))),
                  pl.BlockSpec((None, tf, D_MODEL), lambda b, f, be, bv, bx: (be[b], fidx(b, f, bv), 0))],
        out_specs=pl.BlockSpec((bm, D_MODEL), lambda b, f, be, bv, bx: (b, 0)),
        scratch_shapes=[pltpu.VMEM((bm, D_MODEL), BF16),
                        pltpu.VMEM((bm, D_MODEL), F32)],
    )
    return pl.pallas_call(
        _expert_kernel,
        grid_spec=grid_spec,
        out_shape=jax.ShapeDtypeStruct((n_rows, D_MODEL), F32),
        compiler_params=_params("arbitrary", "arbitrary"),
    )(be, bv, bx, xs, wg, wu, wd)


def _combine_kernel(p1_ref, p2_ref, p1n_ref, p2n_ref, x_ref, route_ref, gn_ref, ys_hbm, out_ref,
                    yg_ref, sem, *, final_norm):
    i = pl.program_id(0)
    tc = x_ref.shape[0]
    slot = lax.rem(i, 2)

    def gather(pa_ref, pb_ref, s_, op):
        def rows(k, j):
            r = k * SUBLANES + j
            op(pltpu.make_async_copy(ys_hbm.at[pl.ds(pa_ref[0, r], 1), :],
                                     yg_ref.at[s_, 0, k, pl.ds(j, 1), :], sem.at[s_]), 0)
            op(pltpu.make_async_copy(ys_hbm.at[pl.ds(pb_ref[0, r], 1), :],
                                     yg_ref.at[s_, 1, k, pl.ds(j, 1), :], sem.at[s_]), 1)
        _for_rows(tc, rows)

    start = lambda c, queue: c.start(priority=queue)
    wait = lambda c, queue: c.wait()

    @pl.when(i == 0)
    def _():
        gather(p1_ref, p2_ref, 0, start)

    gather(p1_ref, p2_ref, slot, wait)

    @pl.when(i + 1 < pl.num_programs(0))
    def _():
        gather(p1n_ref, p2n_ref, 1 - slot, start)

    y1 = yg_ref[slot, 0].reshape(tc, D_MODEL)
    y2 = yg_ref[slot, 1].reshape(tc, D_MODEL)
    out = x_ref[...] + route_ref[:, 2:3] * y1 + route_ref[:, 3:4] * y2
    out_ref[...] = _rms(out, gn_ref[...]) if final_norm else out


def _combine(x1, ys, p1, p2, route, gn, final_norm):
    s = x1.shape[0]
    tc = MOE_TC
    nt = s // tc
    row = lambda w: pl.BlockSpec((tc, w), lambda i: (i, 0))
    cur = lambda: pl.BlockSpec((None, 1, tc), lambda i: (i, 0, 0), memory_space=pltpu.SMEM)
    nxt = lambda: pl.BlockSpec((None, 1, tc), lambda i: (jnp.minimum(i + 1, nt - 1), 0, 0),
                               memory_space=pltpu.SMEM)
    p1 = p1.reshape(nt, 1, tc)
    p2 = p2.reshape(nt, 1, tc)
    return pl.pallas_call(
        functools.partial(_combine_kernel, final_norm=final_norm),
        grid=(nt,),
        in_specs=[cur(), cur(), nxt(), nxt(), row(D_MODEL), row(LANES), _const_spec(gn.shape),
                  pl.BlockSpec(memory_space=pl.ANY)],
        out_specs=row(D_MODEL),
        out_shape=jax.ShapeDtypeStruct((s, D_MODEL), F32),
        scratch_shapes=[pltpu.VMEM((2, 2, tc // SUBLANES, SUBLANES, D_MODEL), F32),
                        pltpu.SemaphoreType.DMA((2,))],
        compiler_params=_params("arbitrary"),
    )(p1, p2, p1, p2, x1, route, gn, ys)


def _moe_plan(route, counts, s):
    bm = MOE_BM
    nb = (2 * s) // bm + N_EXPERTS
    i1 = route[:, 0].astype(jnp.int32)
    i2 = route[:, 1].astype(jnp.int32)
    r1 = route[:, 4].astype(jnp.int32)
    r2 = route[:, 5].astype(jnp.int32)
    cnt = counts[0, :N_EXPERTS].astype(jnp.int32)
    padded = ((cnt + bm - 1) // bm) * bm
    ends = jnp.cumsum(padded)
    off = ends - padded
    p1 = off[i1] + r1
    p2 = off[i2] + r2
    nb_used = ends[-1] // bm

    bidx = jnp.arange(nb, dtype=jnp.int32)
    bvalid = (bidx < nb_used).astype(jnp.int32)
    bsrc = jnp.minimum(bidx, jnp.maximum(nb_used - 1, 0))
    bexp = jnp.sum((bsrc[:, None] * bm >= ends[None, :]).astype(jnp.int32), axis=1)
    bexp = jnp.minimum(bexp, N_EXPERTS - 1)
    real = jnp.clip(off[bexp] + cnt[bexp] - bidx * bm, 0, bm)
    bzero = ((real < bm) | (bvalid == 0)).astype(jnp.int32)
    return p1, p2, bexp, bvalid, bsrc, bzero


def _inproj_weight(w_in, q_scale):
    cols = [w_in[:, :OFF_QD], jnp.zeros((D_MODEL, LANES - 2 * MLSTM_HEADS), w_in.dtype)]
    for g in range(len(DIL_PATTERNS)):
        sl = lambda off: w_in[:, off + g * DIL_GW: off + (g + 1) * DIL_GW]
        cols += [sl(OFF_KD), sl(OFF_VD), sl(OFF_QD) * q_scale]
    cols.append(w_in[:, OFF_QX:])
    return jnp.concatenate(cols, axis=-1).astype(BF16)


def kernel(x, mem, norm_mix, w_in, conv_qk, b_gate_if, mlstm_norm, norm_mem, w_mem_kv, w_br_m, w_br_d,
           w_br_x, w_out, norm_ffn, ffn_w_gate, ffn_w_up, ffn_w_down, moe_router, moe_w_gate, moe_w_up,
           moe_w_down, norm_final):
    s = x.shape[1]
    xs = x.reshape(s, D_MODEL)
    mem2 = mem.reshape(N_MEM, D_MODEL)
    row = lambda a: a.reshape(1, -1)
    q_scale = DIL_DH ** -0.5

    for layer in range(DEPTH):
        wl = w_in[layer]
        g_mix = row(norm_mix[layer])
        km, vm = _memkv(mem2, row(norm_mem[layer]), w_mem_kv[layer].astype(BF16))

        bif = jnp.pad(b_gate_if[layer], (0, LANES - 2 * MLSTM_HEADS)).reshape(1, LANES)
        (q_m, k_m, v_m, o_m, ifc, ifr, d0, d1, d2, h_x, gates) = _inproj(
            xs, g_mix, _inproj_weight(wl, q_scale), bif, conv_qk[layer], km, vm)

        h_m = _mlstm(q_m, k_m, v_m, o_m, ifc, ifr, row(mlstm_norm[layer]))

        ods, lses = [], []
        for qkv, (_, dil) in zip((d0, d1, d2), DIL_PATTERNS):
            o_g, lse_g = _dil_attn(qkv, dil)
            ods.append(o_g)
            lses.append(lse_g)

        merge_w = (w_br_m[layer].astype(BF16), w_br_d[layer].astype(BF16), w_br_x[layer].astype(BF16),
                   w_out[layer].astype(BF16), row(norm_ffn[layer]))
        if layer % 2 == 0:
            li = layer // 2
            dense_w = (ffn_w_gate[li].astype(BF16), ffn_w_up[li].astype(BF16), ffn_w_down[li].astype(BF16))
            xs = _merge(xs, h_m, h_x, gates, ods, lses, *merge_w, dense_w=dense_w)
        else:
            li = layer // 2
            wr = jnp.pad(moe_router[li], ((0, 0), (0, LANES - N_EXPERTS)))
            x1, u, route, counts = _merge(xs, h_m, h_x, gates, ods, lses, *merge_w, w_router=wr)
            p1, p2, bexp, bvalid, bsrc, bzero = _moe_plan(route, counts, s)
            rows, wg, wu, wd = _dispatch(u, p1, p2, bzero, bexp.shape[0] * MOE_BM,
                                         moe_w_gate, moe_w_up, moe_w_down, li)
            y = _experts(rows, wg, wu, wd, bexp, bvalid, bsrc)
            xs = _combine(x1, y, p1, p2, route, row(norm_final), final_norm=layer == DEPTH - 1)
    return xs.reshape(x.shape)
```

```python
import functools

import jax
import jax.numpy as jnp
from jax import lax
from jax.experimental import pallas as pl
from jax.experimental.pallas import tpu as pltpu

F32 = jnp.float32
BF16 = jnp.bfloat16

EPS = 1e-6
D_MODEL = 1024
DEPTH = 4
N_MEM = 256
MLSTM_HEADS = 4
MLSTM_DH = 128
MLSTM_W = MLSTM_HEADS * MLSTM_DH
MLSTM_CHUNK = 128
CONV_W = 4
M_INIT = -1e30
DIL_PATTERNS = ((128, 1), (512, 4), (2048, 16))
DIL_HEADS = 4
DIL_DH = 64
DIL_GW = DIL_HEADS * DIL_DH
DIL_W = 3 * DIL_GW
Q_BLOCK = 128
MEM_HEADS = 4
MEM_DH = 128
MEM_W = MEM_HEADS * MEM_DH
D_FF = 2816
N_EXPERTS = 8
D_FF_EXPERT = 3584

OFF_IF = 4 * MLSTM_W
OFF_QD = OFF_IF + 2 * MLSTM_HEADS
OFF_KD = OFF_QD + DIL_W
OFF_VD = OFF_KD + DIL_W
OFF_QX = OFF_VD + DIL_W
OFF_GATE = OFF_QX + MEM_W
IN_COLS = OFF_GATE + 3 * D_MODEL

LANES = 128
SUBLANES = 8
NEG = -1e30
VMEM_LIMIT = 56 * 1024 * 1024

DIL_SLABS = 3 * DIL_GW // LANES

WCOL_IF = 4 * MLSTM_W
WCOL_DIL = WCOL_IF + LANES
WCOL_QX = WCOL_DIL + 3 * DIL_W
WCOL_GATE = WCOL_QX + MEM_W
WCOL_END = WCOL_GATE + 3 * D_MODEL
MLSTM_CHUNKS_PER_STEP = 1
ATT_UNROLL = 8
ATT_SUPER = 2048
TOK_TILE = 512
FF_CHUNK = 256
MOE_BM = 512
MOE_TD = 512
MOE_TC = 512
MOE_TF = 1792

NT_DIMS = (((1,), (1,)), ((), ()))
TN_DIMS = (((0,), (0,)), ((), ()))


def _params(*sem):
    return pltpu.CompilerParams(dimension_semantics=sem, vmem_limit_bytes=VMEM_LIMIT)


def _dot(a, b):
    return jnp.dot(a, b, preferred_element_type=F32)


def _dot_nt(a, b):
    return lax.dot_general(a, b, NT_DIMS, preferred_element_type=F32)


def _rms(x, g):
    return x * lax.rsqrt(jnp.mean(x * x, axis=-1, keepdims=True) + EPS) * g


def _split3(x):
    hi = x.astype(BF16)
    r1 = x - hi.astype(F32)
    mid = r1.astype(BF16)
    lo = (r1 - mid.astype(F32)).astype(BF16)
    return hi, mid, lo


def _const_spec(shape):
    nd = len(shape)
    return pl.BlockSpec(shape, lambda *_: (0,) * nd, pipeline_mode=pl.Buffered(1))


def _memkv_kernel(mem_ref, g_ref, w_ref, k_ref, v_ref):
    u = _rms(mem_ref[...], g_ref[...]).astype(BF16)
    kv = _dot(u, w_ref[...])
    k_ref[...] = kv[:, :MEM_W].astype(BF16)
    v_ref[...] = kv[:, MEM_W:].astype(BF16)


def _memkv(mem, g, w_kv):
    return pl.pallas_call(
        _memkv_kernel,
        out_shape=(jax.ShapeDtypeStruct((N_MEM, MEM_W), BF16),) * 2,
        compiler_params=pltpu.CompilerParams(vmem_limit_bytes=VMEM_LIMIT),
    )(mem, g, w_kv)


def _inproj_kernel(x_ref, g_ref, w_ref, bif_ref, cw_ref, km_ref, vm_ref,
                   q_out, k_out, v_out, o_out, if_out, ift_out, d0_out, d1_out, d2_out,
                   hx_out, gate_out, conv_buf):
    tm = x_ref.shape[0]
    u = _rms(x_ref[...], g_ref[...]).astype(BF16)

    @pl.when(pl.program_id(0) == 0)
    def _():
        conv_buf[0:8, :] = jnp.zeros((8, 2 * MLSTM_W), F32)

    conv_buf[8:tm + 8, :] = _dot(u, w_ref[:, 0:2 * MLSTM_W])
    acc = cw_ref[0:1, :] * conv_buf[pl.ds(8 - (CONV_W - 1), tm), :]
    for j in range(1, CONV_W):
        acc = acc + cw_ref[j:j + 1, :] * conv_buf[pl.ds(8 - (CONV_W - 1) + j, tm), :]
    conv_buf[0:8, :] = conv_buf[tm:tm + 8, :]
    qk = acc * jax.nn.sigmoid(acc)
    q_out[...] = qk[:, :MLSTM_W].astype(BF16)
    k_out[...] = (qk[:, MLSTM_W:] * (MLSTM_DH ** -0.5)).astype(BF16)

    v_out[...] = _dot(u, w_ref[:, 2 * MLSTM_W:3 * MLSTM_W]).astype(BF16)
    o_out[...] = jax.nn.sigmoid(_dot(u, w_ref[:, 3 * MLSTM_W:4 * MLSTM_W])).astype(BF16)

    if_pre = _dot(u, w_ref[:, WCOL_IF:WCOL_DIL]) + bif_ref[...]
    if_out[...] = if_pre
    ift_out[...] = if_pre.T[0:ift_out.shape[0], :]

    for gi, d_out in enumerate((d0_out, d1_out, d2_out)):
        d = _dot(u, w_ref[:, WCOL_DIL + gi * 3 * DIL_GW:WCOL_DIL + (gi + 1) * 3 * DIL_GW])
        for j in range(DIL_SLABS):
            d_out[j] = d[:, j * LANES:(j + 1) * LANES]

    qx = (_dot(u, w_ref[:, WCOL_QX:WCOL_GATE]) * (MEM_DH ** -0.5)).astype(BF16)
    outs = []
    for h in range(MEM_HEADS):
        sl = slice(h * MEM_DH, (h + 1) * MEM_DH)
        s = _dot_nt(qx[:, sl], km_ref[:, sl])
        p = jnp.exp(s - jnp.max(s, axis=-1, keepdims=True))
        den = jnp.sum(p, axis=-1, keepdims=True)
        outs.append(_dot(p.astype(BF16), vm_ref[:, sl]) / den)
    hx_out[...] = jnp.concatenate(outs, axis=-1).astype(BF16)

    gate_out[...] = jax.nn.sigmoid(_dot(u, w_ref[:, WCOL_GATE:WCOL_END])).astype(BF16)


def _inproj(x, g, w, bif, cw, km, vm):
    s = x.shape[0]
    tm = TOK_TILE
    row = lambda w: pl.BlockSpec((tm, w), lambda i: (i, 0))
    out_shape = (
        jax.ShapeDtypeStruct((s, MLSTM_W), BF16),
        jax.ShapeDtypeStruct((s, MLSTM_W), BF16),
        jax.ShapeDtypeStruct((s, MLSTM_W), BF16),
        jax.ShapeDtypeStruct((s, MLSTM_W), BF16),
        jax.ShapeDtypeStruct((s, LANES), F32),
        jax.ShapeDtypeStruct((8, s), F32),
        jax.ShapeDtypeStruct((DIL_SLABS, s, LANES), F32),
        jax.ShapeDtypeStruct((DIL_SLABS, s, LANES), F32),
        jax.ShapeDtypeStruct((DIL_SLABS, s, LANES), F32),
        jax.ShapeDtypeStruct((s, MEM_W), BF16),
        jax.ShapeDtypeStruct((s, 3 * D_MODEL), BF16),
    )
    slab = pl.BlockSpec((DIL_SLABS, tm, LANES), lambda i: (0, i, 0))
    out_specs = (row(MLSTM_W), row(MLSTM_W), row(MLSTM_W), row(MLSTM_W), row(LANES),
                 pl.BlockSpec((8, tm), lambda i: (0, i)),
                 slab, slab, slab, row(MEM_W), row(3 * D_MODEL))
    in_specs = [row(D_MODEL)] + [_const_spec(a.shape) for a in (g, w, bif, cw, km, vm)]
    return pl.pallas_call(
        _inproj_kernel,
        grid=(s // tm,),
        in_specs=in_specs,
        out_specs=out_specs,
        out_shape=out_shape,
        scratch_shapes=[pltpu.VMEM((tm + 8, 2 * MLSTM_W), F32)],
        compiler_params=_params("arbitrary"),
    )(x, g, w, bif, cw, km, vm)


def _log_sigmoid(x):
    return jnp.minimum(x, 0.0) - jnp.log(1.0 + jnp.exp(-jnp.abs(x)))


def _mlstm_kernel(q_ref, k_ref, v_ref, o_ref, ifc_ref, ifr_ref, g_ref, out_ref, ct_ref, m_ref):
    L = MLSTM_CHUNK
    H = MLSTM_HEADS

    @pl.when(pl.program_id(0) == 0)
    def _():
        ct_ref[...] = jnp.zeros(ct_ref.shape, F32)
        m_ref[...] = jnp.full(m_ref.shape, M_INIT, F32)

    row = lax.broadcasted_iota(jnp.int32, (L, L), 0)
    col = lax.broadcasted_iota(jnp.int32, (L, L), 1)
    causal = col <= row
    tril = jnp.where(causal, 1.0, 0.0).astype(BF16)
    triu = jnp.where(row <= col, 1.0, 0.0).astype(BF16)
    ones_col = jnp.where(col == 0, 1.0, 0.0).astype(BF16)

    for ci in range(q_ref.shape[0] // L):
        rows = slice(ci * L, (ci + 1) * L)
        ifc = ifc_ref[rows, :]
        ifr = ifr_ref[:, rows]
        cum_c = sum(_dot(tril, p) for p in _split3(_log_sigmoid(ifc)))
        cum_r = sum(_dot(p, triu) for p in _split3(_log_sigmoid(ifr)))

        for h in range(H):
            sl = slice(h * MLSTM_DH, (h + 1) * MLSTM_DH)
            i_c = ifc[:, h:h + 1]
            i_r = ifr[h:h + 1, :]
            cc = cum_c[:, H + h:H + h + 1]
            cr = cum_r[H + h:H + h + 1, :]
            total = cr[:, L - 1:L]
            m_prev = m_ref[h:h + 1, 0:1]

            dm = jnp.where(causal, cc - cr + i_r, -jnp.inf)
            inter = cc + m_prev
            m_row = jnp.maximum(jnp.max(dm, axis=-1, keepdims=True), inter)
            w_intra = jnp.exp(dm - m_row)
            w_inter = jnp.exp(inter - m_row)

            qh = q_ref[rows, sl]
            kh = k_ref[rows, sl]
            vaug = jnp.concatenate([v_ref[rows, sl], ones_col], axis=-1)
            s_mat = _dot_nt(qh, kh) * w_intra
            tot = _dot(s_mat.astype(BF16), vaug) + w_inter * _dot(qh, ct_ref[h].astype(BF16))
            den = tot[:, MLSTM_DH:MLSTM_DH + 1]
            h_out = tot[:, :MLSTM_DH] / jnp.maximum(jnp.abs(den), jnp.exp(-m_row))

            g_end = total - cc + i_c
            m_new = jnp.maximum(total + m_prev, jnp.max(g_end, axis=0, keepdims=True))
            w_end = jnp.exp(g_end - m_new)
            decay = jnp.exp(total + m_prev - m_new)
            vw = (vaug.astype(F32) * w_end).astype(BF16)
            ct_ref[h] = decay * ct_ref[h] + lax.dot_general(kh, vw, TN_DIMS, preferred_element_type=F32)
            m_ref[h:h + 1, :] = jnp.broadcast_to(m_new, (1, LANES))

            mu = jnp.mean(h_out, axis=-1, keepdims=True)
            cen = h_out - mu
            var = jnp.mean(cen * cen, axis=-1, keepdims=True)
            y = cen * lax.rsqrt(var + EPS) * g_ref[:, sl] * o_ref[rows, sl].astype(F32)
            out_ref[rows, sl] = y.astype(BF16)


def _mlstm(q, k, v, o, ifc, ifr, g):
    s = q.shape[0]
    L = MLSTM_CHUNK * MLSTM_CHUNKS_PER_STEP
    row = pl.BlockSpec((L, MLSTM_W), lambda c: (c, 0))
    return pl.pallas_call(
        _mlstm_kernel,
        grid=(s // L,),
        in_specs=[row, row, row, row,
                  pl.BlockSpec((L, LANES), lambda c: (c, 0)),
                  pl.BlockSpec((8, L), lambda c: (0, c)),
                  _const_spec(g.shape)],
        out_specs=row,
        out_shape=jax.ShapeDtypeStruct((s, MLSTM_W), BF16),
        scratch_shapes=[pltpu.VMEM((MLSTM_HEADS, MLSTM_DH, 2 * MLSTM_DH), F32),
                        pltpu.VMEM((8, LANES), F32)],
        compiler_params=_params("arbitrary"),
    )(q, k, v, o, ifc, ifr, g)


def _dil_attn_kernel(kv_ref, q_ref, kvp_ref, o_ref, lse_ref, *, dil):
    B = Q_BLOCK
    span = B * dil
    n_sub = ATT_SUPER // span
    row = lax.broadcasted_iota(jnp.int32, (B, 2 * B), 0)
    col = lax.broadcasted_iota(jnp.int32, (B, 2 * B), 1)
    band = jnp.where(col >= row, jnp.where(col <= row + B, 0.0, NEG), NEG)
    first = jnp.where(pl.program_id(0) == 0, 1.0, 0.0)
    band_first = band + first * jnp.where(col < B, NEG, 0.0)
    lane = lax.broadcasted_iota(jnp.int32, (B, LANES), 1)
    lo = lane < DIL_DH
    hi = lane >= DIL_DH

    def rows(start):
        return pl.ds(start, B, stride=dil) if dil > 1 else pl.ds(start, B)

    def unit(cur_start, prev_ref, prev_start, bias):
        for half in range(2):
            q2 = q_ref[half, rows(cur_start), :]
            k2 = jnp.concatenate([prev_ref[half, rows(prev_start), :],
                                  kv_ref[half, rows(cur_start), :]], axis=0).astype(BF16)
            v2 = jnp.concatenate([prev_ref[2 + half, rows(prev_start), :],
                                  kv_ref[2 + half, rows(cur_start), :]], axis=0).astype(BF16)
            res = []
            for keep in (lo, hi):
                qm = jnp.where(keep, q2, 0.0).astype(BF16)
                s = _dot_nt(qm, k2) + bias
                mx = jnp.max(s, axis=-1, keepdims=True)
                p = jnp.exp(s - mx)
                den = jnp.sum(p, axis=-1, keepdims=True)
                res.append((_dot(p.astype(BF16), v2) / den, mx + jnp.log(den)))
            o_ref[half, rows(cur_start), :] = jnp.where(lo, res[0][0], res[1][0])
            lse_ref[half, rows(cur_start), :] = jnp.where(lo, res[0][1], res[1][1])

    def per_residue(r, carry):
        unit(r, kvp_ref, r, band_first)

        def per_sub(j, c):
            unit(j * span + r, kv_ref, (j - 1) * span + r, band)
            return c

        if 1 < n_sub <= ATT_UNROLL:
            for j in range(1, n_sub):
                per_sub(j, 0)
        elif n_sub > 1:
            lax.fori_loop(1, n_sub, per_sub, 0, unroll=ATT_UNROLL)
        return carry

    if dil >= ATT_UNROLL:
        lax.fori_loop(0, dil, per_residue, 0, unroll=max(1, ATT_UNROLL // n_sub))
    else:
        for r in range(dil):
            per_residue(r, 0)


def _dil_attn(qkv, dil):
    s = qkv.shape[1]
    span = Q_BLOCK * dil
    n_prev = ATT_SUPER // span
    blk = lambda n: pl.BlockSpec((n, ATT_SUPER, LANES), lambda i: (0, i, 0))
    return pl.pallas_call(
        functools.partial(_dil_attn_kernel, dil=dil),
        grid=(s // ATT_SUPER,),
        in_specs=[blk(4),
                  pl.BlockSpec((2, ATT_SUPER, LANES), lambda i: (2, i, 0)),
                  pl.BlockSpec((4, span, LANES), lambda i: (0, jnp.maximum(i * n_prev - 1, 0), 0))],
        out_specs=(blk(2), blk(2)),
        out_shape=(jax.ShapeDtypeStruct((2, s, LANES), F32),) * 2,
        compiler_params=_params("arbitrary"),
    )(qkv, qkv, qkv)


def _merge_core(x_ref, hm_ref, hx_ref, gate_ref, od_refs, lse_refs, wm_ref, wdd_ref, wx_ref, wo_ref):
    wide = lambda r: jnp.concatenate([r[0], r[1]], axis=-1)
    lses = [wide(r) for r in lse_refs]
    mx = jnp.maximum(jnp.maximum(lses[0], lses[1]), lses[2])
    es = [jnp.exp(l - mx) for l in lses]
    den = es[0] + es[1] + es[2]
    hd = (es[0] * wide(od_refs[0]) + es[1] * wide(od_refs[1]) + es[2] * wide(od_refs[2])) / den
    d = D_MODEL
    merged = (gate_ref[:, 0:d].astype(F32) * _dot(hm_ref[...], wm_ref[...])
              + gate_ref[:, d:2 * d].astype(F32) * _dot(hd.astype(BF16), wdd_ref[...])
              + gate_ref[:, 2 * d:3 * d].astype(F32) * _dot(hx_ref[...], wx_ref[...]))
    return x_ref[...] + _dot(merged.astype(BF16), wo_ref[...])


def _merge_dense_kernel(x_ref, hm_ref, hx_ref, gate_ref, o0, o1, o2, l0, l1, l2,
                        wm_ref, wdd_ref, wx_ref, wo_ref, gf_ref, wg_ref, wu_ref, wdn_ref,
                        out_ref, acc_ref):
    x1 = _merge_core(x_ref, hm_ref, hx_ref, gate_ref, (o0, o1, o2), (l0, l1, l2),
                     wm_ref, wdd_ref, wx_ref, wo_ref)
    u = _rms(x1, gf_ref[...]).astype(BF16)
    acc_ref[...] = x1

    def body(c, carry):
        cols = pl.ds(pl.multiple_of(c * FF_CHUNK, FF_CHUNK), FF_CHUNK)
        g = _dot(u, wg_ref[:, cols])
        hcol = (g * jax.nn.sigmoid(g) * _dot(u, wu_ref[:, cols])).astype(BF16)
        acc_ref[...] += _dot(hcol, wdn_ref[cols, :])
        return carry

    lax.fori_loop(0, D_FF // FF_CHUNK, body, 0)
    out_ref[...] = acc_ref[...]


def _merge_moe_kernel(x_ref, hm_ref, hx_ref, gate_ref, o0, o1, o2, l0, l1, l2,
                      wm_ref, wdd_ref, wx_ref, wo_ref, gf_ref, wr_ref,
                      x1_out, u_out, route_out, cnt_out, carry_ref):
    tm = x_ref.shape[0]

    @pl.when(pl.program_id(0) == 0)
    def _():
        carry_ref[...] = jnp.zeros(carry_ref.shape, F32)

    x1 = _merge_core(x_ref, hm_ref, hx_ref, gate_ref, (o0, o1, o2), (l0, l1, l2),
                     wm_ref, wdd_ref, wx_ref, wo_ref)
    x1_out[...] = x1
    uf = _rms(x1, gf_ref[...])
    u_out[...] = uf

    uh, um, _ = _split3(uf)
    wh, wmid, _ = _split3(wr_ref[...])
    logits = _dot(uh, wh) + (_dot(uh, wmid) + _dot(um, wh))
    lane = lax.broadcasted_iota(jnp.int32, (tm, LANES), 1).astype(F32)
    valid = lane < N_EXPERTS
    lg = jnp.where(valid, logits, NEG)
    ex = jnp.exp(lg - jnp.max(lg, axis=-1, keepdims=True))
    probs = jnp.where(valid, ex / jnp.sum(ex, axis=-1, keepdims=True), -1.0)
    p1 = jnp.max(probs, axis=-1, keepdims=True)
    i1 = jnp.min(jnp.where(probs == p1, lane, float(LANES)), axis=-1, keepdims=True)
    rest = jnp.where(lane == i1, -1.0, probs)
    p2 = jnp.max(rest, axis=-1, keepdims=True)
    i2 = jnp.min(jnp.where(rest == p2, lane, float(LANES)), axis=-1, keepdims=True)
    g1 = p1 / (p1 + p2)
    g2 = p2 / (p1 + p2)
    sel = jnp.where(lane == i1, 1.0, jnp.where(lane == i2, 1.0, 0.0))
    row = lax.broadcasted_iota(jnp.int32, (tm, tm), 0)
    col = lax.broadcasted_iota(jnp.int32, (tm, tm), 1)
    before = jnp.where(col < row, 1.0, 0.0).astype(BF16)
    ranks = _dot(before, sel.astype(BF16)) + carry_ref[0:1, :]
    r1 = jnp.sum(jnp.where(lane == i1, ranks, 0.0), axis=-1, keepdims=True)
    r2 = jnp.sum(jnp.where(lane == i2, ranks, 0.0), axis=-1, keepdims=True)
    carry_ref[...] = carry_ref[...] + jnp.sum(sel, axis=0, keepdims=True)
    cnt_out[...] = carry_ref[...]
    route = jnp.where(lane == 0, i1, jnp.where(lane == 1, i2, jnp.where(lane == 2, g1,
            jnp.where(lane == 3, g2, jnp.where(lane == 4, r1, jnp.where(lane == 5, r2, 0.0))))))
    route_out[...] = route


def _merge(x, hm, hx, gates, ods, lses, wm, wdd, wx, wo, gf, dense_w=None, w_router=None):
    s = x.shape[0]
    tm = TOK_TILE
    row = lambda w: pl.BlockSpec((tm, w), lambda i: (i, 0))
    acts = (x, hm, hx, gates) + tuple(ods) + tuple(lses)
    slab = pl.BlockSpec((2, tm, LANES), lambda i: (0, i, 0))
    act_specs = [row(D_MODEL), row(MLSTM_W), row(MEM_W), row(3 * D_MODEL)] + [slab] * 6
    if dense_w is not None:
        consts = (wm, wdd, wx, wo, gf) + tuple(dense_w)
        return pl.pallas_call(
            _merge_dense_kernel,
            grid=(s // tm,),
            in_specs=act_specs + [_const_spec(c.shape) for c in consts],
            out_specs=row(D_MODEL),
            out_shape=jax.ShapeDtypeStruct((s, D_MODEL), F32),
            scratch_shapes=[pltpu.VMEM((tm, D_MODEL), F32)],
            compiler_params=_params("arbitrary"),
        )(*acts, *consts)
    consts = (wm, wdd, wx, wo, gf, w_router)
    return pl.pallas_call(
        _merge_moe_kernel,
        grid=(s // tm,),
        in_specs=act_specs + [_const_spec(c.shape) for c in consts],
        out_specs=(row(D_MODEL), row(D_MODEL), row(LANES), pl.BlockSpec((8, LANES), lambda i: (0, 0))),
        out_shape=(jax.ShapeDtypeStruct((s, D_MODEL), F32),
                   jax.ShapeDtypeStruct((s, D_MODEL), F32),
                   jax.ShapeDtypeStruct((s, LANES), F32),
                   jax.ShapeDtypeStruct((8, LANES), F32)),
        scratch_shapes=[pltpu.VMEM((8, LANES), F32)],
        compiler_params=_params("arbitrary"),
    )(*acts, *consts)


def _for_rows(n, fn):
    def body(k, c):
        for j in range(SUBLANES):
            fn(k, j)
        return c
    lax.fori_loop(0, n // SUBLANES, body, 0)


def _dispatch_kernel(zb_ref, p1_ref, p2_ref, u_ref, wg_ref, wu_ref, wd_ref,
                     xs_hbm, wgb_ref, wub_ref, wdb_ref, zero_ref, sem, zsem):
    i = pl.program_id(0)
    td = p1_ref.shape[1]
    bm = zero_ref.shape[0]

    wgb_ref[...] = wg_ref[...].astype(BF16)
    wub_ref[...] = wu_ref[...].astype(BF16)
    wdb_ref[...] = wd_ref[...].astype(BF16)

    @pl.when(i == 0)
    def _():
        zero_ref[...] = jnp.zeros(zero_ref.shape, F32)

        def fill(op):
            def body(b, c):
                @pl.when(zb_ref[b] != 0)
                def _():
                    op(pltpu.make_async_copy(zero_ref, xs_hbm.at[pl.ds(b * bm, bm), :], zsem))
                return c
            lax.fori_loop(0, zb_ref.shape[0], body, 0)

        fill(lambda c: c.start())
        fill(lambda c: c.wait())

    def copies(k, j):
        r = k * SUBLANES + j
        src = u_ref.at[k, pl.ds(j, 1), :]
        return (pltpu.make_async_copy(src, xs_hbm.at[pl.ds(p1_ref[0, r], 1), :], sem),
                pltpu.make_async_copy(src, xs_hbm.at[pl.ds(p2_ref[0, r], 1), :], sem))

    def start(k, j):
        a, b = copies(k, j)
        a.start(priority=0)
        b.start(priority=1)

    def wait(k, j):
        a, b = copies(k, j)
        a.wait()
        b.wait()

    _for_rows(td, start)
    _for_rows(td, wait)


def _dispatch(u, p1, p2, zb, n_rows, wg, wu, wd, li):
    s = u.shape[0]
    td = MOE_TD
    steps = s // td
    assert steps % N_EXPERTS == 0
    parts = steps // N_EXPERTS
    tok = lambda: pl.BlockSpec((None, 1, td), lambda i, zb: (i, 0, 0), memory_space=pltpu.SMEM)

    def w_in_spec(w):
        rows = w.shape[2] // parts
        return pl.BlockSpec((None, None, rows, w.shape[3]), lambda i, zb: (li, i // parts, i % parts, 0))

    def w_out_spec(w):
        rows = w.shape[2] // parts
        return pl.BlockSpec((None, rows, w.shape[3]), lambda i, zb: (i // parts, i % parts, 0))

    grid_spec = pltpu.PrefetchScalarGridSpec(
        num_scalar_prefetch=1,
        grid=(steps,),
        in_specs=[tok(), tok(),
                  pl.BlockSpec((td // SUBLANES, SUBLANES, D_MODEL), lambda i, zb: (i, 0, 0)),
                  w_in_spec(wg), w_in_spec(wu), w_in_spec(wd)],
        out_specs=(pl.BlockSpec(memory_space=pl.ANY), w_out_spec(wg), w_out_spec(wu), w_out_spec(wd)),
        scratch_shapes=[pltpu.VMEM((MOE_BM, D_MODEL), F32),
                        pltpu.SemaphoreType.DMA(()), pltpu.SemaphoreType.DMA(())],
    )
    return pl.pallas_call(
        _dispatch_kernel,
        grid_spec=grid_spec,
        out_shape=(jax.ShapeDtypeStruct((n_rows, D_MODEL), F32),)
        + tuple(jax.ShapeDtypeStruct(w.shape[1:], BF16) for w in (wg, wu, wd)),
        compiler_params=_params("arbitrary"),
    )(zb, p1.reshape(steps, 1, td), p2.reshape(steps, 1, td), u.reshape(-1, SUBLANES, D_MODEL), wg, wu, wd)


def _expert_kernel(be_ref, bv_ref, bx_ref, x_ref, wg_ref, wu_ref, wd_ref, y_ref, xb_ref, acc_ref):
    b = pl.program_id(0)
    f = pl.program_id(1)
    nf = pl.num_programs(1)
    valid = bv_ref[b] != 0

    @pl.when(jnp.logical_and(valid, f == 0))
    def _():
        xb_ref[...] = x_ref[...].astype(BF16)

    @pl.when(valid)
    def _():
        x = xb_ref[...]
        g = _dot(x, wg_ref[...])
        hcol = (g * jax.nn.sigmoid(g) * _dot(x, wu_ref[...])).astype(BF16)
        part = _dot(hcol, wd_ref[...])

        @pl.when(f == 0)
        def _():
            acc_ref[...] = part

        @pl.when(jnp.logical_and(f != 0, f != nf - 1))
        def _():
            acc_ref[...] += part

        @pl.when(f == nf - 1)
        def _():
            y_ref[...] = acc_ref[...] + part

    @pl.when(jnp.logical_and(jnp.logical_not(valid), f == nf - 1))
    def _():
        y_ref[...] = jnp.zeros(y_ref.shape, F32)


def _experts(xs, wg, wu, wd, be, bv, bx):
    n_rows = xs.shape[0]
    bm, tf = MOE_BM, MOE_TF
    nf = D_FF_EXPERT // tf
    assert nf >= 2

    def fidx(b, f, bv):
        return jnp.where(bv[b] != 0, f, nf - 1)

    grid_spec = pltpu.PrefetchScalarGridSpec(
        num_scalar_prefetch=3,
        grid=(n_rows // bm, nf),
        in_specs=[pl.BlockSpec((bm, D_MODEL), lambda b, f, be, bv, bx: (bx[b], 0)),
                  pl.BlockSpec((None, D_MODEL, tf), lambda b, f, be, bv, bx: (be[b], 0, fidx(b, f, bv))),
                  pl.BlockSpec((None, D_MODEL, tf), lambda b, f, be, bv, bx: (be[b], 0, fidx(b, f, bv))),
                  pl.BlockSpec((None, tf, D_MODEL), lambda b, f, be, bv, bx: (be[b], fidx(b, f, bv), 0))],
        out_specs=pl.BlockSpec((bm, D_MODEL), lambda b, f, be, bv, bx: (b, 0)),
        scratch_shapes=[pltpu.VMEM((bm, D_MODEL), BF16),
                        pltpu.VMEM((bm, D_MODEL), F32)],
    )
    return pl.pallas_call(
        _expert_kernel,
        grid_spec=grid_spec,
        out_shape=jax.ShapeDtypeStruct((n_rows, D_MODEL), F32),
        compiler_params=_params("arbitrary", "arbitrary"),
    )(be, bv, bx, xs, wg, wu, wd)


def _combine_kernel(p1_ref, p2_ref, p1n_ref, p2n_ref, x_ref, route_ref, gn_ref, ys_hbm, out_ref,
                    yg_ref, sem, *, final_norm):
    i = pl.program_id(0)
    tc = x_ref.shape[0]
    slot = lax.rem(i, 2)

    def gather(pa_ref, pb_ref, s_, op):
        def rows(k, j):
            r = k * SUBLANES + j
            op(pltpu.make_async_copy(ys_hbm.at[pl.ds(pa_ref[0, r], 1), :],
                                     yg_ref.at[s_, 0, k, pl.ds(j, 1), :], sem.at[s_]), 0)
            op(pltpu.make_async_copy(ys_hbm.at[pl.ds(pb_ref[0, r], 1), :],
                                     yg_ref.at[s_, 1, k, pl.ds(j, 1), :], sem.at[s_]), 1)
        _for_rows(tc, rows)

    start = lambda c, queue: c.start(priority=queue)
    wait = lambda c, queue: c.wait()

    @pl.when(i == 0)
    def _():
        gather(p1_ref, p2_ref, 0, start)

    gather(p1_ref, p2_ref, slot, wait)

    @pl.when(i + 1 < pl.num_programs(0))
    def _():
        gather(p1n_ref, p2n_ref, 1 - slot, start)

    y1 = yg_ref[slot, 0].reshape(tc, D_MODEL)
    y2 = yg_ref[slot, 1].reshape(tc, D_MODEL)
    out = x_ref[...] + route_ref[:, 2:3] * y1 + route_ref[:, 3:4] * y2
    out_ref[...] = _rms(out, gn_ref[...]) if final_norm else out


def _combine(x1, ys, p1, p2, route, gn, final_norm):
    s = x1.shape[0]
    tc = MOE_TC
    nt = s // tc
    row = lambda w: pl.BlockSpec((tc, w), lambda i: (i, 0))
    cur = lambda: pl.BlockSpec((None, 1, tc), lambda i: (i, 0, 0), memory_space=pltpu.SMEM)
    nxt = lambda: pl.BlockSpec((None, 1, tc), lambda i: (jnp.minimum(i + 1, nt - 1), 0, 0),
                               memory_space=pltpu.SMEM)
    p1 = p1.reshape(nt, 1, tc)
    p2 = p2.reshape(nt, 1, tc)
    return pl.pallas_call(
        functools.partial(_combine_kernel, final_norm=final_norm),
        grid=(nt,),
        in_specs=[cur(), cur(), nxt(), nxt(), row(D_MODEL), row(LANES), _const_spec(gn.shape),
                  pl.BlockSpec(memory_space=pl.ANY)],
        out_specs=row(D_MODEL),
        out_shape=jax.ShapeDtypeStruct((s, D_MODEL), F32),
        scratch_shapes=[pltpu.VMEM((2, 2, tc // SUBLANES, SUBLANES, D_MODEL), F32),
                        pltpu.SemaphoreType.DMA((2,))],
        compiler_params=_params("arbitrary"),
    )(p1, p2, p1, p2, x1, route, gn, ys)


def _moe_plan(route, counts, s):
    bm = MOE_BM
    nb = (2 * s) // bm + N_EXPERTS
    i1 = route[:, 0].astype(jnp.int32)
    i2 = route[:, 1].astype(jnp.int32)
    r1 = route[:, 4].astype(jnp.int32)
    r2 = route[:, 5].astype(jnp.int32)
    cnt = counts[0, :N_EXPERTS].astype(jnp.int32)
    padded = ((cnt + bm - 1) // bm) * bm
    ends = jnp.cumsum(padded)
    off = ends - padded
    p1 = off[i1] + r1
    p2 = off[i2] + r2
    nb_used = ends[-1] // bm

    bidx = jnp.arange(nb, dtype=jnp.int32)
    bvalid = (bidx < nb_used).astype(jnp.int32)
    bsrc = jnp.minimum(bidx, jnp.maximum(nb_used - 1, 0))
    bexp = jnp.sum((bsrc[:, None] * bm >= ends[None, :]).astype(jnp.int32), axis=1)
    bexp = jnp.minimum(bexp, N_EXPERTS - 1)
    real = jnp.clip(off[bexp] + cnt[bexp] - bidx * bm, 0, bm)
    bzero = ((real < bm) | (bvalid == 0)).astype(jnp.int32)
    return p1, p2, bexp, bvalid, bsrc, bzero


def _inproj_weight(w_in, q_scale):
    cols = [w_in[:, :OFF_QD], jnp.zeros((D_MODEL, LANES - 2 * MLSTM_HEADS), w_in.dtype)]
    for g in range(len(DIL_PATTERNS)):
        sl = lambda off: w_in[:, off + g * DIL_GW: off + (g + 1) * DIL_GW]
        cols += [sl(OFF_KD), sl(OFF_VD), sl(OFF_QD) * q_scale]
    cols.append(w_in[:, OFF_QX:])
    return jnp.concatenate(cols, axis=-1).astype(BF16)


def kernel(x, mem, norm_mix, w_in, conv_qk, b_gate_if, mlstm_norm, norm_mem, w_mem_kv, w_br_m, w_br_d,
           w_br_x, w_out, norm_ffn, ffn_w_gate, ffn_w_up, ffn_w_down, moe_router, moe_w_gate, moe_w_up,
           moe_w_down, norm_final):
    s = x.shape[1]
    xs = x.reshape(s, D_MODEL)
    mem2 = mem.reshape(N_MEM, D_MODEL)
    row = lambda a: a.reshape(1, -1)
    q_scale = DIL_DH ** -0.5

    for layer in range(DEPTH):
        wl = w_in[layer]
        g_mix = row(norm_mix[layer])
        km, vm = _memkv(mem2, row(norm_mem[layer]), w_mem_kv[layer].astype(BF16))

        bif = jnp.pad(b_gate_if[layer], (0, LANES - 2 * MLSTM_HEADS)).reshape(1, LANES)
        (q_m, k_m, v_m, o_m, ifc, ifr, d0, d1, d2, h_x, gates) = _inproj(
            xs, g_mix, _inproj_weight(wl, q_scale), bif, conv_qk[layer], km, vm)

        h_m = _mlstm(q_m, k_m, v_m, o_m, ifc, ifr, row(mlstm_norm[layer]))

        ods, lses = [], []
        for qkv, (_, dil) in zip((d0, d1, d2), DIL_PATTERNS):
            o_g, lse_g = _dil_attn(qkv, dil)
            ods.append(o_g)
            lses.append(lse_g)

        merge_w = (w_br_m[layer].astype(BF16), w_br_d[layer].astype(BF16), w_br_x[layer].astype(BF16),
                   w_out[layer].astype(BF16), row(norm_ffn[layer]))
        if layer % 2 == 0:
            li = layer // 2
            dense_w = (ffn_w_gate[li].astype(BF16), ffn_w_up[li].astype(BF16), ffn_w_down[li].astype(BF16))
            xs = _merge(xs, h_m, h_x, gates, ods, lses, *merge_w, dense_w=dense_w)
        else:
            li = layer // 2
            wr = jnp.pad(moe_router[li], ((0, 0), (0, LANES - N_EXPERTS)))
            x1, u, route, counts = _merge(xs, h_m, h_x, gates, ods, lses, *merge_w, w_router=wr)
            p1, p2, bexp, bvalid, bsrc, bzero = _moe_plan(route, counts, s)
            rows, wg, wu, wd = _dispatch(u, p1, p2, bzero, bexp.shape[0] * MOE_BM,
                                         moe_w_gate, moe_w_up, moe_w_down, li)
            y = _experts(rows, wg, wu, wd, bexp, bvalid, bsrc)
            xs = _combine(x1, y, p1, p2, route, row(norm_final), final_norm=layer == DEPTH - 1)
    return xs.reshape(x.shape)
```

```python
import functools

import jax
import jax.numpy as jnp
from jax import lax
from jax.experimental import pallas as pl
from jax.experimental.pallas import tpu as pltpu

F32 = jnp.float32
BF16 = jnp.bfloat16

EPS = 1e-6
D_MODEL = 1024
DEPTH = 4
N_MEM = 256
MLSTM_HEADS = 4
MLSTM_DH = 128
MLSTM_W = MLSTM_HEADS * MLSTM_DH
MLSTM_CHUNK = 128
CONV_W = 4
M_INIT = -1e30
DIL_PATTERNS = ((128, 1), (512, 4), (2048, 16))
DIL_HEADS = 4
DIL_DH = 64
DIL_GW = DIL_HEADS * DIL_DH
DIL_W = 3 * DIL_GW
Q_BLOCK = 128
MEM_HEADS = 4
MEM_DH = 128
MEM_W = MEM_HEADS * MEM_DH
D_FF = 2816
N_EXPERTS = 8
D_FF_EXPERT = 3584

OFF_IF = 4 * MLSTM_W
OFF_QD = OFF_IF + 2 * MLSTM_HEADS
OFF_KD = OFF_QD + DIL_W
OFF_VD = OFF_KD + DIL_W
OFF_QX = OFF_VD + DIL_W
OFF_GATE = OFF_QX + MEM_W
IN_COLS = OFF_GATE + 3 * D_MODEL

LANES = 128
SUBLANES = 8
NEG = -1e30
VMEM_LIMIT = 56 * 1024 * 1024

DIL_SLABS = 3 * DIL_GW // LANES

WCOL_IF = 4 * MLSTM_W
WCOL_DIL = WCOL_IF + LANES
WCOL_QX = WCOL_DIL + 3 * DIL_W
WCOL_GATE = WCOL_QX + MEM_W
WCOL_END = WCOL_GATE + 3 * D_MODEL
MLSTM_CHUNKS_PER_STEP = 1
ATT_UNROLL = 8
ATT_SUPER = 2048
TOK_TILE = 512
FF_CHUNK = 1408
MOE_BM = 512
MOE_TD = 512
MOE_TC = 512
MOE_TF = 1792

NT_DIMS = (((1,), (1,)), ((), ()))
TN_DIMS = (((0,), (0,)), ((), ()))


def _params(*sem):
    return pltpu.CompilerParams(dimension_semantics=sem, vmem_limit_bytes=VMEM_LIMIT)


def _dot(a, b):
    return jnp.dot(a, b, preferred_element_type=F32)


def _dot_nt(a, b):
    return lax.dot_general(a, b, NT_DIMS, preferred_element_type=F32)


def _rms(x, g):
    return x * lax.rsqrt(jnp.mean(x * x, axis=-1, keepdims=True) + EPS) * g


def _split3(x):
    hi = x.astype(BF16)
    r1 = x - hi.astype(F32)
    mid = r1.astype(BF16)
    lo = (r1 - mid.astype(F32)).astype(BF16)
    return hi, mid, lo


def _const_spec(shape):
    nd = len(shape)
    return pl.BlockSpec(shape, lambda *_: (0,) * nd, pipeline_mode=pl.Buffered(1))


def _memkv_kernel(mem_ref, g_ref, w_ref, k_ref, v_ref):
    u = _rms(mem_ref[...], g_ref[...]).astype(BF16)
    kv = _dot(u, w_ref[...])
    k_ref[...] = kv[:, :MEM_W].astype(BF16)
    v_ref[...] = kv[:, MEM_W:].astype(BF16)


def _memkv(mem, g, w_kv):
    return pl.pallas_call(
        _memkv_kernel,
        out_shape=(jax.ShapeDtypeStruct((N_MEM, MEM_W), BF16),) * 2,
        compiler_params=pltpu.CompilerParams(vmem_limit_bytes=VMEM_LIMIT),
    )(mem, g, w_kv)


def _inproj_kernel(x_ref, g_ref, w_ref, bif_ref, cw_ref, km_ref, vm_ref,
                   q_out, k_out, v_out, o_out, if_out, ift_out, d0_out, d1_out, d2_out,
                   hx_out, gate_out, conv_buf):
    tm = x_ref.shape[0]
    u = _rms(x_ref[...], g_ref[...]).astype(BF16)

    @pl.when(pl.program_id(0) == 0)
    def _():
        conv_buf[0:8, :] = jnp.zeros((8, 2 * MLSTM_W), F32)

    conv_buf[8:tm + 8, :] = _dot(u, w_ref[:, 0:2 * MLSTM_W])
    acc = cw_ref[0:1, :] * conv_buf[pl.ds(8 - (CONV_W - 1), tm), :]
    for j in range(1, CONV_W):
        acc = acc + cw_ref[j:j + 1, :] * conv_buf[pl.ds(8 - (CONV_W - 1) + j, tm), :]
    conv_buf[0:8, :] = conv_buf[tm:tm + 8, :]
    qk = acc * jax.nn.sigmoid(acc)
    q_out[...] = qk[:, :MLSTM_W].astype(BF16)
    k_out[...] = (qk[:, MLSTM_W:] * (MLSTM_DH ** -0.5)).astype(BF16)

    v_out[...] = _dot(u, w_ref[:, 2 * MLSTM_W:3 * MLSTM_W]).astype(BF16)
    o_out[...] = jax.nn.sigmoid(_dot(u, w_ref[:, 3 * MLSTM_W:4 * MLSTM_W])).astype(BF16)

    if_pre = _dot(u, w_ref[:, WCOL_IF:WCOL_DIL]) + bif_ref[...]
    if_out[...] = if_pre
    ift_out[...] = if_pre.T[0:ift_out.shape[0], :]

    for gi, d_out in enumerate((d0_out, d1_out, d2_out)):
        d = _dot(u, w_ref[:, WCOL_DIL + gi * 3 * DIL_GW:WCOL_DIL + (gi + 1) * 3 * DIL_GW])
        for j in range(DIL_SLABS):
            d_out[j] = d[:, j * LANES:(j + 1) * LANES]

    qx = (_dot(u, w_ref[:, WCOL_QX:WCOL_GATE]) * (MEM_DH ** -0.5)).astype(BF16)
    outs = []
    for h in range(MEM_HEADS):
        sl = slice(h * MEM_DH, (h + 1) * MEM_DH)
        s = _dot_nt(qx[:, sl], km_ref[:, sl])
        p = jnp.exp(s - jnp.max(s, axis=-1, keepdims=True))
        den = jnp.sum(p, axis=-1, keepdims=True)
        outs.append(_dot(p.astype(BF16), vm_ref[:, sl]) / den)
    hx_out[...] = jnp.concatenate(outs, axis=-1).astype(BF16)

    gate_out[...] = jax.nn.sigmoid(_dot(u, w_ref[:, WCOL_GATE:WCOL_END])).astype(BF16)


def _inproj(x, g, w, bif, cw, km, vm):
    s = x.shape[0]
    tm = TOK_TILE
    row = lambda w: pl.BlockSpec((tm, w), lambda i: (i, 0))
    out_shape = (
        jax.ShapeDtypeStruct((s, MLSTM_W), BF16),
        jax.ShapeDtypeStruct((s, MLSTM_W), BF16),
        jax.ShapeDtypeStruct((s, MLSTM_W), BF16),
        jax.ShapeDtypeStruct((s, MLSTM_W), BF16),
        jax.ShapeDtypeStruct((s, LANES), F32),
        jax.ShapeDtypeStruct((8, s), F32),
        jax.ShapeDtypeStruct((DIL_SLABS, s, LANES), F32),
        jax.ShapeDtypeStruct((DIL_SLABS, s, LANES), F32),
        jax.ShapeDtypeStruct((DIL_SLABS, s, LANES), F32),
        jax.ShapeDtypeStruct((s, MEM_W), BF16),
        jax.ShapeDtypeStruct((s, 3 * D_MODEL), BF16),
    )
    slab = pl.BlockSpec((DIL_SLABS, tm, LANES), lambda i: (0, i, 0))
    out_specs = (row(MLSTM_W), row(MLSTM_W), row(MLSTM_W), row(MLSTM_W), row(LANES),
                 pl.BlockSpec((8, tm), lambda i: (0, i)),
                 slab, slab, slab, row(MEM_W), row(3 * D_MODEL))
    in_specs = [row(D_MODEL)] + [_const_spec(a.shape) for a in (g, w, bif, cw, km, vm)]
    return pl.pallas_call(
        _inproj_kernel,
        grid=(s // tm,),
        in_specs=in_specs,
        out_specs=out_specs,
        out_shape=out_shape,
        scratch_shapes=[pltpu.VMEM((tm + 8, 2 * MLSTM_W), F32)],
        compiler_params=_params("arbitrary"),
    )(x, g, w, bif, cw, km, vm)


def _log_sigmoid(x):
    return jnp.minimum(x, 0.0) - jnp.log(1.0 + jnp.exp(-jnp.abs(x)))


def _mlstm_kernel(q_ref, k_ref, v_ref, o_ref, ifc_ref, ifr_ref, g_ref, out_ref, ct_ref, m_ref):
    L = MLSTM_CHUNK
    H = MLSTM_HEADS

    @pl.when(pl.program_id(0) == 0)
    def _():
        ct_ref[...] = jnp.zeros(ct_ref.shape, F32)
        m_ref[...] = jnp.full(m_ref.shape, M_INIT, F32)

    row = lax.broadcasted_iota(jnp.int32, (L, L), 0)
    col = lax.broadcasted_iota(jnp.int32, (L, L), 1)
    causal = col <= row
    tril = jnp.where(causal, 1.0, 0.0).astype(BF16)
    triu = jnp.where(row <= col, 1.0, 0.0).astype(BF16)
    ones_col = jnp.where(col == 0, 1.0, 0.0).astype(BF16)

    for ci in range(q_ref.shape[0] // L):
        rows = slice(ci * L, (ci + 1) * L)
        ifc = ifc_ref[rows, :]
        ifr = ifr_ref[:, rows]
        cum_c = sum(_dot(tril, p) for p in _split3(_log_sigmoid(ifc)))
        cum_r = sum(_dot(p, triu) for p in _split3(_log_sigmoid(ifr)))

        for h in range(H):
            sl = slice(h * MLSTM_DH, (h + 1) * MLSTM_DH)
            i_c = ifc[:, h:h + 1]
            i_r = ifr[h:h + 1, :]
            cc = cum_c[:, H + h:H + h + 1]
            cr = cum_r[H + h:H + h + 1, :]
            total = cr[:, L - 1:L]
            m_prev = m_ref[h:h + 1, 0:1]

            dm = jnp.where(causal, cc - cr + i_r, -jnp.inf)
            inter = cc + m_prev
            m_row = jnp.maximum(jnp.max(dm, axis=-1, keepdims=True), inter)
            w_intra = jnp.exp(dm - m_row)
            w_inter = jnp.exp(inter - m_row)

            qh = q_ref[rows, sl]
            kh = k_ref[rows, sl]
            vaug = jnp.concatenate([v_ref[rows, sl], ones_col], axis=-1)
            s_mat = _dot_nt(qh, kh) * w_intra
            tot = _dot(s_mat.astype(BF16), vaug) + w_inter * _dot(qh, ct_ref[h].astype(BF16))
            den = tot[:, MLSTM_DH:MLSTM_DH + 1]
            h_out = tot[:, :MLSTM_DH] / jnp.maximum(jnp.abs(den), jnp.exp(-m_row))

            g_end = total - cc + i_c
            m_new = jnp.maximum(total + m_prev, jnp.max(g_end, axis=0, keepdims=True))
            w_end = jnp.exp(g_end - m_new)
            decay = jnp.exp(total + m_prev - m_new)
            vw = (vaug.astype(F32) * w_end).astype(BF16)
            ct_ref[h] = decay * ct_ref[h] + lax.dot_general(kh, vw, TN_DIMS, preferred_element_type=F32)
            m_ref[h:h + 1, :] = jnp.broadcast_to(m_new, (1, LANES))

            mu = jnp.mean(h_out, axis=-1, keepdims=True)
            cen = h_out - mu
            var = jnp.mean(cen * cen, axis=-1, keepdims=True)
            y = cen * lax.rsqrt(var + EPS) * g_ref[:, sl] * o_ref[rows, sl].astype(F32)
            out_ref[rows, sl] = y.astype(BF16)


def _mlstm(q, k, v, o, ifc, ifr, g):
    s = q.shape[0]
    L = MLSTM_CHUNK * MLSTM_CHUNKS_PER_STEP
    row = pl.BlockSpec((L, MLSTM_W), lambda c: (c, 0))
    return pl.pallas_call(
        _mlstm_kernel,
        grid=(s // L,),
        in_specs=[row, row, row, row,
                  pl.BlockSpec((L, LANES), lambda c: (c, 0)),
                  pl.BlockSpec((8, L), lambda c: (0, c)),
                  _const_spec(g.shape)],
        out_specs=row,
        out_shape=jax.ShapeDtypeStruct((s, MLSTM_W), BF16),
        scratch_shapes=[pltpu.VMEM((MLSTM_HEADS, MLSTM_DH, 2 * MLSTM_DH), F32),
                        pltpu.VMEM((8, LANES), F32)],
        compiler_params=_params("arbitrary"),
    )(q, k, v, o, ifc, ifr, g)


def _dil_attn_kernel(kv_ref, q_ref, kvp_ref, o_ref, lse_ref, *, dil):
    B = Q_BLOCK
    span = B * dil
    n_sub = ATT_SUPER // span
    row = lax.broadcasted_iota(jnp.int32, (B, 2 * B), 0)
    col = lax.broadcasted_iota(jnp.int32, (B, 2 * B), 1)
    band = jnp.where(col >= row, jnp.where(col <= row + B, 0.0, NEG), NEG)
    first = jnp.where(pl.program_id(0) == 0, 1.0, 0.0)
    band_first = band + first * jnp.where(col < B, NEG, 0.0)
    lane = lax.broadcasted_iota(jnp.int32, (B, LANES), 1)
    lo = lane < DIL_DH
    hi = lane >= DIL_DH

    def rows(start):
        return pl.ds(start, B, stride=dil) if dil > 1 else pl.ds(start, B)

    def unit(cur_start, prev_ref, prev_start, bias):
        for half in range(2):
            q2 = q_ref[half, rows(cur_start), :]
            k2 = jnp.concatenate([prev_ref[half, rows(prev_start), :],
                                  kv_ref[half, rows(cur_start), :]], axis=0).astype(BF16)
            v2 = jnp.concatenate([prev_ref[2 + half, rows(prev_start), :],
                                  kv_ref[2 + half, rows(cur_start), :]], axis=0).astype(BF16)
            res = []
            for keep in (lo, hi):
                qm = jnp.where(keep, q2, 0.0).astype(BF16)
                s = _dot_nt(qm, k2) + bias
                mx = jnp.max(s, axis=-1, keepdims=True)
                p = jnp.exp(s - mx)
                den = jnp.sum(p, axis=-1, keepdims=True)
                res.append((_dot(p.astype(BF16), v2) / den, mx + jnp.log(den)))
            o_ref[half, rows(cur_start), :] = jnp.where(lo, res[0][0], res[1][0])
            lse_ref[half, rows(cur_start), :] = jnp.where(lo, res[0][1], res[1][1])

    def per_residue(r, carry):
        unit(r, kvp_ref, r, band_first)

        def per_sub(j, c):
            unit(j * span + r, kv_ref, (j - 1) * span + r, band)
            return c

        if 1 < n_sub <= ATT_UNROLL:
            for j in range(1, n_sub):
                per_sub(j, 0)
        elif n_sub > 1:
            lax.fori_loop(1, n_sub, per_sub, 0, unroll=ATT_UNROLL)
        return carry

    if dil >= ATT_UNROLL:
        lax.fori_loop(0, dil, per_residue, 0, unroll=max(1, ATT_UNROLL // n_sub))
    else:
        for r in range(dil):
            per_residue(r, 0)


def _dil_attn(qkv, dil):
    s = qkv.shape[1]
    span = Q_BLOCK * dil
    n_prev = ATT_SUPER // span
    blk = lambda n: pl.BlockSpec((n, ATT_SUPER, LANES), lambda i: (0, i, 0))
    return pl.pallas_call(
        functools.partial(_dil_attn_kernel, dil=dil),
        grid=(s // ATT_SUPER,),
        in_specs=[blk(4),
                  pl.BlockSpec((2, ATT_SUPER, LANES), lambda i: (2, i, 0)),
                  pl.BlockSpec((4, span, LANES), lambda i: (0, jnp.maximum(i * n_prev - 1, 0), 0))],
        out_specs=(blk(2), blk(2)),
        out_shape=(jax.ShapeDtypeStruct((2, s, LANES), F32),) * 2,
        compiler_params=_params("arbitrary"),
    )(qkv, qkv, qkv)


def _merge_core(x_ref, hm_ref, hx_ref, gate_ref, od_refs, lse_refs, wm_ref, wdd_ref, wx_ref, wo_ref):
    wide = lambda r: jnp.concatenate([r[0], r[1]], axis=-1)
    lses = [wide(r) for r in lse_refs]
    mx = jnp.maximum(jnp.maximum(lses[0], lses[1]), lses[2])
    es = [jnp.exp(l - mx) for l in lses]
    den = es[0] + es[1] + es[2]
    hd = (es[0] * wide(od_refs[0]) + es[1] * wide(od_refs[1]) + es[2] * wide(od_refs[2])) / den
    d = D_MODEL
    merged = (gate_ref[:, 0:d].astype(F32) * _dot(hm_ref[...], wm_ref[...])
              + gate_ref[:, d:2 * d].astype(F32) * _dot(hd.astype(BF16), wdd_ref[...])
              + gate_ref[:, 2 * d:3 * d].astype(F32) * _dot(hx_ref[...], wx_ref[...]))
    return x_ref[...] + _dot(merged.astype(BF16), wo_ref[...])


def _merge_dense_kernel(x_ref, hm_ref, hx_ref, gate_ref, o0, o1, o2, l0, l1, l2,
                        wm_ref, wdd_ref, wx_ref, wo_ref, gf_ref, wg_ref, wu_ref, wdn_ref,
                        out_ref, acc_ref):
    x1 = _merge_core(x_ref, hm_ref, hx_ref, gate_ref, (o0, o1, o2), (l0, l1, l2),
                     wm_ref, wdd_ref, wx_ref, wo_ref)
    u = _rms(x1, gf_ref[...]).astype(BF16)
    acc_ref[...] = x1

    def body(c, carry):
        cols = pl.ds(pl.multiple_of(c * FF_CHUNK, FF_CHUNK), FF_CHUNK)
        g = _dot(u, wg_ref[:, cols])
        hcol = (g * jax.nn.sigmoid(g) * _dot(u, wu_ref[:, cols])).astype(BF16)
        acc_ref[...] += _dot(hcol, wdn_ref[cols, :])
        return carry

    lax.fori_loop(0, D_FF // FF_CHUNK, body, 0)
    out_ref[...] = acc_ref[...]


def _merge_moe_kernel(x_ref, hm_ref, hx_ref, gate_ref, o0, o1, o2, l0, l1, l2,
                      wm_ref, wdd_ref, wx_ref, wo_ref, gf_ref, wr_ref,
                      x1_out, u_out, route_out, cnt_out, carry_ref):
    tm = x_ref.shape[0]

    @pl.when(pl.program_id(0) == 0)
    def _():
        carry_ref[...] = jnp.zeros(carry_ref.shape, F32)

    x1 = _merge_core(x_ref, hm_ref, hx_ref, gate_ref, (o0, o1, o2), (l0, l1, l2),
                     wm_ref, wdd_ref, wx_ref, wo_ref)
    x1_out[...] = x1
    uf = _rms(x1, gf_ref[...])
    u_out[...] = uf

    uh, um, _ = _split3(uf)
    wh, wmid, _ = _split3(wr_ref[...])
    logits = _dot(uh, wh) + (_dot(uh, wmid) + _dot(um, wh))
    lane = lax.broadcasted_iota(jnp.int32, (tm, LANES), 1).astype(F32)
    valid = lane < N_EXPERTS
    lg = jnp.where(valid, logits, NEG)
    ex = jnp.exp(lg - jnp.max(lg, axis=-1, keepdims=True))
    probs = jnp.where(valid, ex / jnp.sum(ex, axis=-1, keepdims=True), -1.0)
    p1 = jnp.max(probs, axis=-1, keepdims=True)
    i1 = jnp.min(jnp.where(probs == p1, lane, float(LANES)), axis=-1, keepdims=True)
    rest = jnp.where(lane == i1, -1.0, probs)
    p2 = jnp.max(rest, axis=-1, keepdims=True)
    i2 = jnp.min(jnp.where(rest == p2, lane, float(LANES)), axis=-1, keepdims=True)
    g1 = p1 / (p1 + p2)
    g2 = p2 / (p1 + p2)
    sel = jnp.where(lane == i1, 1.0, jnp.where(lane == i2, 1.0, 0.0))
    row = lax.broadcasted_iota(jnp.int32, (tm, tm), 0)
    col = lax.broadcasted_iota(jnp.int32, (tm, tm), 1)
    before = jnp.where(col < row, 1.0, 0.0).astype(BF16)
    ranks = _dot(before, sel.astype(BF16)) + carry_ref[0:1, :]
    r1 = jnp.sum(jnp.where(lane == i1, ranks, 0.0), axis=-1, keepdims=True)
    r2 = jnp.sum(jnp.where(lane == i2, ranks, 0.0), axis=-1, keepdims=True)
    carry_ref[...] = carry_ref[...] + jnp.sum(sel, axis=0, keepdims=True)
    cnt_out[...] = carry_ref[...]
    route = jnp.where(lane == 0, i1, jnp.where(lane == 1, i2, jnp.where(lane == 2, g1,
            jnp.where(lane == 3, g2, jnp.where(lane == 4, r1, jnp.where(lane == 5, r2, 0.0))))))
    route_out[...] = route


def _merge(x, hm, hx, gates, ods, lses, wm, wdd, wx, wo, gf, dense_w=None, w_router=None):
    s = x.shape[0]
    tm = TOK_TILE
    row = lambda w: pl.BlockSpec((tm, w), lambda i: (i, 0))
    acts = (x, hm, hx, gates) + tuple(ods) + tuple(lses)
    slab = pl.BlockSpec((2, tm, LANES), lambda i: (0, i, 0))
    act_specs = [row(D_MODEL), row(MLSTM_W), row(MEM_W), row(3 * D_MODEL)] + [slab] * 6
    if dense_w is not None:
        consts = (wm, wdd, wx, wo, gf) + tuple(dense_w)
        return pl.pallas_call(
            _merge_dense_kernel,
            grid=(s // tm,),
            in_specs=act_specs + [_const_spec(c.shape) for c in consts],
            out_specs=row(D_MODEL),
            out_shape=jax.ShapeDtypeStruct((s, D_MODEL), F32),
            scratch_shapes=[pltpu.VMEM((tm, D_MODEL), F32)],
            compiler_params=_params("arbitrary"),
        )(*acts, *consts)
    consts = (wm, wdd, wx, wo, gf, w_router)
    return pl.pallas_call(
        _merge_moe_kernel,
        grid=(s // tm,),
        in_specs=act_specs + [_const_spec(c.shape) for c in consts],
        out_specs=(row(D_MODEL), row(D_MODEL), row(LANES), pl.BlockSpec((8, LANES), lambda i: (0, 0))),
        out_shape=(jax.ShapeDtypeStruct((s, D_MODEL), F32),
                   jax.ShapeDtypeStruct((s, D_MODEL), F32),
                   jax.ShapeDtypeStruct((s, LANES), F32),
                   jax.ShapeDtypeStruct((8, LANES), F32)),
        scratch_shapes=[pltpu.VMEM((8, LANES), F32)],
        compiler_params=_params("arbitrary"),
    )(*acts, *consts)


def _for_rows(n, fn):
    def body(k, c):
        for j in range(SUBLANES):
            fn(k, j)
        return c
    lax.fori_loop(0, n // SUBLANES, body, 0)


def _dispatch_kernel(zb_ref, p1_ref, p2_ref, u_ref, wg_ref, wu_ref, wd_ref,
                     xs_hbm, wgb_ref, wub_ref, wdb_ref, zero_ref, sem, zsem):
    i = pl.program_id(0)
    td = p1_ref.shape[1]
    bm = zero_ref.shape[0]


    @pl.when(i == 0)
    def _():
        zero_ref[...] = jnp.zeros(zero_ref.shape, F32)

        def fill(op):
            def body(b, c):
                @pl.when(zb_ref[b] != 0)
                def _():
                    op(pltpu.make_async_copy(zero_ref, xs_hbm.at[pl.ds(b * bm, bm), :], zsem))
                return c
            lax.fori_loop(0, zb_ref.shape[0], body, 0)

        fill(lambda c: c.start())
        fill(lambda c: c.wait())

    def copies(k, j):
        r = k * SUBLANES + j
        src = u_ref.at[k, pl.ds(j, 1), :]
        return (pltpu.make_async_copy(src, xs_hbm.at[pl.ds(p1_ref[0, r], 1), :], sem),
                pltpu.make_async_copy(src, xs_hbm.at[pl.ds(p2_ref[0, r], 1), :], sem))

    def start(k, j):
        a, b = copies(k, j)
        a.start(priority=0)
        b.start(priority=1)

    def wait(k, j):
        a, b = copies(k, j)
        a.wait()
        b.wait()

    _for_rows(td, start)
    wgb_ref[...] = wg_ref[...].astype(BF16)
    wub_ref[...] = wu_ref[...].astype(BF16)
    wdb_ref[...] = wd_ref[...].astype(BF16)
    _for_rows(td, wait)


def _dispatch(u, p1, p2, zb, n_rows, wg, wu, wd, li):
    s = u.shape[0]
    td = MOE_TD
    steps = s // td
    assert steps % N_EXPERTS == 0
    parts = steps // N_EXPERTS
    tok = lambda: pl.BlockSpec((None, 1, td), lambda i, zb: (i, 0, 0), memory_space=pltpu.SMEM)

    def w_in_spec(w):
        rows = w.shape[2] // parts
        return pl.BlockSpec((None, None, rows, w.shape[3]), lambda i, zb: (li, i // parts, i % parts, 0))

    def w_out_spec(w):
        rows = w.shape[2] // parts
        return pl.BlockSpec((None, rows, w.shape[3]), lambda i, zb: (i // parts, i % parts, 0))

    grid_spec = pltpu.PrefetchScalarGridSpec(
        num_scalar_prefetch=1,
        grid=(steps,),
        in_specs=[tok(), tok(),
                  pl.BlockSpec((td // SUBLANES, SUBLANES, D_MODEL), lambda i, zb: (i, 0, 0)),
                  w_in_spec(wg), w_in_spec(wu), w_in_spec(wd)],
        out_specs=(pl.BlockSpec(memory_space=pl.ANY), w_out_spec(wg), w_out_spec(wu), w_out_spec(wd)),
        scratch_shapes=[pltpu.VMEM((MOE_BM, D_MODEL), F32),
                        pltpu.SemaphoreType.DMA(()), pltpu.SemaphoreType.DMA(())],
    )
    return pl.pallas_call(
        _dispatch_kernel,
        grid_spec=grid_spec,
        out_shape=(jax.ShapeDtypeStruct((n_rows, D_MODEL), F32),)
        + tuple(jax.ShapeDtypeStruct(w.shape[1:], BF16) for w in (wg, wu, wd)),
        compiler_params=_params("arbitrary"),
    )(zb, p1.reshape(steps, 1, td), p2.reshape(steps, 1, td), u.reshape(-1, SUBLANES, D_MODEL), wg, wu, wd)


def _expert_kernel(be_ref, bv_ref, bx_ref, x_ref, wg_ref, wu_ref, wd_ref, y_ref, xb_ref, acc_ref):
    b = pl.program_id(0)
    f = pl.program_id(1)
    nf = pl.num_programs(1)
    valid = bv_ref[b] != 0

    @pl.when(jnp.logical_and(valid, f == 0))
    def _():
        xb_ref[...] = x_ref[...].astype(BF16)

    @pl.when(valid)
    def _():
        x = xb_ref[...]
        g = _dot(x, wg_ref[...])
        hcol = (g * jax.nn.sigmoid(g) * _dot(x, wu_ref[...])).astype(BF16)
        part = _dot(hcol, wd_ref[...])

        @pl.when(f == 0)
        def _():
            acc_ref[...] = part

        @pl.when(jnp.logical_and(f != 0, f != nf - 1))
        def _():
            acc_ref[...] += part

        @pl.when(f == nf - 1)
        def _():
            y_ref[...] = acc_ref[...] + part

    @pl.when(jnp.logical_and(jnp.logical_not(valid), f == nf - 1))
    def _():
        y_ref[...] = jnp.zeros(y_ref.shape, F32)


def _experts(xs, wg, wu, wd, be, bv, bx):
    n_rows = xs.shape[0]
    bm, tf = MOE_BM, MOE_TF
    nf = D_FF_EXPERT // tf
    assert nf >= 2

    def fidx(b, f, bv):
        return jnp.where(bv[b] != 0, f, nf - 1)

    grid_spec = pltpu.PrefetchScalarGridSpec(
        num_scalar_prefetch=3,
        grid=(n_rows // bm, nf),
        in_specs=[pl.BlockSpec((bm, D_MODEL), lambda b, f, be, bv, bx: (bx[b], 0)),
                  pl.BlockSpec((None, D_MODEL, tf), lambda b, f, be, bv, bx: (be[b], 0, fidx(b, f, bv))),
                  pl.BlockSpec((None, D_MODEL, tf), lambda b, f, be, bv, bx: (be[b], 0, fidx(b, f, bv))),
                  pl.BlockSpec((None, tf, D_MODEL), lambda b, f, be, bv, bx: (be[b], fidx(b, f, bv), 0))],
        out_specs=pl.BlockSpec((bm, D_MODEL), lambda b, f, be, bv, bx: (b, 0)),
        scratch_shapes=[pltpu.VMEM((bm, D_MODEL), BF16),
                        pltpu.VMEM((bm, D_MODEL), F32)],
    )
    return pl.pallas_call(
        _expert_kernel,
        grid_spec=grid_spec,
        out_shape=jax.ShapeDtypeStruct((n_rows, D_MODEL), F32),
        compiler_params=_params("arbitrary", "arbitrary"),
    )(be, bv, bx, xs, wg, wu, wd)


def _combine_kernel(p1_ref, p2_ref, p1n_ref, p2n_ref, x_ref, route_ref, gn_ref, ys_hbm, out_ref,
                    yg_ref, sem, *, final_norm):
    i = pl.program_id(0)
    tc = x_ref.shape[0]
    slot = lax.rem(i, 2)

    def gather(pa_ref, pb_ref, s_, op):
        def rows(k, j):
            r = k * SUBLANES + j
            op(pltpu.make_async_copy(ys_hbm.at[pl.ds(pa_ref[0, r], 1), :],
                                     yg_ref.at[s_, 0, k, pl.ds(j, 1), :], sem.at[s_]), 0)
            op(pltpu.make_async_copy(ys_hbm.at[pl.ds(pb_ref[0, r], 1), :],
                                     yg_ref.at[s_, 1, k, pl.ds(j, 1), :], sem.at[s_]), 1)
        _for_rows(tc, rows)

    start = lambda c, queue: c.start(priority=queue)
    wait = lambda c, queue: c.wait()

    @pl.when(i == 0)
    def _():
        gather(p1_ref, p2_ref, 0, start)

    gather(p1_ref, p2_ref, slot, wait)

    @pl.when(i + 1 < pl.num_programs(0))
    def _():
        gather(p1n_ref, p2n_ref, 1 - slot, start)

    y1 = yg_ref[slot, 0].reshape(tc, D_MODEL)
    y2 = yg_ref[slot, 1].reshape(tc, D_MODEL)
    out = x_ref[...] + route_ref[:, 2:3] * y1 + route_ref[:, 3:4] * y2
    out_ref[...] = _rms(out, gn_ref[...]) if final_norm else out


def _combine(x1, ys, p1, p2, route, gn, final_norm):
    s = x1.shape[0]
    tc = MOE_TC
    nt = s // tc
    row = lambda w: pl.BlockSpec((tc, w), lambda i: (i, 0))
    cur = lambda: pl.BlockSpec((None, 1, tc), lambda i: (i, 0, 0), memory_space=pltpu.SMEM)
    nxt = lambda: pl.BlockSpec((None, 1, tc), lambda i: (jnp.minimum(i + 1, nt - 1), 0, 0),
                               memory_space=pltpu.SMEM)
    p1 = p1.reshape(nt, 1, tc)
    p2 = p2.reshape(nt, 1, tc)
    return pl.pallas_call(
        functools.partial(_combine_kernel, final_norm=final_norm),
        grid=(nt,),
        in_specs=[cur(), cur(), nxt(), nxt(), row(D_MODEL), row(LANES), _const_spec(gn.shape),
                  pl.BlockSpec(memory_space=pl.ANY)],
        out_specs=row(D_MODEL),
        out_shape=jax.ShapeDtypeStruct((s, D_MODEL), F32),
        scratch_shapes=[pltpu.VMEM((2, 2, tc // SUBLANES, SUBLANES, D_MODEL), F32),
                        pltpu.SemaphoreType.DMA((2,))],
        compiler_params=_params("arbitrary"),
    )(p1, p2, p1, p2, x1, route, gn, ys)


def _moe_plan(route, counts, s):
    bm = MOE_BM
    nb = (2 * s) // bm + N_EXPERTS
    i1 = route[:, 0].astype(jnp.int32)
    i2 = route[:, 1].astype(jnp.int32)
    r1 = route[:, 4].astype(jnp.int32)
    r2 = route[:, 5].astype(jnp.int32)
    cnt = counts[0, :N_EXPERTS].astype(jnp.int32)
    padded = ((cnt + bm - 1) // bm) * bm
    ends = jnp.cumsum(padded)
    off = ends - padded
    p1 = off[i1] + r1
    p2 = off[i2] + r2
    nb_used = ends[-1] // bm

    bidx = jnp.arange(nb, dtype=jnp.int32)
    bvalid = (bidx < nb_used).astype(jnp.int32)
    bsrc = jnp.minimum(bidx, jnp.maximum(nb_used - 1, 0))
    bexp = jnp.sum((bsrc[:, None] * bm >= ends[None, :]).astype(jnp.int32), axis=1)
    bexp = jnp.minimum(bexp, N_EXPERTS - 1)
    real = jnp.clip(off[bexp] + cnt[bexp] - bidx * bm, 0, bm)
    bzero = ((real < bm) | (bvalid == 0)).astype(jnp.int32)
    return p1, p2, bexp, bvalid, bsrc, bzero


def _inproj_weight(w_in, q_scale):
    cols = [w_in[:, :OFF_QD], jnp.zeros((D_MODEL, LANES - 2 * MLSTM_HEADS), w_in.dtype)]
    for g in range(len(DIL_PATTERNS)):
        sl = lambda off: w_in[:, off + g * DIL_GW: off + (g + 1) * DIL_GW]
        cols += [sl(OFF_KD), sl(OFF_VD), sl(OFF_QD) * q_scale]
    cols.append(w_in[:, OFF_QX:])
    return jnp.concatenate(cols, axis=-1).astype(BF16)


def kernel(x, mem, norm_mix, w_in, conv_qk, b_gate_if, mlstm_norm, norm_mem, w_mem_kv, w_br_m, w_br_d,
           w_br_x, w_out, norm_ffn, ffn_w_gate, ffn_w_up, ffn_w_down, moe_router, moe_w_gate, moe_w_up,
           moe_w_down, norm_final):
    s = x.shape[1]
    xs = x.reshape(s, D_MODEL)
    mem2 = mem.reshape(N_MEM, D_MODEL)
    row = lambda a: a.reshape(1, -1)
    q_scale = DIL_DH ** -0.5

    for layer in range(DEPTH):
        wl = w_in[layer]
        g_mix = row(norm_mix[layer])
        km, vm = _memkv(mem2, row(norm_mem[layer]), w_mem_kv[layer].astype(BF16))

        bif = jnp.pad(b_gate_if[layer], (0, LANES - 2 * MLSTM_HEADS)).reshape(1, LANES)
        (q_m, k_m, v_m, o_m, ifc, ifr, d0, d1, d2, h_x, gates) = _inproj(
            xs, g_mix, _inproj_weight(wl, q_scale), bif, conv_qk[layer], km, vm)

        h_m = _mlstm(q_m, k_m, v_m, o_m, ifc, ifr, row(mlstm_norm[layer]))

        ods, lses = [], []
        for qkv, (_, dil) in zip((d0, d1, d2), DIL_PATTERNS):
            o_g, lse_g = _dil_attn(qkv, dil)
            ods.append(o_g)
            lses.append(lse_g)

        merge_w = (w_br_m[layer].astype(BF16), w_br_d[layer].astype(BF16), w_br_x[layer].astype(BF16),
                   w_out[layer].astype(BF16), row(norm_ffn[layer]))
        if layer % 2 == 0:
            li = layer // 2
            dense_w = (ffn_w_gate[li].astype(BF16), ffn_w_up[li].astype(BF16), ffn_w_down[li].astype(BF16))
            xs = _merge(xs, h_m, h_x, gates, ods, lses, *merge_w, dense_w=dense_w)
        else:
            li = layer // 2
            wr = jnp.pad(moe_router[li], ((0, 0), (0, LANES - N_EXPERTS)))
            x1, u, route, counts = _merge(xs, h_m, h_x, gates, ods, lses, *merge_w, w_router=wr)
            p1, p2, bexp, bvalid, bsrc, bzero = _moe_plan(route, counts, s)
            rows, wg, wu, wd = _dispatch(u, p1, p2, bzero, bexp.shape[0] * MOE_BM,
                                         moe_w_gate, moe_w_up, moe_w_down, li)
            y = _experts(rows, wg, wu, wd, bexp, bvalid, bsrc)
            xs = _combine(x1, y, p1, p2, route, row(norm_final), final_norm=layer == DEPTH - 1)
    return xs.reshape(x.shape)
```
